```python
import jax, jax.numpy as jnp
from jax import lax
import numpy as np

D_MODEL = 1024
BATCH = 32
SEQ = 256
DEPTH = 2
DEC_BATCH = 8
DEC_SEQ = 2048
PAST_LEN = 256

GRID_W = 64
ROPE_THETA = 10000.0
N_MIXERS = 2
N_HGRN_LAYERS = (DEPTH + 1) // 2
N_ATTN_LAYERS = DEPTH // 2
HGRN_DK = 128
HGRN_HEADS = D_MODEL // HGRN_DK
HGRN_DV = D_MODEL // HGRN_HEADS
HGRN_FK = HGRN_HEADS * HGRN_DK
HGRN_CHUNK = 32
HEAD_DIM = 128
N_HEADS = D_MODEL // HEAD_DIM
N_KV_HEADS = 2
GROUP = N_HEADS // N_KV_HEADS
AXIS_DIM = HEAD_DIM // 2
Q_BLOCK = 128
N_EXPERTS = 32
TOP_K = 4
D_FF = D_MODEL
SWIGLU_LIMIT = 7.0
SWIGLU_ALPHA = 1.702
MOE_BLOCK = 128
EPS = 1e-6

kernel_name = "hybrid_flow_hgrn2_gqa_moe_step"

F32 = jnp.float32


def rms_norm(x, gain):
    xf = x.astype(F32)
    y = xf * lax.rsqrt(jnp.mean(xf * xf, axis=-1, keepdims=True) + EPS)
    return (y * gain.astype(F32)).astype(x.dtype)


def modulate(x, shift, scale):
    return x * (1 + scale) + shift


def gla_chunkwise(q, k, v, log_f, s0):
    B, T, H, DK = q.shape
    DV = v.shape[-1]
    nc = T // HGRN_CHUNK
    ch = lambda a: a.reshape(B, nc, HGRN_CHUNK, H, a.shape[-1]).astype(F32)
    qc, kc, vc, gc = ch(q), ch(k), ch(v), ch(log_f)
    bcum = jnp.cumsum(gc, axis=2)
    btot = bcum[:, :, -1]
    q_dec = qc * jnp.exp(bcum)
    k_inv = kc * jnp.exp(-bcum)
    k_end = kc * jnp.exp(btot[:, :, None] - bcum)
    a = jnp.einsum('bnchk,bnshk->bnhcs', q_dec, k_inv)
    mask = jnp.tril(jnp.ones((HGRN_CHUNK, HGRN_CHUNK), dtype=bool))
    a = jnp.where(mask, a, 0.0)
    o_intra = jnp.einsum('bnhcs,bnshv->bnchv', a, vc)

    def step(s, inp):
        qd, ke, vv, bt = inp
        o = jnp.einsum('bchk,bhkv->bchv', qd, s)
        s = jnp.exp(bt)[..., None] * s + jnp.einsum('bchk,bchv->bhkv', ke, vv)
        return s, o

    mv = lambda a: jnp.moveaxis(a, 1, 0)
    s_fin, o_inter = lax.scan(step, s0.astype(F32), (mv(q_dec), mv(k_end), mv(vc), mv(btot)))
    o = o_intra + jnp.moveaxis(o_inter, 0, 1)
    return o.reshape(B, T, H, DV), s_fin


def hgrn2_mixer(h, lb, w_in, g_norm, w_out, s0_fwd, s0_bwd):
    B, T, _ = h.shape
    proj = h @ w_in
    o1, o2, o3 = HGRN_FK, 3 * HGRN_FK, 3 * HGRN_FK + HGRN_HEADS * HGRN_DV
    q = jax.nn.silu(proj[..., :o1]).reshape(B, T, HGRN_HEADS, HGRN_DK)
    f_pre = proj[..., o1:o2].reshape(B, T, 2, HGRN_FK).astype(F32)
    v = proj[..., o2:o3].reshape(B, T, HGRN_HEADS, HGRN_DV)
    g = proj[..., o3:]
    f = lb + (1.0 - lb) * jax.nn.sigmoid(f_pre)
    log_f = jnp.log(f).reshape(B, T, 2, HGRN_HEADS, HGRN_DK)
    k = (1.0 - f).reshape(B, T, 2, HGRN_HEADS, HGRN_DK)
    flip = lambda a: jnp.flip(a, axis=1)
    o_f, s_f = gla_chunkwise(q, k[:, :, 0], v, log_f[:, :, 0], s0_fwd)
    o_b, s_b = gla_chunkwise(flip(q), flip(k[:, :, 1]), flip(v), flip(log_f[:, :, 1]), s0_bwd)
    o = rms_norm(o_f + flip(o_b), g_norm).reshape(B, T, HGRN_HEADS * HGRN_DV)
    o = (o * jax.nn.silu(g.astype(F32))).astype(h.dtype)
    return o @ w_out, s_f, s_b


def axial_rope(x):
    n_tok = x.shape[1]
    rows = n_tok // GRID_W
    row_id, col_id = jnp.meshgrid(jnp.arange(rows), jnp.arange(GRID_W), indexing='ij')
    row_id = row_id.reshape(-1).astype(F32)
    col_id = col_id.reshape(-1).astype(F32)
    inv_freq = ROPE_THETA ** (-jnp.arange(0, AXIS_DIM, 2, dtype=F32) / AXIS_DIM)

    def rotate(xa, pos):
        ang = pos[:, None] * inv_freq[None, :]
        cos = jnp.cos(ang)[None, :, None, :]
        sin = jnp.sin(ang)[None, :, None, :]
        xa = xa.astype(F32)
        x1, x2 = xa[..., :AXIS_DIM // 2], xa[..., AXIS_DIM // 2:]
        return jnp.concatenate([x1 * cos - x2 * sin, x2 * cos + x1 * sin], axis=-1)

    out = jnp.concatenate([rotate(x[..., :AXIS_DIM], row_id), rotate(x[..., AXIS_DIM:], col_id)], axis=-1)
    return out.astype(x.dtype)


def attn_qkv(h, w_qkv, q_gain, k_gain):
    B, T, _ = h.shape
    qkv = h @ w_qkv
    nq, nk = N_HEADS * HEAD_DIM, N_KV_HEADS * HEAD_DIM
    q = rms_norm(qkv[..., :nq].reshape(B, T, N_HEADS, HEAD_DIM), q_gain)
    k = rms_norm(qkv[..., nq:nq + nk].reshape(B, T, N_KV_HEADS, HEAD_DIM), k_gain)
    v = qkv[..., nq + nk:].reshape(B, T, N_KV_HEADS, HEAD_DIM)
    return q, k, v


def gqa_attend(q, k, v):
    B, Tq = q.shape[:2]
    qg = q.reshape(B, Tq, N_KV_HEADS, GROUP, HEAD_DIM)
    s = jnp.einsum('bqkgd,bskd->bkgqs', qg, k).astype(F32) * (HEAD_DIM ** -0.5)
    p = jax.nn.softmax(s, axis=-1).astype(v.dtype)
    o = jnp.einsum('bkgqs,bskd->bqkgd', p, v)
    return o.reshape(B, Tq, N_HEADS * HEAD_DIM)


def attend_blocked(q, k, v):
    B, T = q.shape[:2]
    nb = T // Q_BLOCK
    qb = jnp.moveaxis(q.reshape(B, nb, Q_BLOCK, N_HEADS, HEAD_DIM), 1, 0)
    ob = lax.map(lambda qq: gqa_attend(qq, k, v), qb)
    return jnp.moveaxis(ob, 0, 1).reshape(B, T, N_HEADS * HEAD_DIM)


def moe(x, w_router, b_router, w_gate, b_gate, w_up, b_up, w_down, b_down):
    T, D = x.shape
    logits = (x @ w_router + b_router).astype(F32)
    top_val, top_idx = lax.top_k(logits, TOP_K)
    gates = jax.nn.softmax(top_val, axis=-1)
    n_assign = T * TOP_K
    expert_flat = top_idx.reshape(-1).astype(jnp.int32)
    token_flat = jnp.arange(n_assign, dtype=jnp.int32) // TOP_K
    gate_flat = gates.reshape(-1)
    order = jnp.argsort(expert_flat)
    sorted_exp = expert_flat[order]
    counts = jnp.bincount(expert_flat, length=N_EXPERTS).astype(jnp.int32)
    padded = (counts + MOE_BLOCK - 1) // MOE_BLOCK * MOE_BLOCK
    pad_end = jnp.cumsum(padded)
    pad_start = pad_end - padded
    raw_start = jnp.cumsum(counts) - counts
    dest = pad_start[sorted_exp] + jnp.arange(n_assign, dtype=jnp.int32) - raw_start[sorted_exp]
    n_blocks = (n_assign + N_EXPERTS * (MOE_BLOCK - 1) + MOE_BLOCK - 1) // MOE_BLOCK
    n_slots = n_blocks * MOE_BLOCK
    slot_tok = jnp.zeros((n_slots,), jnp.int32).at[dest].set(token_flat[order])
    slot_gate = jnp.zeros((n_slots,), F32).at[dest].set(gate_flat[order])
    block_start = jnp.arange(n_blocks, dtype=jnp.int32) * MOE_BLOCK
    block_exp = jnp.minimum(jnp.searchsorted(pad_end, block_start, side='right'), N_EXPERTS - 1)

    def block_fn(args):
        e, tok, g = args
        xb = x[tok]
        hg = jnp.minimum(xb @ w_gate[e] + b_gate[e], SWIGLU_LIMIT)
        hu = jnp.clip(xb @ w_up[e] + b_up[e], -SWIGLU_LIMIT, SWIGLU_LIMIT)
        hh = (hu + 1) * hg * jax.nn.sigmoid(SWIGLU_ALPHA * hg)
        y = hh @ w_down[e] + b_down[e]
        return y.astype(F32) * g[:, None]

    y_slots = lax.map(block_fn, (block_exp, slot_tok.reshape(n_blocks, MOE_BLOCK),
                                 slot_gate.reshape(n_blocks, MOE_BLOCK)))
    out = jnp.zeros((T, D), F32).at[slot_tok].add(y_slots.reshape(n_slots, D))
    return out.astype(x.dtype)


def setup_inputs(seed: int = 0) -> dict:
    key = jax.random.key(seed)
    ks = jax.random.split(key, 40)
    D = D_MODEL
    nrm = lambda k, shape, s: jax.random.normal(k, shape, F32) * s
    gain = lambda k, shape: 1.0 + 0.05 * jax.random.normal(k, shape, F32)
    qkv_cols = (N_HEADS + 2 * N_KV_HEADS) * HEAD_DIM
    hgrn_cols = 3 * HGRN_FK + 2 * HGRN_HEADS * HGRN_DV
    return {
        "x_prompt": nrm(ks[0], (BATCH, SEQ, D), 1.0),
        "x_sample": nrm(ks[1], (DEC_BATCH, DEC_SEQ, D), 1.0),
        "state_hgrn": nrm(ks[2], (DEC_BATCH, N_HGRN_LAYERS, 2, HGRN_HEADS, HGRN_DK, HGRN_DV), 0.5),
        "cache_k": nrm(ks[3], (DEC_BATCH, N_ATTN_LAYERS, PAST_LEN, N_KV_HEADS, HEAD_DIM), 1.0),
        "cache_v": nrm(ks[4], (DEC_BATCH, N_ATTN_LAYERS, PAST_LEN, N_KV_HEADS, HEAD_DIM), 1.0),
        "c": nrm(ks[5], (DEC_BATCH, D), 1.0),
        "c_ctx": nrm(ks[6], (D,), 1.0),
        "w_mod": nrm(ks[7], (DEPTH, D, 6 * D), 0.5 * D ** -0.5),
        "b_mod": nrm(ks[8], (DEPTH, 6 * D), 0.02),
        "g_pre_mix": gain(ks[9], (DEPTH, D)),
        "g_post_mix": gain(ks[10], (DEPTH, D)),
        "g_pre_ffn": gain(ks[11], (DEPTH, D)),
        "g_post_ffn": gain(ks[12], (DEPTH, D)),
        "hgrn_w_in": nrm(ks[13], (N_HGRN_LAYERS, D, hgrn_cols), D ** -0.5),
        "hgrn_gamma": nrm(ks[14], (DEPTH + 1, 2, HGRN_FK), 0.1),
        "hgrn_g_norm": gain(ks[15], (N_HGRN_LAYERS, HGRN_DV)),
        "hgrn_w_out": nrm(ks[16], (N_HGRN_LAYERS, HGRN_HEADS * HGRN_DV, D), (HGRN_HEADS * HGRN_DV) ** -0.5),
        "attn_w_qkv": nrm(ks[17], (N_ATTN_LAYERS, D, qkv_cols), D ** -0.5),
        "attn_q_gain": gain(ks[18], (N_ATTN_LAYERS, HEAD_DIM)),
        "attn_k_gain": gain(ks[19], (N_ATTN_LAYERS, HEAD_DIM)),
        "attn_w_o": nrm(ks[20], (N_ATTN_LAYERS, N_HEADS * HEAD_DIM, D), (N_HEADS * HEAD_DIM) ** -0.5),
        "moe_w_router": nrm(ks[21], (DEPTH, D, N_EXPERTS), D ** -0.5),
        "moe_b_router": nrm(ks[22], (DEPTH, N_EXPERTS), 0.01),
        "moe_w_gate": nrm(ks[23], (DEPTH, N_EXPERTS, D, D_FF), D ** -0.5),
        "moe_b_gate": nrm(ks[24], (DEPTH, N_EXPERTS, D_FF), 0.02),
        "moe_w_up": nrm(ks[25], (DEPTH, N_EXPERTS, D, D_FF), D ** -0.5),
        "moe_b_up": nrm(ks[26], (DEPTH, N_EXPERTS, D_FF), 0.02),
        "moe_w_down": nrm(ks[27], (DEPTH, N_EXPERTS, D_FF, D), D_FF ** -0.5),
        "moe_b_down": nrm(ks[28], (DEPTH, N_EXPERTS, D), 0.02),
    }


def reference(x_prompt, x_sample, state_hgrn, cache_k, cache_v, c, c_ctx,
              w_mod, b_mod, g_pre_mix, g_post_mix, g_pre_ffn, g_post_ffn,
              hgrn_w_in, hgrn_gamma, hgrn_g_norm, hgrn_w_out,
              attn_w_qkv, attn_q_gain, attn_k_gain, attn_w_o,
              moe_w_router, moe_b_router, moe_w_gate, moe_b_gate,
              moe_w_up, moe_b_up, moe_w_down, moe_b_down):
    D = D_MODEL
    xp, xs = x_prompt, x_sample
    Bp, Tp = xp.shape[:2]
    Bs, Ts = xs.shape[:2]
    lb_all = jnp.cumsum(jax.nn.softmax(hgrn_gamma.astype(F32), axis=0), axis=0)
    new_states, new_k, new_v = [], [], []
    for l in range(DEPTH):
        mod_ctx = (jax.nn.silu(c_ctx) @ w_mod[l] + b_mod[l]).reshape(6, 1, 1, D)
        mod_lat = (jax.nn.silu(c) @ w_mod[l] + b_mod[l]).reshape(Bs, 1, 6, D)
        ml = [mod_lat[:, :, j] for j in range(6)]
        h_ctx = modulate(rms_norm(xp, g_pre_mix[l]), mod_ctx[0], mod_ctx[1])
        h_lat = modulate(rms_norm(xs, g_pre_mix[l]), ml[0], ml[1])
        if l % N_MIXERS == 0:
            a = l // N_MIXERS
            zero = jnp.zeros((Bp, HGRN_HEADS, HGRN_DK, HGRN_DV), F32)
            out_ctx, s_f, s_b = hgrn2_mixer(h_ctx, lb_all[l], hgrn_w_in[a], hgrn_g_norm[a],
                                            hgrn_w_out[a], zero, zero)
            out_lat, _, _ = hgrn2_mixer(h_lat, lb_all[l], hgrn_w_in[a], hgrn_g_norm[a],
                                        hgrn_w_out[a], state_hgrn[:, a, 0], state_hgrn[:, a, 1])
            new_states.append(jnp.stack([s_f, s_b], axis=1).astype(xp.dtype))
        else:
            m = l // N_MIXERS
            q_c, k_c, v_c = attn_qkv(h_ctx, attn_w_qkv[m], attn_q_gain[m], attn_k_gain[m])
            out_ctx = attend_blocked(q_c, k_c, v_c) @ attn_w_o[m]
            new_k.append(k_c)
            new_v.append(v_c)
            q_l, k_l, v_l = attn_qkv(h_lat, attn_w_qkv[m], attn_q_gain[m], attn_k_gain[m])
            q_l, k_l = axial_rope(q_l), axial_rope(k_l)
            k_all = jnp.concatenate([cache_k[:, m].astype(k_l.dtype), k_l], axis=1)
            v_all = jnp.concatenate([cache_v[:, m].astype(v_l.dtype), v_l], axis=1)
            out_lat = attend_blocked(q_l, k_all, v_all) @ attn_w_o[m]
        xp = xp + mod_ctx[2] * rms_norm(out_ctx, g_post_mix[l])
        xs = xs + ml[2] * rms_norm(out_lat, g_post_mix[l])
        h_ctx = modulate(rms_norm(xp, g_pre_ffn[l]), mod_ctx[3], mod_ctx[4])
        h_lat = modulate(rms_norm(xs, g_pre_ffn[l]), ml[3], ml[4])
        flat = jnp.concatenate([h_ctx.reshape(-1, D), h_lat.reshape(-1, D)], axis=0)
        y = moe(flat, moe_w_router[l], moe_b_router[l], moe_w_gate[l], moe_b_gate[l],
                moe_w_up[l], moe_b_up[l], moe_w_down[l], moe_b_down[l])
        y_ctx = y[:Bp * Tp].reshape(Bp, Tp, D)
        y_lat = y[Bp * Tp:].reshape(Bs, Ts, D)
        xp = xp + mod_ctx[5] * rms_norm(y_ctx, g_post_ffn[l])
        xs = xs + ml[5] * rms_norm(y_lat, g_post_ffn[l])
    new_state_hgrn = jnp.stack(new_states, axis=1)
    new_cache_k = jnp.stack(new_k, axis=1)
    new_cache_v = jnp.stack(new_v, axis=1)
    return (xp, xs, new_state_hgrn, new_cache_k, new_cache_v)
```

```python
import functools

import numpy as np
import jax
import jax.numpy as jnp
from jax import lax
from jax.experimental import pallas as pl
from jax.experimental.pallas import tpu as pltpu

F32 = jnp.float32
BF16 = jnp.bfloat16
I32 = jnp.int32

EPS = 1e-6
GRID_W = 64
ROPE_THETA = 10000.0
HGRN_DK = 128
HEAD_DIM = 128
N_KV_HEADS = 2
N_EXPERTS = 32
TOP_K = 4
SWIGLU_LIMIT = 7.0
SWIGLU_ALPHA = 1.702

LANES = 128
TM = 256
GLA_CHUNK = 64
MOE_BM = 256
MOD_COLS = 512
VMEM_LIMIT = 56 * 1024 * 1024
NEG_BIG = -1e30

NT_DIMS = (((1,), (1,)), ((), ()))
TN_DIMS = (((0,), (0,)), ((), ()))


def _params(*sem):
    return pltpu.CompilerParams(dimension_semantics=sem, vmem_limit_bytes=VMEM_LIMIT)


def _sigmoid(x):
    return 1.0 / (1.0 + jnp.exp(-x))


def _silu(x):
    return x * _sigmoid(x)


def _rms(x, gain):
    ms = jnp.mean(x * x, axis=-1, keepdims=True)
    return x * lax.rsqrt(ms + EPS) * gain


def _dot(a, b):
    return jnp.dot(a, b, preferred_element_type=F32)


def _mod_kernel(c_ref, w_ref, b_ref, o_ref):
    s = _silu(c_ref[...]).astype(BF16)
    o_ref[...] = _dot(s, w_ref[...].astype(BF16)) + b_ref[...]


def _mod_table(cvec, w_mod, b_mod):
    depth, d, n = w_mod.shape
    rows = cvec.shape[0]
    return pl.pallas_call(
        _mod_kernel,
        grid=(depth, n // MOD_COLS),
        in_specs=[
            pl.BlockSpec((rows, d), lambda l, j: (0, 0)),
            pl.BlockSpec((None, d, MOD_COLS), lambda l, j: (l, 0, j)),
            pl.BlockSpec((None, 1, MOD_COLS), lambda l, j: (l, 0, j)),
        ],
        out_specs=pl.BlockSpec((None, rows, MOD_COLS), lambda l, j: (l, 0, j)),
        out_shape=jax.ShapeDtypeStruct((depth, rows, n), F32),
        compiler_params=_params("arbitrary", "arbitrary"),
        name="mod_table",
    )(cvec, w_mod, b_mod.reshape(depth, 1, n))


def _gla_tile(q_s, k_s, lf_s, v_s, st_ref, o_s, direction):
    c = GLA_CHUNK
    d_model = q_s.shape[1]
    n_heads = d_model // HGRN_DK
    ri = lax.broadcasted_iota(I32, (c, c), 0)
    ci = lax.broadcasted_iota(I32, (c, c), 1)
    keep = (ci <= ri) if direction == 0 else (ci >= ri)
    ones_tri = jnp.where(keep, 1.0, 0.0).astype(BF16)
    chunks = range(TM // c)
    if direction == 1:
        chunks = reversed(chunks)
    for ch in chunks:
        rows = pl.ds(ch * c, c)
        lf = lf_s[rows, :]
        lf_hi = lf.astype(BF16)
        lf_lo = (lf - lf_hi.astype(F32)).astype(BF16)
        cum = _dot(ones_tri, lf_hi) + _dot(ones_tri, lf_lo)
        if direction == 0:
            btot, bmid = cum[c - 1:c], cum[c // 2 - 1:c // 2]
        else:
            btot, bmid = cum[0:1], cum[c // 2:c // 2 + 1]
        q_dec = q_s[rows, :] * jnp.exp(cum - bmid)
        k_inv = k_s[rows, :] * jnp.exp(bmid - cum)
        q_in = (q_dec * jnp.exp(bmid)).astype(BF16)
        k_end = (k_inv * jnp.exp(btot - bmid)).astype(BF16)
        q_dec = q_dec.astype(BF16)
        k_inv = k_inv.astype(BF16)
        e_tot = jnp.exp(btot)
        vc = v_s[rows, :].astype(BF16)
        for h in range(n_heads):
            sl = slice(h * HGRN_DK, (h + 1) * HGRN_DK)
            a = lax.dot_general(q_dec[:, sl], k_inv[:, sl], NT_DIMS, preferred_element_type=F32)
            a = jnp.where(keep, a, 0.0).astype(BF16)
            st = st_ref[h]
            o = _dot(a, vc[:, sl]) + lax.dot_general(
                q_in[:, sl], st.astype(BF16), NT_DIMS, preferred_element_type=F32)
            o_s[rows, sl] = o
            st_ref[h] = st * e_tot[:, sl] + lax.dot_general(
                vc[:, sl], k_end[:, sl], TN_DIMS, preferred_element_type=F32)


def _hgrn_gates(h, w_ref, lb, q_s, k_s, lf_s, v_s):
    d = h.shape[1]
    q_s[...] = _silu(_dot(h, w_ref[:, 0:d]))
    f = lb + (1.0 - lb) * _sigmoid(_dot(h, w_ref[:, d:2 * d]))
    lf_s[...] = jnp.log(f)
    k_s[...] = 1.0 - f
    v_s[...] = _dot(h, w_ref[:, 2 * d:3 * d])


def _load_state(meta, s0_ref, st_s):
    @pl.when(meta[2, pl.program_id(0)] == 1)
    def _():
        for h in range(st_s.shape[0]):
            st_s[h] = s0_ref[h].T


def _store_state(meta, sfin_ref, st_s):
    @pl.when(meta[3, pl.program_id(0)] == 1)
    def _():
        for h in range(st_s.shape[0]):
            sfin_ref[h] = st_s[h].T


def _hgrn_bwd_kernel(meta, x_ref, mod_ref, gpre_ref, w_ref, lb_ref, s0_ref,
                     ob_ref, sfin_ref, st_s, q_s, k_s, lf_s, v_s, o_s):
    _load_state(meta, s0_ref, st_s)
    h = (_rms(x_ref[...], gpre_ref[...]) * (1.0 + mod_ref[1:2]) + mod_ref[0:1]).astype(BF16)
    _hgrn_gates(h, w_ref, lb_ref[...], q_s, k_s, lf_s, v_s)
    _gla_tile(q_s, k_s, lf_s, v_s, st_s, o_s, 1)
    ob_ref[...] = o_s[...].astype(BF16)
    _store_state(meta, sfin_ref, st_s)


def _hgrn_fwd_kernel(meta, x_ref, mod_ref, gpre_ref, w_ref, lb_ref, s0_ref, ob_ref,
                     gnorm_ref, wout_ref, gpost_ref,
                     xo_ref, sfin_ref, st_s, q_s, k_s, lf_s, v_s, o_s):
    _load_state(meta, s0_ref, st_s)
    x = x_ref[...]
    d = x.shape[1]
    h = (_rms(x, gpre_ref[...]) * (1.0 + mod_ref[1:2]) + mod_ref[0:1]).astype(BF16)
    _hgrn_gates(h, w_ref, lb_ref[...], q_s, k_s, lf_s, v_s)
    _gla_tile(q_s, k_s, lf_s, v_s, st_s, o_s, 0)
    _store_state(meta, sfin_ref, st_s)
    gate = _silu(_dot(h, w_ref[:, 3 * d:4 * d]))
    gnorm = gnorm_ref[...]
    for hh in range(d // HGRN_DK):
        sl = slice(hh * HGRN_DK, (hh + 1) * HGRN_DK)
        o = o_s[:, sl] + ob_ref[:, sl].astype(F32)
        o_s[:, sl] = _rms(o, gnorm) * gate[:, sl]
    out = _dot(o_s[...].astype(BF16), wout_ref[...])
    xo_ref[...] = x + mod_ref[2:3] * _rms(out, gpost_ref[...])


def _seq_meta(n_ctx_seq, ctx_tiles, n_lat_seq, lat_tiles, direction):
    cols = []
    tile0 = 0
    for s in range(n_ctx_seq + n_lat_seq):
        n = ctx_tiles if s < n_ctx_seq else lat_tiles
        grp = 0 if s < n_ctx_seq else 1 + s - n_ctx_seq
        order = range(n) if direction == 0 else range(n - 1, -1, -1)
        for j, t in enumerate(order):
            cols.append((tile0 + t, s, int(j == 0), int(j == n - 1), grp))
        tile0 += n
    return jnp.asarray(np.array(cols, dtype=np.int32).T)


def _tile_groups(n_ctx_tiles, n_lat_seq, lat_tiles):
    g = [0] * n_ctx_tiles
    for b in range(n_lat_seq):
        g += [1 + b] * lat_tiles
    return jnp.asarray(np.array(g, dtype=np.int32))


def _hgrn_layer(x, mod, g_pre, g_post, w_in, lb, g_norm, w_out, s0, geom):
    t, d = x.shape
    n_heads = d // HGRN_DK
    n_seq = s0.shape[0]
    n_tiles = t // TM
    w_bwd = jnp.concatenate([w_in[:, 0:d], w_in[:, 2 * d:3 * d], w_in[:, 3 * d:4 * d]], axis=1).astype(BF16)
    w_fwd = jnp.concatenate([w_in[:, 0:d], w_in[:, d:2 * d], w_in[:, 3 * d:5 * d]], axis=1).astype(BF16)
    state_spec = pl.BlockSpec((None, n_heads, HGRN_DK, HGRN_DK), lambda i, m: (m[1, i], 0, 0, 0))
    tile_spec = pl.BlockSpec((TM, d), lambda i, m: (m[0, i], 0))
    mod_spec = pl.BlockSpec((None, 6, d), lambda i, m: (m[4, i], 0, 0))
    vec_spec = pl.BlockSpec((1, d), lambda i, m: (0, 0))
    scratch = [pltpu.VMEM((n_heads, HGRN_DK, HGRN_DK), F32)] + [pltpu.VMEM((TM, d), F32)] * 5

    meta_b = _seq_meta(*geom, 1)
    o_bwd, s_bwd = pl.pallas_call(
        _hgrn_bwd_kernel,
        grid_spec=pltpu.PrefetchScalarGridSpec(
            num_scalar_prefetch=1, grid=(n_tiles,),
            in_specs=[tile_spec, mod_spec, vec_spec,
                      pl.BlockSpec((d, 3 * d), lambda i, m: (0, 0)), vec_spec, state_spec],
            out_specs=[tile_spec, state_spec],
            scratch_shapes=scratch),
        out_shape=[jax.ShapeDtypeStruct((t, d), BF16),
                   jax.ShapeDtypeStruct((n_seq, n_heads, HGRN_DK, HGRN_DK), F32)],
        compiler_params=_params("arbitrary"),
        name="hgrn_bwd",
    )(meta_b, x, mod, g_pre.reshape(1, d), w_bwd, lb[1].reshape(1, d), s0[:, 1])

    meta_f = _seq_meta(*geom, 0)
    x_new, s_fwd = pl.pallas_call(
        _hgrn_fwd_kernel,
        grid_spec=pltpu.PrefetchScalarGridSpec(
            num_scalar_prefetch=1, grid=(n_tiles,),
            in_specs=[tile_spec, mod_spec, vec_spec,
                      pl.BlockSpec((d, 4 * d), lambda i, m: (0, 0)), vec_spec, state_spec,
                      tile_spec, pl.BlockSpec((1, HGRN_DK), lambda i, m: (0, 0)),
                      pl.BlockSpec((d, d), lambda i, m: (0, 0)), vec_spec],
            out_specs=[tile_spec, state_spec],
            scratch_shapes=scratch),
        out_shape=[jax.ShapeDtypeStruct((t, d), F32),
                   jax.ShapeDtypeStruct((n_seq, n_heads, HGRN_DK, HGRN_DK), F32)],
        input_output_aliases={1: 0},
        compiler_params=_params("arbitrary"),
        name="hgrn_fwd",
    )(meta_f, x, mod, g_pre.reshape(1, d), w_fwd, lb[0].reshape(1, d), s0[:, 0],
      o_bwd, g_norm.reshape(1, HGRN_DK), w_out.astype(BF16), g_post.reshape(1, d))
    return x_new, s_fwd, s_bwd


def _rope_tables(n_tok):
    axis_dim = HEAD_DIM // 2
    half = axis_dim // 2
    pos = np.arange(n_tok)
    row, col = pos // GRID_W, pos % GRID_W
    inv_freq = ROPE_THETA ** (-np.arange(0, axis_dim, 2, dtype=np.float64) / axis_dim)
    lane = np.arange(HEAD_DIM)
    p = np.where(lane[None, :] < axis_dim, row[:, None], col[:, None]).astype(np.float64)
    ang = p * inv_freq[(lane % axis_dim) % half][None, :]
    sign = np.where((lane % axis_dim) < half, -1.0, 1.0)[None, :]
    return jnp.asarray(np.cos(ang), F32), jnp.asarray(np.sin(ang) * sign, F32)


def _rope(xh, cos, sin_signed):
    axis_dim = HEAD_DIM // 2
    half = axis_dim // 2
    lane = lax.broadcasted_iota(I32, xh.shape, 1)
    upper = pltpu.roll(xh, HEAD_DIM - half, 1)
    lower = pltpu.roll(xh, half, 1)
    partner = jnp.where((lane % axis_dim) < half, upper, lower)
    return xh * cos + partner * sin_signed


def _qkv_kernel(grp, x_ref, mod_ref, gpre_ref, w_ref, qg_ref, kg_ref, *rest, rope):
    del grp
    if rope:
        cos_ref, sin_ref, q_out, k_out, v_out = rest
    else:
        q_out, k_out, v_out = rest
    d = x_ref.shape[1]
    nk = N_KV_HEADS * HEAD_DIM
    h = (_rms(x_ref[...], gpre_ref[...]) * (1.0 + mod_ref[1:2]) + mod_ref[0:1]).astype(BF16)
    q = _dot(h, w_ref[:, 0:d])
    k = _dot(h, w_ref[:, d:d + nk])
    v_out[...] = _dot(h, w_ref[:, d + nk:d + 2 * nk]).astype(v_out.dtype)
    scale = HEAD_DIM ** -0.5
    for hh in range(d // HEAD_DIM):
        sl = slice(hh * HEAD_DIM, (hh + 1) * HEAD_DIM)
        qh = _rms(q[:, sl], qg_ref[...])
        if rope:
            qh = _rope(qh, cos_ref[...], sin_ref[...])
        q_out[:, sl] = (qh * scale).astype(q_out.dtype)
    for hh in range(N_KV_HEADS):
        sl = slice(hh * HEAD_DIM, (hh + 1) * HEAD_DIM)
        kh = _rms(k[:, sl], kg_ref[...])
        if rope:
            kh = _rope(kh, cos_ref[...], sin_ref[...])
        k_out[:, sl] = kh.astype(k_out.dtype)


def _qkv(x, mod, grp, g_pre, w_qkv, q_gain, k_gain, tile0, n_tiles, rope_len, kv_dtype):
    t, d = x.shape
    nk = N_KV_HEADS * HEAD_DIM
    n_out = n_tiles * TM
    tile_in = pl.BlockSpec((TM, d), lambda i, g: (tile0 + i, 0))
    in_specs = [tile_in,
                pl.BlockSpec((None, 6, d), lambda i, g: (g[tile0 + i], 0, 0)),
                pl.BlockSpec((1, d), lambda i, g: (0, 0)),
                pl.BlockSpec((d, d + 2 * nk), lambda i, g: (0, 0)),
                pl.BlockSpec((1, HEAD_DIM), lambda i, g: (0, 0)),
                pl.BlockSpec((1, HEAD_DIM), lambda i, g: (0, 0))]
    args = [grp, x, mod, g_pre.reshape(1, d), w_qkv, q_gain.reshape(1, HEAD_DIM), k_gain.reshape(1, HEAD_DIM)]
    if rope_len:
        per_seq = rope_len // TM
        cos, sin = _rope_tables(rope_len)
        in_specs += [pl.BlockSpec((TM, HEAD_DIM), lambda i, g: (i % per_seq, 0))] * 2
        args += [cos, sin]
    return pl.pallas_call(
        functools.partial(_qkv_kernel, rope=bool(rope_len)),
        grid_spec=pltpu.PrefetchScalarGridSpec(
            num_scalar_prefetch=1, grid=(n_tiles,), in_specs=in_specs,
            out_specs=[pl.BlockSpec((TM, d), lambda i, g: (i, 0)),
                       pl.BlockSpec((TM, nk), lambda i, g: (i, 0)),
                       pl.BlockSpec((TM, nk), lambda i, g: (i, 0))]),
        out_shape=[jax.ShapeDtypeStruct((n_out, d), BF16),
                   jax.ShapeDtypeStruct((n_out, nk), kv_dtype),
                   jax.ShapeDtypeStruct((n_out, nk), kv_dtype)],
        compiler_params=_params("arbitrary"),
        name="qkv_rope" if rope_len else "qkv",
    )(*args)


def _attn_kernel(q_ref, *rest, n_pieces):
    kv, o_ref = rest[:-1], rest[-1]
    ks = [kv[2 * p][...].astype(BF16) for p in range(n_pieces)]
    vs = [kv[2 * p + 1][...].astype(BF16) for p in range(n_pieces)]
    group = q_ref.shape[1] // HEAD_DIM
    for g in range(group):
        sl = slice(g * HEAD_DIM, (g + 1) * HEAD_DIM)
        qh = q_ref[:, sl]
        ss = [lax.dot_general(qh, kp, NT_DIMS, preferred_element_type=F32) for kp in ks]
        m = ss[0].max(axis=-1, keepdims=True)
        for s in ss[1:]:
            m = jnp.maximum(m, s.max(axis=-1, keepdims=True))
        ps = [jnp.exp(s - m) for s in ss]
        denom = ps[0].sum(axis=-1, keepdims=True)
        for p in ps[1:]:
            denom = denom + p.sum(axis=-1, keepdims=True)
        o = _dot(ps[0].astype(BF16), vs[0])
        for p, vp in zip(ps[1:], vs[1:]):
            o = o + _dot(p.astype(BF16), vp)
        o_ref[:, sl] = (o / denom).astype(o_ref.dtype)


def _attention(q, k_new, v_new, n_seq, seq_len, cache=None):
    t, d = q.shape
    group_w = d // N_KV_HEADS
    nq = seq_len // TM
    q_spec = pl.BlockSpec((TM, group_w), lambda b, i, kv: (b * nq + i, kv))
    new_spec = pl.BlockSpec((seq_len, HEAD_DIM), lambda b, i, kv: (b, kv))
    in_specs, args = [q_spec], [q]
    if cache is not None:
        past = cache[0].shape[1]
        c_spec = pl.BlockSpec((None, past, HEAD_DIM), lambda b, i, kv: (b, 0, kv))
        in_specs += [c_spec, c_spec]
        args += list(cache)
    in_specs += [new_spec, new_spec]
    args += [k_new, v_new]
    return pl.pallas_call(
        functools.partial(_attn_kernel, n_pieces=len(args) // 2),
        grid=(n_seq, nq, N_KV_HEADS),
        in_specs=in_specs,
        out_specs=q_spec,
        out_shape=jax.ShapeDtypeStruct((t, d), BF16),
        compiler_params=_params("arbitrary", "arbitrary", "arbitrary"),
        name="attention",
    )(*args)


def _attn_out_kernel(grp, x_ref, o_ref, mod_ref, w_ref, gpost_ref, xo_ref):
    del grp
    out = _dot(o_ref[...], w_ref[...])
    xo_ref[...] = x_ref[...] + mod_ref[2:3] * _rms(out, gpost_ref[...])


def _attn_out(x, o, mod, grp, w_o, g_post):
    t, d = x.shape
    tile = pl.BlockSpec((TM, d), lambda i, g: (i, 0))
    return pl.pallas_call(
        _attn_out_kernel,
        grid_spec=pltpu.PrefetchScalarGridSpec(
            num_scalar_prefetch=1, grid=(t // TM,),
            in_specs=[tile, tile,
                      pl.BlockSpec((None, 6, d), lambda i, g: (g[i], 0, 0)),
                      pl.BlockSpec((d, d), lambda i, g: (0, 0)),
                      pl.BlockSpec((1, d), lambda i, g: (0, 0))],
            out_specs=tile),
        out_shape=jax.ShapeDtypeStruct((t, d), F32),
        input_output_aliases={1: 0},
        compiler_params=_params("arbitrary"),
        name="attn_out",
    )(grp, x, o, mod, w_o, g_post.reshape(1, d))


def _router_kernel(grp, x_ref, mod_ref, g_ref, wr_ref, br_ref,
                   h_ref, route_ref, gate_ref, cnt_ref, carry):
    del grp
    @pl.when(pl.program_id(0) == 0)
    def _():
        carry[...] = jnp.zeros_like(carry)
    h = _rms(x_ref[...], g_ref[...]) * (1.0 + mod_ref[4:5]) + mod_ref[3:4]
    h_ref[...] = h
    logits = _dot(h.astype(BF16), wr_ref[...]) + br_ref[...]
    lane = lax.broadcasted_iota(I32, logits.shape, 1)
    lane_f = lane.astype(F32)
    vals, idxs = [], []
    work = logits
    for _ in range(TOP_K):
        m = work.max(axis=-1, keepdims=True)
        idx = jnp.where(work == m, lane_f, float(LANES)).min(axis=-1, keepdims=True).astype(I32)
        vals.append(m)
        idxs.append(idx)
        work = jnp.where(lane == idx, -jnp.inf, work)
    es = [jnp.exp(v - vals[0]) for v in vals]
    denom = es[0]
    for e in es[1:]:
        denom = denom + e
    onehot = jnp.zeros(logits.shape, F32)
    for idx in idxs:
        onehot = onehot + jnp.where(lane == idx, 1.0, 0.0)
    ri = lax.broadcasted_iota(I32, (TM, TM), 0)
    ci = lax.broadcasted_iota(I32, (TM, TM), 1)
    before = jnp.where(ci < ri, 1.0, 0.0).astype(BF16)
    rank_all = _dot(before, onehot.astype(BF16)) + carry[0:1]
    route = jnp.zeros(logits.shape, I32)
    gate = jnp.zeros(logits.shape, F32)
    for kk in range(TOP_K):
        rank = jnp.where(lane == idxs[kk], rank_all, 0.0).sum(axis=-1, keepdims=True)
        route = jnp.where(lane == kk, idxs[kk], route)
        route = jnp.where(lane == TOP_K + kk, rank.astype(I32), route)
        gate = jnp.where(lane == kk, es[kk] / denom, gate)
    route_ref[...] = route
    gate_ref[...] = gate
    carry[...] = carry[...] + onehot.sum(axis=0, keepdims=True)
    cnt_ref[...] = carry[...]


def _router(x, mod, grp, g_pre, w_router, b_router):
    t, d = x.shape
    n_exp = w_router.shape[1]
    wr = jnp.zeros((d, LANES), BF16).at[:, :n_exp].set(w_router.astype(BF16))
    br = jnp.full((1, LANES), NEG_BIG, F32).at[0, :n_exp].set(b_router)
    tile = pl.BlockSpec((TM, d), lambda i, g: (i, 0))
    lane_tile = pl.BlockSpec((TM, LANES), lambda i, g: (i, 0))
    return pl.pallas_call(
        _router_kernel,
        grid_spec=pltpu.PrefetchScalarGridSpec(
            num_scalar_prefetch=1, grid=(t // TM,),
            in_specs=[tile,
                      pl.BlockSpec((None, 6, d), lambda i, g: (g[i], 0, 0)),
                      pl.BlockSpec((1, d), lambda i, g: (0, 0)),
                      pl.BlockSpec((d, LANES), lambda i, g: (0, 0)),
                      pl.BlockSpec((1, LANES), lambda i, g: (0, 0))],
            out_specs=[tile, lane_tile, lane_tile,
                       pl.BlockSpec((8, LANES), lambda i, g: (0, 0))],
            scratch_shapes=[pltpu.VMEM((8, LANES), F32)]),
        out_shape=[jax.ShapeDtypeStruct((t, d), F32),
                   jax.ShapeDtypeStruct((t, LANES), I32),
                   jax.ShapeDtypeStruct((t, LANES), F32),
                   jax.ShapeDtypeStruct((8, LANES), F32)],
        compiler_params=_params("arbitrary"),
        name="router",
    )(grp, x, mod, g_pre.reshape(1, d), wr, br)


def _row_copy(src, src_row, dst, dst_row, sem):
    return pltpu.make_async_copy(src.at[pl.ds(src_row, 1)], dst.at[pl.ds(dst_row, 1)], sem)


def _dispatch_kernel(pad_lo, pad_hi, dest_ref, h_hbm, xs_hbm, zero_s, sem, zsem):
    i = pl.program_id(0)
    base = i * TM

    def issue(t, c):
        for kk in range(TOP_K):
            _row_copy(h_hbm, base + t, xs_hbm, dest_ref[kk, t], sem).start()
        return c
    lax.fori_loop(0, TM, issue, 0)

    @pl.when(i == 0)
    def _():
        zero_s[...] = jnp.zeros_like(zero_s)

        def per_expert(e, c):
            def zstart(s, c2):
                _row_copy(zero_s, 0, xs_hbm, s, zsem).start()
                return c2
            lax.fori_loop(pad_lo[e], pad_hi[e], zstart, 0)

            def zwait(s, c2):
                _row_copy(zero_s, 0, xs_hbm, s, zsem).wait()
                return c2
            lax.fori_loop(pad_lo[e], pad_hi[e], zwait, 0)
            return c
        lax.fori_loop(0, pad_lo.shape[0], per_expert, 0)

    def drain(t, c):
        for kk in range(TOP_K):
            _row_copy(h_hbm, base + t, xs_hbm, dest_ref[kk, t], sem).wait()
        return c
    lax.fori_loop(0, TM, drain, 0)


def _dispatch(h, dest_t, pad_lo, pad_hi, n_slots):
    t, d = h.shape
    return pl.pallas_call(
        _dispatch_kernel,
        grid_spec=pltpu.PrefetchScalarGridSpec(
            num_scalar_prefetch=2, grid=(t // TM,),
            in_specs=[pl.BlockSpec((TOP_K, TM), lambda i, lo, hi: (0, i), memory_space=pltpu.SMEM),
                      pl.BlockSpec(memory_space=pl.ANY)],
            out_specs=pl.BlockSpec(memory_space=pl.ANY),
            scratch_shapes=[pltpu.VMEM((8, d), F32), pltpu.SemaphoreType.DMA, pltpu.SemaphoreType.DMA]),
        out_shape=jax.ShapeDtypeStruct((n_slots, d), F32),
        compiler_params=_params("arbitrary"),
        name="moe_dispatch",
    )(pad_lo, pad_hi, dest_t, h)


def _expert_kernel(blk_e, n_used, xs_ref, wg_ref, bg_ref, wu_ref, bu_ref, wd_ref, bd_ref,
                   y_ref, wg_s, wu_s, wd_s):
    i = pl.program_id(0)
    prev = blk_e[jnp.maximum(i - 1, 0)]

    @pl.when(jnp.logical_and(i < n_used[0], jnp.logical_or(i == 0, blk_e[i] != prev)))
    def _():
        wg_s[...] = wg_ref[...].astype(BF16)
        wu_s[...] = wu_ref[...].astype(BF16)
        wd_s[...] = wd_ref[...].astype(BF16)

    @pl.when(i < n_used[0])
    def _():
        x = xs_ref[...].astype(BF16)
        hg = jnp.minimum(_dot(x, wg_s[...]) + bg_ref[...], SWIGLU_LIMIT)
        hu = jnp.clip(_dot(x, wu_s[...]) + bu_ref[...], -SWIGLU_LIMIT, SWIGLU_LIMIT)
        hh = (hu + 1.0) * hg * _sigmoid(SWIGLU_ALPHA * hg)
        y_ref[...] = _dot(hh.astype(BF16), wd_s[...]) + bd_ref[...]


def _experts(xs, blk_e, n_used, layer, w_gate, b_gate, w_up, b_up, w_down, b_down):
    n_slots, d = xs.shape
    depth, n_exp, _, f = w_gate.shape

    def blk(i, be, nu):
        return jnp.minimum(i, nu[0] - 1)

    row_spec = pl.BlockSpec((MOE_BM, d), lambda i, be, nu: (blk(i, be, nu), 0))

    def w_spec(k, n):
        return pl.BlockSpec((None, None, k, n), lambda i, be, nu: (layer, be[blk(i, be, nu)], 0, 0))

    return pl.pallas_call(
        _expert_kernel,
        grid_spec=pltpu.PrefetchScalarGridSpec(
            num_scalar_prefetch=2, grid=(n_slots // MOE_BM,),
            in_specs=[row_spec,
                      w_spec(d, f), w_spec(1, f), w_spec(d, f), w_spec(1, f), w_spec(f, d), w_spec(1, d)],
            out_specs=row_spec,
            scratch_shapes=[pltpu.VMEM((d, f), BF16), pltpu.VMEM((d, f), BF16), pltpu.VMEM((f, d), BF16)]),
        out_shape=jax.ShapeDtypeStruct((n_slots, d), F32),
        compiler_params=_params("arbitrary"),
        name="moe_experts",
    )(blk_e, n_used, xs, w_gate, b_gate.reshape(depth, n_exp, 1, f), w_up, b_up.reshape(depth, n_exp, 1, f),
      w_down, b_down.reshape(depth, n_exp, 1, d))


def _combine_kernel(grp, dest_ref, x_ref, gate_ref, mod_ref, gpost_ref, y_hbm, xo_ref, ybuf, sem):
    del grp

    def issue(t, c):
        for kk in range(TOP_K):
            _row_copy(y_hbm, dest_ref[kk, t], ybuf.at[kk], t, sem).start()
        return c
    lax.fori_loop(0, TM, issue, 0)

    def drain(t, c):
        for kk in range(TOP_K):
            _row_copy(y_hbm, dest_ref[kk, t], ybuf.at[kk], t, sem).wait()
        return c
    lax.fori_loop(0, TM, drain, 0)

    gates = gate_ref[...]
    acc = gates[:, 0:1] * ybuf[0]
    for kk in range(1, TOP_K):
        acc = acc + gates[:, kk:kk + 1] * ybuf[kk]
    xo_ref[...] = x_ref[...] + mod_ref[5:6] * _rms(acc, gpost_ref[...])


def _combine(x, y, dest_t, gates, mod, grp, g_post):
    t, d = x.shape
    tile = pl.BlockSpec((TM, d), lambda i, g: (i, 0))
    return pl.pallas_call(
        _combine_kernel,
        grid_spec=pltpu.PrefetchScalarGridSpec(
            num_scalar_prefetch=1, grid=(t // TM,),
            in_specs=[pl.BlockSpec((TOP_K, TM), lambda i, g: (0, i), memory_space=pltpu.SMEM),
                      tile,
                      pl.BlockSpec((TM, LANES), lambda i, g: (i, 0)),
                      pl.BlockSpec((None, 6, d), lambda i, g: (g[i], 0, 0)),
                      pl.BlockSpec((1, d), lambda i, g: (0, 0)),
                      pl.BlockSpec(memory_space=pl.ANY)],
            out_specs=tile,
            scratch_shapes=[pltpu.VMEM((TOP_K, TM, d), F32), pltpu.SemaphoreType.DMA]),
        out_shape=jax.ShapeDtypeStruct((t, d), F32),
        input_output_aliases={2: 0},
        compiler_params=_params("arbitrary"),
        name="moe_combine",
    )(grp, dest_t, x, gates, mod, g_post.reshape(1, d), y)


def _moe_layer(x, mod, grp, layer, g_pre, g_post, w_router, b_router,
               w_gate, b_gate, w_up, b_up, w_down, b_down):
    t, d = x.shape
    n_exp = w_router.shape[1]
    h, route, gates, cnt = _router(x, mod, grp, g_pre, w_router, b_router)
    idx = route[:, 0:TOP_K]
    rank = route[:, TOP_K:2 * TOP_K]
    counts = cnt[0, :n_exp].astype(I32)
    padded = (counts + MOE_BM - 1) // MOE_BM * MOE_BM
    pad_end = jnp.cumsum(padded)
    pad_start = pad_end - padded
    dest_t = (pad_start[idx] + rank).T
    n_blocks = t * TOP_K // MOE_BM + n_exp
    n_used = (pad_end[-1:] // MOE_BM).astype(I32)
    blk_start = jnp.arange(n_blocks, dtype=I32) * MOE_BM
    blk_e = jnp.minimum(jnp.searchsorted(pad_end, blk_start, side="right"), n_exp - 1).astype(I32)
    xs = _dispatch(h, dest_t, pad_start + counts, pad_end, n_blocks * MOE_BM)
    y = _experts(xs, blk_e, n_used, layer, w_gate, b_gate, w_up, b_up, w_down, b_down)
    return _combine(x, y, dest_t, gates, mod, grp, g_post)


def kernel(x_prompt, x_sample, state_hgrn, cache_k, cache_v, c, c_ctx, w_mod, b_mod, g_pre_mix, g_post_mix, g_pre_ffn, g_post_ffn, hgrn_w_in, hgrn_gamma, hgrn_g_norm, hgrn_w_out, attn_w_qkv, attn_q_gain, attn_k_gain, attn_w_o, moe_w_router, moe_b_router, moe_w_gate, moe_b_gate, moe_w_up, moe_b_up, moe_w_down, moe_b_down):
    bp, tp, d = x_prompt.shape
    bs, ts, _ = x_sample.shape
    depth = w_mod.shape[0]
    n_heads = d // HGRN_DK
    nk = N_KV_HEADS * HEAD_DIM
    assert tp % TM == 0 and ts % TM == 0 and ts % GRID_W == 0 and d % LANES == 0
    n_ctx = bp * tp
    ctx_tiles, lat_tiles = tp // TM, ts // TM
    geom = (bp, ctx_tiles, bs, lat_tiles)
    grp = _tile_groups(bp * ctx_tiles, bs, lat_tiles)

    x = jnp.concatenate([x_prompt.reshape(n_ctx, d), x_sample.reshape(bs * ts, d)], axis=0)

    rows = -(-(1 + bs) // 8) * 8
    cvec = jnp.zeros((rows, d), F32).at[0].set(c_ctx).at[1:1 + bs].set(c)
    mod_all = _mod_table(cvec, w_mod, b_mod).reshape(depth, rows, 6, d)

    lb_all = jnp.cumsum(jax.nn.softmax(hgrn_gamma.astype(F32), axis=0), axis=0)

    new_states, new_k, new_v = [], [], []
    for l in range(depth):
        mod = mod_all[l]
        if l % 2 == 0:
            a = l // 2
            s0 = jnp.concatenate(
                [jnp.zeros((bp, 2, n_heads, HGRN_DK, HGRN_DK), F32), state_hgrn[:, a].astype(F32)], axis=0)
            x, s_f, s_b = _hgrn_layer(x, mod, g_pre_mix[l], g_post_mix[l], hgrn_w_in[a], lb_all[l],
                                      hgrn_g_norm[a], hgrn_w_out[a], s0, geom)
            new_states.append(jnp.stack([s_f[:bp], s_b[:bp]], axis=1))
        else:
            m = l // 2
            w_qkv = attn_w_qkv[m].astype(BF16)
            q_c, k_c, v_c = _qkv(x, mod, grp, g_pre_mix[l], w_qkv, attn_q_gain[m], attn_k_gain[m],
                                 0, bp * ctx_tiles, 0, F32)
            q_l, k_l, v_l = _qkv(x, mod, grp, g_pre_mix[l], w_qkv, attn_q_gain[m], attn_k_gain[m],
                                 bp * ctx_tiles, bs * lat_tiles, ts, BF16)
            o_c = _attention(q_c, k_c, v_c, bp, tp)
            past = cache_k.shape[2]
            o_l = _attention(q_l, k_l, v_l, bs, ts,
                             cache=(cache_k[:, m].reshape(bs, past, nk), cache_v[:, m].reshape(bs, past, nk)))
            new_k.append(k_c.reshape(bp, tp, N_KV_HEADS, HEAD_DIM))
            new_v.append(v_c.reshape(bp, tp, N_KV_HEADS, HEAD_DIM))
            x = _attn_out(x, jnp.concatenate([o_c, o_l], axis=0), mod, grp,
                          attn_w_o[m].astype(BF16), g_post_mix[l])
        x = _moe_layer(x, mod, grp, l, g_pre_ffn[l], g_post_ffn[l], moe_w_router[l], moe_b_router[l],
                       moe_w_gate, moe_b_gate, moe_w_up, moe_b_up, moe_w_down, moe_b_down)

    y_prompt = x[:n_ctx].reshape(bp, tp, d)
    y_sample = x[n_ctx:].reshape(bs, ts, d)
    return (y_prompt, y_sample, jnp.stack(new_states, axis=1),
            jnp.stack(new_k, axis=1), jnp.stack(new_v, axis=1))
```

```python
import functools

import numpy as np
import jax
import jax.numpy as jnp
from jax import lax
from jax.experimental import pallas as pl
from jax.experimental.pallas import tpu as pltpu

F32 = jnp.float32
BF16 = jnp.bfloat16
I32 = jnp.int32

EPS = 1e-6
GRID_W = 64
ROPE_THETA = 10000.0
HGRN_DK = 128
HEAD_DIM = 128
N_KV_HEADS = 2
N_EXPERTS = 32
TOP_K = 4
SWIGLU_LIMIT = 7.0
SWIGLU_ALPHA = 1.702

LANES = 128
TM = 256
GLA_CHUNK = 64
MOE_BM = 256
ROW_TILE = 8
DMA_UNROLL = 8
MOD_COLS = 512
VMEM_LIMIT = 56 * 1024 * 1024
NEG_BIG = -1e30

NT_DIMS = (((1,), (1,)), ((), ()))
TN_DIMS = (((0,), (0,)), ((), ()))


def _params(*sem):
    return pltpu.CompilerParams(dimension_semantics=sem, vmem_limit_bytes=VMEM_LIMIT)


def _sigmoid(x):
    return 1.0 / (1.0 + jnp.exp(-x))


def _silu(x):
    return x * _sigmoid(x)


def _rms(x, gain):
    ms = jnp.mean(x * x, axis=-1, keepdims=True)
    return x * lax.rsqrt(ms + EPS) * gain


def _dot(a, b):
    return jnp.dot(a, b, preferred_element_type=F32)


def _mod_kernel(c_ref, w_ref, b_ref, o_ref):
    s = _silu(c_ref[...]).astype(BF16)
    o_ref[...] = _dot(s, w_ref[...].astype(BF16)) + b_ref[...]


def _mod_table(cvec, w_mod, b_mod):
    depth, d, n = w_mod.shape
    rows = cvec.shape[0]
    return pl.pallas_call(
        _mod_kernel,
        grid=(depth, n // MOD_COLS),
        in_specs=[
            pl.BlockSpec((rows, d), lambda l, j: (0, 0)),
            pl.BlockSpec((None, d, MOD_COLS), lambda l, j: (l, 0, j)),
            pl.BlockSpec((None, 1, MOD_COLS), lambda l, j: (l, 0, j)),
        ],
        out_specs=pl.BlockSpec((None, rows, MOD_COLS), lambda l, j: (l, 0, j)),
        out_shape=jax.ShapeDtypeStruct((depth, rows, n), F32),
        compiler_params=_params("arbitrary", "arbitrary"),
        name="mod_table",
    )(cvec, w_mod, b_mod.reshape(depth, 1, n))


def _gla_tile(q_s, k_s, lf_s, v_s, st_ref, o_s, direction):
    c = GLA_CHUNK
    d_model = q_s.shape[1]
    n_heads = d_model // HGRN_DK
    ri = lax.broadcasted_iota(I32, (c, c), 0)
    ci = lax.broadcasted_iota(I32, (c, c), 1)
    keep = (ci <= ri) if direction == 0 else (ci >= ri)
    ones_tri = jnp.where(keep, 1.0, 0.0).astype(BF16)
    chunks = range(TM // c)
    if direction == 1:
        chunks = reversed(chunks)
    for ch in chunks:
        rows = pl.ds(ch * c, c)
        lf = lf_s[rows, :]
        lf_hi = lf.astype(BF16)
        lf_lo = (lf - lf_hi.astype(F32)).astype(BF16)
        cum = _dot(ones_tri, lf_hi) + _dot(ones_tri, lf_lo)
        if direction == 0:
            btot, bmid = cum[c - 1:c], cum[c // 2 - 1:c // 2]
        else:
            btot, bmid = cum[0:1], cum[c // 2:c // 2 + 1]
        q_dec = q_s[rows, :] * jnp.exp(cum - bmid)
        k_inv = k_s[rows, :] * jnp.exp(bmid - cum)
        q_in = (q_dec * jnp.exp(bmid)).astype(BF16)
        k_end = (k_inv * jnp.exp(btot - bmid)).astype(BF16)
        q_dec = q_dec.astype(BF16)
        k_inv = k_inv.astype(BF16)
        e_tot = jnp.exp(btot)
        vc = v_s[rows, :].astype(BF16)
        for h in range(n_heads):
            sl = slice(h * HGRN_DK, (h + 1) * HGRN_DK)
            a = lax.dot_general(q_dec[:, sl], k_inv[:, sl], NT_DIMS, preferred_element_type=F32)
            a = jnp.where(keep, a, 0.0).astype(BF16)
            st = st_ref[h]
            o = _dot(a, vc[:, sl]) + lax.dot_general(
                q_in[:, sl], st.astype(BF16), NT_DIMS, preferred_element_type=F32)
            o_s[rows, sl] = o
            st_ref[h] = st * e_tot[:, sl] + lax.dot_general(
                vc[:, sl], k_end[:, sl], TN_DIMS, preferred_element_type=F32)


def _hgrn_gates(h, w_ref, lb, q_s, k_s, lf_s, v_s):
    d = h.shape[1]
    q_s[...] = _silu(_dot(h, w_ref[:, 0:d]))
    f = lb + (1.0 - lb) * _sigmoid(_dot(h, w_ref[:, d:2 * d]))
    lf_s[...] = jnp.log(f)
    k_s[...] = 1.0 - f
    v_s[...] = _dot(h, w_ref[:, 2 * d:3 * d])


def _load_state(meta, s0_ref, st_s):
    @pl.when(meta[2, pl.program_id(0)] == 1)
    def _():
        for h in range(st_s.shape[0]):
            st_s[h] = s0_ref[h].T


def _store_state(meta, sfin_ref, st_s):
    @pl.when(meta[3, pl.program_id(0)] == 1)
    def _():
        for h in range(st_s.shape[0]):
            sfin_ref[h] = st_s[h].T


def _hgrn_bwd_kernel(meta, x_ref, mod_ref, gpre_ref, w_ref, lb_ref, s0_ref,
                     ob_ref, sfin_ref, st_s, q_s, k_s, lf_s, v_s, o_s):
    _load_state(meta, s0_ref, st_s)
    h = (_rms(x_ref[...], gpre_ref[...]) * (1.0 + mod_ref[1:2]) + mod_ref[0:1]).astype(BF16)
    _hgrn_gates(h, w_ref, lb_ref[...], q_s, k_s, lf_s, v_s)
    _gla_tile(q_s, k_s, lf_s, v_s, st_s, o_s, 1)
    ob_ref[...] = o_s[...].astype(BF16)
    _store_state(meta, sfin_ref, st_s)


def _hgrn_fwd_kernel(meta, x_ref, mod_ref, gpre_ref, w_ref, lb_ref, s0_ref, ob_ref,
                     gnorm_ref, wout_ref, gpost_ref,
                     xo_ref, sfin_ref, st_s, q_s, k_s, lf_s, v_s, o_s):
    _load_state(meta, s0_ref, st_s)
    x = x_ref[...]
    d = x.shape[1]
    h = (_rms(x, gpre_ref[...]) * (1.0 + mod_ref[1:2]) + mod_ref[0:1]).astype(BF16)
    _hgrn_gates(h, w_ref, lb_ref[...], q_s, k_s, lf_s, v_s)
    _gla_tile(q_s, k_s, lf_s, v_s, st_s, o_s, 0)
    _store_state(meta, sfin_ref, st_s)
    gate = _silu(_dot(h, w_ref[:, 3 * d:4 * d]))
    gnorm = gnorm_ref[...]
    for hh in range(d // HGRN_DK):
        sl = slice(hh * HGRN_DK, (hh + 1) * HGRN_DK)
        o = o_s[:, sl] + ob_ref[:, sl].astype(F32)
        o_s[:, sl] = _rms(o, gnorm) * gate[:, sl]
    out = _dot(o_s[...].astype(BF16), wout_ref[...])
    xo_ref[...] = x + mod_ref[2:3] * _rms(out, gpost_ref[...])


def _seq_meta(n_ctx_seq, ctx_tiles, n_lat_seq, lat_tiles, direction):
    cols = []
    tile0 = 0
    for s in range(n_ctx_seq + n_lat_seq):
        n = ctx_tiles if s < n_ctx_seq else lat_tiles
        grp = 0 if s < n_ctx_seq else 1 + s - n_ctx_seq
        order = range(n) if direction == 0 else range(n - 1, -1, -1)
        for j, t in enumerate(order):
            cols.append((tile0 + t, s, int(j == 0), int(j == n - 1), grp))
        tile0 += n
    return jnp.asarray(np.array(cols, dtype=np.int32).T)


def _tile_groups(n_ctx_tiles, n_lat_seq, lat_tiles):
    g = [0] * n_ctx_tiles
    for b in range(n_lat_seq):
        g += [1 + b] * lat_tiles
    return jnp.asarray(np.array(g, dtype=np.int32))


def _hgrn_layer(x, mod, g_pre, g_post, w_in, lb, g_norm, w_out, s0, geom):
    t, d = x.shape
    n_heads = d // HGRN_DK
    n_seq = s0.shape[0]
    n_tiles = t // TM
    w_bwd = jnp.concatenate([w_in[:, 0:d], w_in[:, 2 * d:3 * d], w_in[:, 3 * d:4 * d]], axis=1).astype(BF16)
    w_fwd = jnp.concatenate([w_in[:, 0:d], w_in[:, d:2 * d], w_in[:, 3 * d:5 * d]], axis=1).astype(BF16)
    state_spec = pl.BlockSpec((None, n_heads, HGRN_DK, HGRN_DK), lambda i, m: (m[1, i], 0, 0, 0))
    tile_spec = pl.BlockSpec((TM, d), lambda i, m: (m[0, i], 0))
    mod_spec = pl.BlockSpec((None, 6, d), lambda i, m: (m[4, i], 0, 0))
    vec_spec = pl.BlockSpec((1, d), lambda i, m: (0, 0))
    scratch = [pltpu.VMEM((n_heads, HGRN_DK, HGRN_DK), F32)] + [pltpu.VMEM((TM, d), F32)] * 5

    meta_b = _seq_meta(*geom, 1)
    o_bwd, s_bwd = pl.pallas_call(
        _hgrn_bwd_kernel,
        grid_spec=pltpu.PrefetchScalarGridSpec(
            num_scalar_prefetch=1, grid=(n_tiles,),
            in_specs=[tile_spec, mod_spec, vec_spec,
                      pl.BlockSpec((d, 3 * d), lambda i, m: (0, 0)), vec_spec, state_spec],
            out_specs=[tile_spec, state_spec],
            scratch_shapes=scratch),
        out_shape=[jax.ShapeDtypeStruct((t, d), BF16),
                   jax.ShapeDtypeStruct((n_seq, n_heads, HGRN_DK, HGRN_DK), F32)],
        compiler_params=_params("arbitrary"),
        name="hgrn_bwd",
    )(meta_b, x, mod, g_pre.reshape(1, d), w_bwd, lb[1].reshape(1, d), s0[:, 1])

    meta_f = _seq_meta(*geom, 0)
    x_new, s_fwd = pl.pallas_call(
        _hgrn_fwd_kernel,
        grid_spec=pltpu.PrefetchScalarGridSpec(
            num_scalar_prefetch=1, grid=(n_tiles,),
            in_specs=[tile_spec, mod_spec, vec_spec,
                      pl.BlockSpec((d, 4 * d), lambda i, m: (0, 0)), vec_spec, state_spec,
                      tile_spec, pl.BlockSpec((1, HGRN_DK), lambda i, m: (0, 0)),
                      pl.BlockSpec((d, d), lambda i, m: (0, 0)), vec_spec],
            out_specs=[tile_spec, state_spec],
            scratch_shapes=scratch),
        out_shape=[jax.ShapeDtypeStruct((t, d), F32),
                   jax.ShapeDtypeStruct((n_seq, n_heads, HGRN_DK, HGRN_DK), F32)],
        input_output_aliases={1: 0},
        compiler_params=_params("arbitrary"),
        name="hgrn_fwd",
    )(meta_f, x, mod, g_pre.reshape(1, d), w_fwd, lb[0].reshape(1, d), s0[:, 0],
      o_bwd, g_norm.reshape(1, HGRN_DK), w_out.astype(BF16), g_post.reshape(1, d))
    return x_new, s_fwd, s_bwd


def _rope_tables(n_tok):
    axis_dim = HEAD_DIM // 2
    half = axis_dim // 2
    pos = np.arange(n_tok)
    row, col = pos // GRID_W, pos % GRID_W
    inv_freq = ROPE_THETA ** (-np.arange(0, axis_dim, 2, dtype=np.float64) / axis_dim)
    lane = np.arange(HEAD_DIM)
    p = np.where(lane[None, :] < axis_dim, row[:, None], col[:, None]).astype(np.float64)
    ang = p * inv_freq[(lane % axis_dim) % half][None, :]
    sign = np.where((lane % axis_dim) < half, -1.0, 1.0)[None, :]
    return jnp.asarray(np.cos(ang), F32), jnp.asarray(np.sin(ang) * sign, F32)


def _rope(xh, cos, sin_signed):
    axis_dim = HEAD_DIM // 2
    half = axis_dim // 2
    lane = lax.broadcasted_iota(I32, xh.shape, 1)
    upper = pltpu.roll(xh, HEAD_DIM - half, 1)
    lower = pltpu.roll(xh, half, 1)
    partner = jnp.where((lane % axis_dim) < half, upper, lower)
    return xh * cos + partner * sin_signed


def _qkv_kernel(grp, x_ref, mod_ref, gpre_ref, w_ref, qg_ref, kg_ref, *rest, rope):
    del grp
    if rope:
        cos_ref, sin_ref, q_out, k_out, v_out = rest
    else:
        q_out, k_out, v_out = rest
    d = x_ref.shape[1]
    nk = N_KV_HEADS * HEAD_DIM
    h = (_rms(x_ref[...], gpre_ref[...]) * (1.0 + mod_ref[1:2]) + mod_ref[0:1]).astype(BF16)
    q = _dot(h, w_ref[:, 0:d])
    k = _dot(h, w_ref[:, d:d + nk])
    v_out[...] = _dot(h, w_ref[:, d + nk:d + 2 * nk]).astype(v_out.dtype)
    scale = HEAD_DIM ** -0.5
    for hh in range(d // HEAD_DIM):
        sl = slice(hh * HEAD_DIM, (hh + 1) * HEAD_DIM)
        qh = _rms(q[:, sl], qg_ref[...])
        if rope:
            qh = _rope(qh, cos_ref[...], sin_ref[...])
        q_out[:, sl] = (qh * scale).astype(q_out.dtype)
    for hh in range(N_KV_HEADS):
        sl = slice(hh * HEAD_DIM, (hh + 1) * HEAD_DIM)
        kh = _rms(k[:, sl], kg_ref[...])
        if rope:
            kh = _rope(kh, cos_ref[...], sin_ref[...])
        k_out[:, sl] = kh.astype(k_out.dtype)


def _qkv(x, mod, grp, g_pre, w_qkv, q_gain, k_gain, tile0, n_tiles, rope_len, kv_dtype):
    t, d = x.shape
    nk = N_KV_HEADS * HEAD_DIM
    n_out = n_tiles * TM
    tile_in = pl.BlockSpec((TM, d), lambda i, g: (tile0 + i, 0))
    in_specs = [tile_in,
                pl.BlockSpec((None, 6, d), lambda i, g: (g[tile0 + i], 0, 0)),
                pl.BlockSpec((1, d), lambda i, g: (0, 0)),
                pl.BlockSpec((d, d + 2 * nk), lambda i, g: (0, 0)),
                pl.BlockSpec((1, HEAD_DIM), lambda i, g: (0, 0)),
                pl.BlockSpec((1, HEAD_DIM), lambda i, g: (0, 0))]
    args = [grp, x, mod, g_pre.reshape(1, d), w_qkv, q_gain.reshape(1, HEAD_DIM), k_gain.reshape(1, HEAD_DIM)]
    if rope_len:
        per_seq = rope_len // TM
        cos, sin = _rope_tables(rope_len)
        in_specs += [pl.BlockSpec((TM, HEAD_DIM), lambda i, g: (i % per_seq, 0))] * 2
        args += [cos, sin]
    return pl.pallas_call(
        functools.partial(_qkv_kernel, rope=bool(rope_len)),
        grid_spec=pltpu.PrefetchScalarGridSpec(
            num_scalar_prefetch=1, grid=(n_tiles,), in_specs=in_specs,
            out_specs=[pl.BlockSpec((TM, d), lambda i, g: (i, 0)),
                       pl.BlockSpec((TM, nk), lambda i, g: (i, 0)),
                       pl.BlockSpec((TM, nk), lambda i, g: (i, 0))]),
        out_shape=[jax.ShapeDtypeStruct((n_out, d), BF16),
                   jax.ShapeDtypeStruct((n_out, nk), kv_dtype),
                   jax.ShapeDtypeStruct((n_out, nk), kv_dtype)],
        compiler_params=_params("arbitrary"),
        name="qkv_rope" if rope_len else "qkv",
    )(*args)


def _attn_kernel(q_ref, *rest, n_pieces):
    kv, o_ref = rest[:-1], rest[-1]
    ks = [kv[2 * p][...].astype(BF16) for p in range(n_pieces)]
    vs = [kv[2 * p + 1][...].astype(BF16) for p in range(n_pieces)]
    group = q_ref.shape[1] // HEAD_DIM
    for g in range(group):
        sl = slice(g * HEAD_DIM, (g + 1) * HEAD_DIM)
        qh = q_ref[:, sl]
        ss = [lax.dot_general(qh, kp, NT_DIMS, preferred_element_type=F32) for kp in ks]
        m = ss[0].max(axis=-1, keepdims=True)
        for s in ss[1:]:
            m = jnp.maximum(m, s.max(axis=-1, keepdims=True))
        ps = [jnp.exp(s - m) for s in ss]
        denom = ps[0].sum(axis=-1, keepdims=True)
        for p in ps[1:]:
            denom = denom + p.sum(axis=-1, keepdims=True)
        o = _dot(ps[0].astype(BF16), vs[0])
        for p, vp in zip(ps[1:], vs[1:]):
            o = o + _dot(p.astype(BF16), vp)
        o_ref[:, sl] = (o / denom).astype(o_ref.dtype)


def _attention(q, k_new, v_new, n_seq, seq_len, cache=None):
    t, d = q.shape
    group_w = d // N_KV_HEADS
    nq = seq_len // TM
    q_spec = pl.BlockSpec((TM, group_w), lambda b, i, kv: (b * nq + i, kv))
    new_spec = pl.BlockSpec((seq_len, HEAD_DIM), lambda b, i, kv: (b, kv))
    in_specs, args = [q_spec], [q]
    if cache is not None:
        past = cache[0].shape[1]
        c_spec = pl.BlockSpec((None, past, HEAD_DIM), lambda b, i, kv: (b, 0, kv))
        in_specs += [c_spec, c_spec]
        args += list(cache)
    in_specs += [new_spec, new_spec]
    args += [k_new, v_new]
    return pl.pallas_call(
        functools.partial(_attn_kernel, n_pieces=len(args) // 2),
        grid=(n_seq, nq, N_KV_HEADS),
        in_specs=in_specs,
        out_specs=q_spec,
        out_shape=jax.ShapeDtypeStruct((t, d), BF16),
        compiler_params=_params("arbitrary", "arbitrary", "arbitrary"),
        name="attention",
    )(*args)


def _attn_out_kernel(grp, x_ref, o_ref, mod_ref, w_ref, gpost_ref, xo_ref):
    del grp
    out = _dot(o_ref[...], w_ref[...])
    xo_ref[...] = x_ref[...] + mod_ref[2:3] * _rms(out, gpost_ref[...])


def _attn_out(x, o, mod, grp, w_o, g_post):
    t, d = x.shape
    tile = pl.BlockSpec((TM, d), lambda i, g: (i, 0))
    return pl.pallas_call(
        _attn_out_kernel,
        grid_spec=pltpu.PrefetchScalarGridSpec(
            num_scalar_prefetch=1, grid=(t // TM,),
            in_specs=[tile, tile,
                      pl.BlockSpec((None, 6, d), lambda i, g: (g[i], 0, 0)),
                      pl.BlockSpec((d, d), lambda i, g: (0, 0)),
                      pl.BlockSpec((1, d), lambda i, g: (0, 0))],
            out_specs=tile),
        out_shape=jax.ShapeDtypeStruct((t, d), F32),
        input_output_aliases={1: 0},
        compiler_params=_params("arbitrary"),
        name="attn_out",
    )(grp, x, o, mod, w_o, g_post.reshape(1, d))


def _to_token_tiles(ref, x):
    rows = x.shape[0]
    for s in range(x.shape[1] // LANES):
        ref[pl.ds(s, rows, stride=ROW_TILE), :] = x[:, s * LANES:(s + 1) * LANES]


def _from_token_tiles(ref, rows):
    n = ROW_TILE
    return jnp.concatenate([ref[pl.ds(s, rows, stride=n), :] for s in range(n)], axis=1)


def _router_kernel(grp, x_ref, mod_ref, g_ref, wr_ref, br_ref,
                   h_ref, route_ref, gate_ref, cnt_ref, carry):
    del grp
    @pl.when(pl.program_id(0) == 0)
    def _():
        carry[...] = jnp.zeros_like(carry)
    h = _rms(x_ref[...], g_ref[...]) * (1.0 + mod_ref[4:5]) + mod_ref[3:4]
    _to_token_tiles(h_ref, h)
    logits = _dot(h.astype(BF16), wr_ref[...]) + br_ref[...]
    lane = lax.broadcasted_iota(I32, logits.shape, 1)
    lane_f = lane.astype(F32)
    vals, idxs = [], []
    work = logits
    for _ in range(TOP_K):
        m = work.max(axis=-1, keepdims=True)
        idx = jnp.where(work == m, lane_f, float(LANES)).min(axis=-1, keepdims=True).astype(I32)
        vals.append(m)
        idxs.append(idx)
        work = jnp.where(lane == idx, -jnp.inf, work)
    es = [jnp.exp(v - vals[0]) for v in vals]
    denom = es[0]
    for e in es[1:]:
        denom = denom + e
    onehot = jnp.zeros(logits.shape, F32)
    for idx in idxs:
        onehot = onehot + jnp.where(lane == idx, 1.0, 0.0)
    ri = lax.broadcasted_iota(I32, (TM, TM), 0)
    ci = lax.broadcasted_iota(I32, (TM, TM), 1)
    before = jnp.where(ci < ri, 1.0, 0.0).astype(BF16)
    rank_all = _dot(before, onehot.astype(BF16)) + carry[0:1]
    route = jnp.zeros(logits.shape, I32)
    gate = jnp.zeros(logits.shape, F32)
    for kk in range(TOP_K):
        rank = jnp.where(lane == idxs[kk], rank_all, 0.0).sum(axis=-1, keepdims=True)
        route = jnp.where(lane == kk, idxs[kk], route)
        route = jnp.where(lane == TOP_K + kk, rank.astype(I32), route)
        gate = jnp.where(lane == kk, es[kk] / denom, gate)
    route_ref[...] = route
    gate_ref[...] = gate
    carry[...] = carry[...] + onehot.sum(axis=0, keepdims=True)
    cnt_ref[...] = carry[...]


def _router(x, mod, grp, g_pre, w_router, b_router):
    t, d = x.shape
    n_exp = w_router.shape[1]
    wr = jnp.pad(w_router.astype(BF16), ((0, 0), (0, LANES - n_exp)))
    br = jnp.pad(b_router.reshape(1, n_exp), ((0, 0), (0, LANES - n_exp)), constant_values=NEG_BIG)
    tile = pl.BlockSpec((TM, d), lambda i, g: (i, 0))
    lane_tile = pl.BlockSpec((TM, LANES), lambda i, g: (i, 0))
    return pl.pallas_call(
        _router_kernel,
        grid_spec=pltpu.PrefetchScalarGridSpec(
            num_scalar_prefetch=1, grid=(t // TM,),
            in_specs=[tile,
                      pl.BlockSpec((None, 6, d), lambda i, g: (g[i], 0, 0)),
                      pl.BlockSpec((1, d), lambda i, g: (0, 0)),
                      pl.BlockSpec((d, LANES), lambda i, g: (0, 0)),
                      pl.BlockSpec((1, LANES), lambda i, g: (0, 0))],
            out_specs=[pl.BlockSpec((TM * ROW_TILE, LANES), lambda i, g: (i, 0)), lane_tile, lane_tile,
                       pl.BlockSpec((8, LANES), lambda i, g: (0, 0))],
            scratch_shapes=[pltpu.VMEM((8, LANES), F32)]),
        out_shape=[jax.ShapeDtypeStruct((t * ROW_TILE, LANES), F32),
                   jax.ShapeDtypeStruct((t, LANES), I32),
                   jax.ShapeDtypeStruct((t, LANES), F32),
                   jax.ShapeDtypeStruct((8, LANES), F32)],
        compiler_params=_params("arbitrary"),
        name="router",
    )(grp, x, mod, g_pre.reshape(1, d), wr, br)


def _tile_copy(src, src_row, dst, dst_row, sem):
    n = ROW_TILE
    return pltpu.make_async_copy(src.at[pl.ds(pl.multiple_of(src_row * n, n), n)],
                                 dst.at[pl.ds(pl.multiple_of(dst_row * n, n), n)], sem)


def _dispatch_kernel(pad_lo, pad_hi, dest_ref, h_ref, xs_hbm, zero_s, sem, zsem):
    def issue(g, c):
        for j in range(DMA_UNROLL):
            t = g * DMA_UNROLL + j
            for kk in range(TOP_K):
                _tile_copy(h_ref, t, xs_hbm, dest_ref[t * TOP_K + kk], sem).start(priority=kk % 2)
        return c
    lax.fori_loop(0, TM // DMA_UNROLL, issue, 0)

    @pl.when(pl.program_id(0) == 0)
    def _():
        zero_s[...] = jnp.zeros_like(zero_s)

        def per_expert(e, c):
            def zstart(s, c2):
                _tile_copy(zero_s, 0, xs_hbm, s, zsem).start()
                return c2
            lax.fori_loop(pad_lo[e], pad_hi[e], zstart, 0)

            def zwait(s, c2):
                _tile_copy(zero_s, 0, xs_hbm, s, zsem).wait()
                return c2
            lax.fori_loop(pad_lo[e], pad_hi[e], zwait, 0)
            return c
        lax.fori_loop(0, pad_lo.shape[0], per_expert, 0)

    for _ in range(TOP_K):
        pltpu.make_async_copy(h_ref, xs_hbm.at[pl.ds(0, TM * ROW_TILE)], sem).wait()


def _dispatch(h3, dest, pad_lo, pad_hi, n_slots):
    t = h3.shape[0] // ROW_TILE
    return pl.pallas_call(
        _dispatch_kernel,
        grid_spec=pltpu.PrefetchScalarGridSpec(
            num_scalar_prefetch=2, grid=(t // TM,),
            in_specs=[pl.BlockSpec((TM * TOP_K,), lambda i, lo, hi: (i,), memory_space=pltpu.SMEM),
                      pl.BlockSpec((TM * ROW_TILE, LANES), lambda i, lo, hi: (i, 0))],
            out_specs=pl.BlockSpec(memory_space=pl.ANY),
            scratch_shapes=[pltpu.VMEM((ROW_TILE, LANES), F32),
                            pltpu.SemaphoreType.DMA, pltpu.SemaphoreType.DMA]),
        out_shape=jax.ShapeDtypeStruct((n_slots * ROW_TILE, LANES), F32),
        compiler_params=_params("arbitrary"),
        name="moe_dispatch",
    )(pad_lo, pad_hi, dest, h3)


def _expert_kernel(blk_e, n_used, xs_ref, wg_ref, bg_ref, wu_ref, bu_ref, wd_ref, bd_ref,
                   y_ref, wg_s, wu_s, wd_s):
    i = pl.program_id(0)
    prev = blk_e[jnp.maximum(i - 1, 0)]

    @pl.when(jnp.logical_and(i < n_used[0], jnp.logical_or(i == 0, blk_e[i] != prev)))
    def _():
        wg_s[...] = wg_ref[...].astype(BF16)
        wu_s[...] = wu_ref[...].astype(BF16)
        wd_s[...] = wd_ref[...].astype(BF16)

    @pl.when(i < n_used[0])
    def _():
        x = _from_token_tiles(xs_ref, MOE_BM).astype(BF16)
        hg = jnp.minimum(_dot(x, wg_s[...]) + bg_ref[...], SWIGLU_LIMIT)
        hu = jnp.clip(_dot(x, wu_s[...]) + bu_ref[...], -SWIGLU_LIMIT, SWIGLU_LIMIT)
        hh = (hu + 1.0) * hg * _sigmoid(SWIGLU_ALPHA * hg)
        _to_token_tiles(y_ref, _dot(hh.astype(BF16), wd_s[...]) + bd_ref[...])


def _experts(xs, blk_e, n_used, layer, w_gate, b_gate, w_up, b_up, w_down, b_down):
    n_slots = xs.shape[0] // ROW_TILE
    depth, n_exp, d, f = w_gate.shape

    def blk(i, be, nu):
        return jnp.minimum(i, nu[0] - 1)

    row_spec = pl.BlockSpec((MOE_BM * ROW_TILE, LANES), lambda i, be, nu: (blk(i, be, nu), 0))

    def w_spec(k, n):
        return pl.BlockSpec((None, None, k, n), lambda i, be, nu: (layer, be[blk(i, be, nu)], 0, 0))

    return pl.pallas_call(
        _expert_kernel,
        grid_spec=pltpu.PrefetchScalarGridSpec(
            num_scalar_prefetch=2, grid=(n_slots // MOE_BM,),
            in_specs=[row_spec,
                      w_spec(d, f), w_spec(1, f), w_spec(d, f), w_spec(1, f), w_spec(f, d), w_spec(1, d)],
            out_specs=row_spec,
            scratch_shapes=[pltpu.VMEM((d, f), BF16), pltpu.VMEM((d, f), BF16), pltpu.VMEM((f, d), BF16)]),
        out_shape=jax.ShapeDtypeStruct(xs.shape, F32),
        compiler_params=_params("arbitrary"),
        name="moe_experts",
    )(blk_e, n_used, xs, w_gate, b_gate.reshape(depth, n_exp, 1, f), w_up, b_up.reshape(depth, n_exp, 1, f),
      w_down, b_down.reshape(depth, n_exp, 1, d))


def _combine_kernel(grp, dest_ref, dest_next_ref, x_ref, gate_ref, mod_ref, gpost_ref, y_hbm,
                    xo_ref, ybuf, sem):
    del grp
    i = pl.program_id(0)
    n = pl.num_programs(0)
    rows = TM * ROW_TILE

    def issue(dref, slot):
        def body(g, c):
            for j in range(DMA_UNROLL):
                t = g * DMA_UNROLL + j
                for kk in range(TOP_K):
                    _tile_copy(y_hbm, dref[t * TOP_K + kk], ybuf.at[slot, kk], t, sem.at[slot]).start(
                        priority=kk % 2)
            return c
        lax.fori_loop(0, TM // DMA_UNROLL, body, 0)

    @pl.when(i == 0)
    def _():
        issue(dest_ref, 0)

    @pl.when(i + 1 < n)
    def _():
        issue(dest_next_ref, (i + 1) % 2)

    slot = i % 2

    for kk in range(TOP_K):
        pltpu.make_async_copy(y_hbm.at[pl.ds(0, rows)], ybuf.at[slot, kk], sem.at[slot]).wait()

    gates = gate_ref[...]
    for s in range(ROW_TILE):
        sl = slice(s * LANES, (s + 1) * LANES)
        acc = gates[:, 0:1] * ybuf[slot, 0, pl.ds(s, TM, stride=ROW_TILE), :]
        for kk in range(1, TOP_K):
            acc = acc + gates[:, kk:kk + 1] * ybuf[slot, kk, pl.ds(s, TM, stride=ROW_TILE), :]
        xo_ref[:, sl] = acc
    y = xo_ref[...]
    xo_ref[...] = x_ref[...] + mod_ref[5:6] * _rms(y, gpost_ref[...])


def _combine(x, y, dest, gates, mod, grp, g_post):
    t, d = x.shape
    n_tiles = t // TM
    tile = pl.BlockSpec((TM, d), lambda i, g: (i, 0))
    return pl.pallas_call(
        _combine_kernel,
        grid_spec=pltpu.PrefetchScalarGridSpec(
            num_scalar_prefetch=1, grid=(n_tiles,),
            in_specs=[pl.BlockSpec((TM * TOP_K,), lambda i, g: (i,), memory_space=pltpu.SMEM),
                      pl.BlockSpec((TM * TOP_K,), lambda i, g: (jnp.minimum(i + 1, n_tiles - 1),),
                                   memory_space=pltpu.SMEM),
                      tile,
                      pl.BlockSpec((TM, LANES), lambda i, g: (i, 0)),
                      pl.BlockSpec((None, 6, d), lambda i, g: (g[i], 0, 0)),
                      pl.BlockSpec((1, d), lambda i, g: (0, 0)),
                      pl.BlockSpec(memory_space=pl.ANY)],
            out_specs=tile,
            scratch_shapes=[pltpu.VMEM((2, TOP_K, TM * ROW_TILE, LANES), F32),
                            pltpu.SemaphoreType.DMA((2,))]),
        out_shape=jax.ShapeDtypeStruct((t, d), F32),
        input_output_aliases={3: 0},
        compiler_params=_params("arbitrary"),
        name="moe_combine",
    )(grp, dest, dest, x, gates, mod, g_post.reshape(1, d), y)


def _moe_layer(x, mod, grp, layer, g_pre, g_post, w_router, b_router,
               w_gate, b_gate, w_up, b_up, w_down, b_down):
    t, d = x.shape
    n_exp = w_router.shape[1]
    assert d == ROW_TILE * LANES
    h3, route, gates, cnt = _router(x, mod, grp, g_pre, w_router, b_router)
    idx = route[:, 0:TOP_K]
    rank = route[:, TOP_K:2 * TOP_K]
    counts = cnt[0, :n_exp].astype(I32)
    padded = (counts + MOE_BM - 1) // MOE_BM * MOE_BM
    pad_end = jnp.cumsum(padded)
    pad_start = pad_end - padded
    experts = jnp.arange(n_exp, dtype=I32)
    dest = (rank + jnp.sum(jnp.where(idx[..., None] == experts, pad_start, 0), axis=-1)).reshape(-1)
    n_blocks = t * TOP_K // MOE_BM + n_exp
    n_used = (pad_end[-1:] // MOE_BM).astype(I32)
    blk_start = jnp.arange(n_blocks, dtype=I32) * MOE_BM
    blk_e = jnp.minimum(jnp.sum((blk_start[:, None] >= pad_end[None, :]).astype(I32), axis=-1), n_exp - 1)
    xs = _dispatch(h3, dest, pad_start + counts, pad_end, n_blocks * MOE_BM)
    y = _experts(xs, blk_e, n_used, layer, w_gate, b_gate, w_up, b_up, w_down, b_down)
    return _combine(x, y, dest, gates, mod, grp, g_post)


def kernel(x_prompt, x_sample, state_hgrn, cache_k, cache_v, c, c_ctx, w_mod, b_mod, g_pre_mix, g_post_mix, g_pre_ffn, g_post_ffn, hgrn_w_in, hgrn_gamma, hgrn_g_norm, hgrn_w_out, attn_w_qkv, attn_q_gain, attn_k_gain, attn_w_o, moe_w_router, moe_b_router, moe_w_gate, moe_b_gate, moe_w_up, moe_b_up, moe_w_down, moe_b_down):
    bp, tp, d = x_prompt.shape
    bs, ts, _ = x_sample.shape
    depth = w_mod.shape[0]
    n_heads = d // HGRN_DK
    nk = N_KV_HEADS * HEAD_DIM
    assert tp % TM == 0 and ts % TM == 0 and ts % GRID_W == 0 and d % LANES == 0
    n_ctx = bp * tp
    ctx_tiles, lat_tiles = tp // TM, ts // TM
    geom = (bp, ctx_tiles, bs, lat_tiles)
    grp = _tile_groups(bp * ctx_tiles, bs, lat_tiles)

    x = jnp.concatenate([x_prompt.reshape(n_ctx, d), x_sample.reshape(bs * ts, d)], axis=0)

    rows = -(-(1 + bs) // 8) * 8
    cvec = jnp.zeros((rows, d), F32).at[0].set(c_ctx).at[1:1 + bs].set(c)
    mod_all = _mod_table(cvec, w_mod, b_mod).reshape(depth, rows, 6, d)

    lb_all = jnp.cumsum(jax.nn.softmax(hgrn_gamma.astype(F32), axis=0), axis=0)

    new_states, new_k, new_v = [], [], []
    for l in range(depth):
        mod = mod_all[l]
        if l % 2 == 0:
            a = l // 2
            s0 = jnp.concatenate(
                [jnp.zeros((bp, 2, n_heads, HGRN_DK, HGRN_DK), F32), state_hgrn[:, a].astype(F32)], axis=0)
            x, s_f, s_b = _hgrn_layer(x, mod, g_pre_mix[l], g_post_mix[l], hgrn_w_in[a], lb_all[l],
                                      hgrn_g_norm[a], hgrn_w_out[a], s0, geom)
            new_states.append(jnp.stack([s_f[:bp], s_b[:bp]], axis=1))
        else:
            m = l // 2
            w_qkv = attn_w_qkv[m].astype(BF16)
            q_c, k_c, v_c = _qkv(x, mod, grp, g_pre_mix[l], w_qkv, attn_q_gain[m], attn_k_gain[m],
                                 0, bp * ctx_tiles, 0, F32)
            q_l, k_l, v_l = _qkv(x, mod, grp, g_pre_mix[l], w_qkv, attn_q_gain[m], attn_k_gain[m],
                                 bp * ctx_tiles, bs * lat_tiles, ts, BF16)
            o_c = _attention(q_c, k_c, v_c, bp, tp)
            past = cache_k.shape[2]
            o_l = _attention(q_l, k_l, v_l, bs, ts,
                             cache=(cache_k[:, m].reshape(bs, past, nk), cache_v[:, m].reshape(bs, past, nk)))
            new_k.append(k_c.reshape(bp, tp, N_KV_HEADS, HEAD_DIM))
            new_v.append(v_c.reshape(bp, tp, N_KV_HEADS, HEAD_DIM))
            x = _attn_out(x, jnp.concatenate([o_c, o_l], axis=0), mod, grp,
                          attn_w_o[m].astype(BF16), g_post_mix[l])
        x = _moe_layer(x, mod, grp, l, g_pre_ffn[l], g_post_ffn[l], moe_w_router[l], moe_b_router[l],
                       moe_w_gate, moe_b_gate, moe_w_up, moe_b_up, moe_w_down, moe_b_down)

    y_prompt = x[:n_ctx].reshape(bp, tp, d)
    y_sample = x[n_ctx:].reshape(bs, ts, d)
    return (y_prompt, y_sample, jnp.stack(new_states, axis=1),
            jnp.stack(new_k, axis=1), jnp.stack(new_v, axis=1))
```

```python
import functools

import numpy as np
import jax
import jax.numpy as jnp
from jax import lax
from jax.experimental import pallas as pl
from jax.experimental.pallas import tpu as pltpu

F32 = jnp.float32
BF16 = jnp.bfloat16
I32 = jnp.int32

EPS = 1e-6
GRID_W = 64
ROPE_THETA = 10000.0
HGRN_DK = 128
HEAD_DIM = 128
N_KV_HEADS = 2
N_EXPERTS = 32
TOP_K = 4
SWIGLU_LIMIT = 7.0
SWIGLU_ALPHA = 1.702

LANES = 128
TM = 256
GLA_CHUNK = 64
MOE_BM = 256
ROW_TILE = 8
DMA_UNROLL = 8
MOD_COLS = 512
VMEM_LIMIT = 56 * 1024 * 1024
NEG_BIG = -1e30

NT_DIMS = (((1,), (1,)), ((), ()))
TN_DIMS = (((0,), (0,)), ((), ()))


def _params(*sem):
    return pltpu.CompilerParams(dimension_semantics=sem, vmem_limit_bytes=VMEM_LIMIT)


def _sigmoid(x):
    return 1.0 / (1.0 + jnp.exp(-x))


def _silu(x):
    return x * _sigmoid(x)


def _rms(x, gain):
    ms = jnp.mean(x * x, axis=-1, keepdims=True)
    return x * lax.rsqrt(ms + EPS) * gain


def _dot(a, b):
    return jnp.dot(a, b, preferred_element_type=F32)


def _mod_kernel(c_ref, w_ref, b_ref, o_ref):
    s = _silu(c_ref[...]).astype(BF16)
    o_ref[...] = _dot(s, w_ref[...].astype(BF16)) + b_ref[...]


def _mod_table(cvec, w_mod, b_mod):
    depth, d, n = w_mod.shape
    rows = cvec.shape[0]
    return pl.pallas_call(
        _mod_kernel,
        grid=(depth, n // MOD_COLS),
        in_specs=[
            pl.BlockSpec((rows, d), lambda l, j: (0, 0)),
            pl.BlockSpec((None, d, MOD_COLS), lambda l, j: (l, 0, j)),
            pl.BlockSpec((None, 1, MOD_COLS), lambda l, j: (l, 0, j)),
        ],
        out_specs=pl.BlockSpec((None, rows, MOD_COLS), lambda l, j: (l, 0, j)),
        out_shape=jax.ShapeDtypeStruct((depth, rows, n), F32),
        compiler_params=_params("arbitrary", "arbitrary"),
        name="mod_table",
    )(cvec, w_mod, b_mod.reshape(depth, 1, n))


def _gla_tile(q_s, k_s, lf_s, v_s, st_ref, o_s, direction):
    c = GLA_CHUNK
    d_model = q_s.shape[1]
    n_heads = d_model // HGRN_DK
    ri = lax.broadcasted_iota(I32, (c, c), 0)
    ci = lax.broadcasted_iota(I32, (c, c), 1)
    keep = (ci <= ri) if direction == 0 else (ci >= ri)
    ones_tri = jnp.where(keep, 1.0, 0.0).astype(BF16)
    chunks = range(TM // c)
    if direction == 1:
        chunks = reversed(chunks)
    for ch in chunks:
        rows = pl.ds(ch * c, c)
        lf = lf_s[rows, :]
        lf_hi = lf.astype(BF16)
        lf_lo = (lf - lf_hi.astype(F32)).astype(BF16)
        cum = _dot(ones_tri, lf_hi) + _dot(ones_tri, lf_lo)
        if direction == 0:
            btot, bmid = cum[c - 1:c], cum[c // 2 - 1:c // 2]
        else:
            btot, bmid = cum[0:1], cum[c // 2:c // 2 + 1]
        q_dec = q_s[rows, :] * jnp.exp(cum - bmid)
        k_inv = k_s[rows, :] * jnp.exp(bmid - cum)
        q_in = (q_dec * jnp.exp(bmid)).astype(BF16)
        k_end = (k_inv * jnp.exp(btot - bmid)).astype(BF16)
        q_dec = q_dec.astype(BF16)
        k_inv = k_inv.astype(BF16)
        e_tot = jnp.exp(btot)
        vc = v_s[rows, :].astype(BF16)
        for h in range(n_heads):
            sl = slice(h * HGRN_DK, (h + 1) * HGRN_DK)
            a = lax.dot_general(q_dec[:, sl], k_inv[:, sl], NT_DIMS, preferred_element_type=F32)
            a = jnp.where(keep, a, 0.0).astype(BF16)
            st = st_ref[h]
            o = _dot(a, vc[:, sl]) + lax.dot_general(
                q_in[:, sl], st.astype(BF16), NT_DIMS, preferred_element_type=F32)
            o_s[rows, sl] = o
            st_ref[h] = st * e_tot[:, sl] + lax.dot_general(
                vc[:, sl], k_end[:, sl], TN_DIMS, preferred_element_type=F32)


def _hgrn_gates(h, w_ref, lb, q_s, k_s, lf_s, v_s):
    d = h.shape[1]
    q_s[...] = _silu(_dot(h, w_ref[:, 0:d]))
    f = lb + (1.0 - lb) * _sigmoid(_dot(h, w_ref[:, d:2 * d]))
    lf_s[...] = jnp.log(f)
    k_s[...] = 1.0 - f
    v_s[...] = _dot(h, w_ref[:, 2 * d:3 * d])


def _load_state(meta, s0_ref, st_s):
    @pl.when(meta[2, pl.program_id(0)] == 1)
    def _():
        for h in range(st_s.shape[0]):
            st_s[h] = s0_ref[h].T


def _store_state(meta, sfin_ref, st_s):
    @pl.when(meta[3, pl.program_id(0)] == 1)
    def _():
        for h in range(st_s.shape[0]):
            sfin_ref[h] = st_s[h].T


def _hgrn_project(x_ref, mod_ref, gpre_ref, w_ref, lb_ref, bufs):
    h = (_rms(x_ref[...], gpre_ref[...]) * (1.0 + mod_ref[1:2]) + mod_ref[0:1]).astype(BF16)
    _hgrn_gates(h, w_ref, lb_ref[...], *bufs[:4])
    if len(bufs) > 4:
        d = h.shape[1]
        bufs[4][...] = _silu(_dot(h, w_ref[:, 3 * d:4 * d]))


def _two_phase(first_project, step, set_a, set_b):
    s = pl.program_id(0)

    @pl.when(s == 0)
    def _():
        first_project(set_a)

    @pl.when(s % 2 == 0)
    def _():
        step(set_a, set_b)

    @pl.when(s % 2 == 1)
    def _():
        step(set_b, set_a)


def _hgrn_bwd_kernel(meta, xn_ref, modn_ref, x0_ref, mod0_ref, gpre_ref, w_ref, lb_ref, s0_ref,
                     ob_ref, sfin_ref, st_s, o_s, *bufs):
    _load_state(meta, s0_ref, st_s)

    def step(cur, nxt):
        _hgrn_project(xn_ref, modn_ref, gpre_ref, w_ref, lb_ref, nxt)
        _gla_tile(*cur, st_s, o_s, 1)
        ob_ref[...] = o_s[...].astype(BF16)

    _two_phase(lambda b: _hgrn_project(x0_ref, mod0_ref, gpre_ref, w_ref, lb_ref, b),
               step, bufs[:4], bufs[4:])
    _store_state(meta, sfin_ref, st_s)


def _hgrn_fwd_kernel(meta, xn_ref, modn_ref, x0_ref, mod0_ref, gpre_ref, w_ref, lb_ref, s0_ref,
                     x_ref, mod_ref, ob_ref, gnorm_ref, wout_ref, gpost_ref,
                     xo_ref, sfin_ref, st_s, o_s, *bufs):
    _load_state(meta, s0_ref, st_s)
    d = x_ref.shape[1]

    def step(cur, nxt):
        _hgrn_project(xn_ref, modn_ref, gpre_ref, w_ref, lb_ref, nxt)
        _gla_tile(*cur[:4], st_s, o_s, 0)
        gnorm = gnorm_ref[...]
        for hh in range(d // HGRN_DK):
            sl = slice(hh * HGRN_DK, (hh + 1) * HGRN_DK)
            o = o_s[:, sl] + ob_ref[:, sl].astype(F32)
            o_s[:, sl] = _rms(o, gnorm) * cur[4][:, sl]
        out = _dot(o_s[...].astype(BF16), wout_ref[...])
        xo_ref[...] = x_ref[...] + mod_ref[2:3] * _rms(out, gpost_ref[...])

    _two_phase(lambda b: _hgrn_project(x0_ref, mod0_ref, gpre_ref, w_ref, lb_ref, b),
               step, bufs[:5], bufs[5:])
    _store_state(meta, sfin_ref, st_s)


def _seq_meta(n_ctx_seq, ctx_tiles, n_lat_seq, lat_tiles, direction):
    cols = []
    tile0 = 0
    for s in range(n_ctx_seq + n_lat_seq):
        n = ctx_tiles if s < n_ctx_seq else lat_tiles
        grp = 0 if s < n_ctx_seq else 1 + s - n_ctx_seq
        order = range(n) if direction == 0 else range(n - 1, -1, -1)
        for j, t in enumerate(order):
            cols.append([tile0 + t, s, int(j == 0), int(j == n - 1), grp])
        tile0 += n
    for i, col in enumerate(cols):
        nxt = cols[min(i + 1, len(cols) - 1)]
        col += [nxt[0], nxt[4]]
    return jnp.asarray(np.array(cols, dtype=np.int32).T)


def _tile_groups(n_ctx_tiles, n_lat_seq, lat_tiles):
    g = [0] * n_ctx_tiles
    for b in range(n_lat_seq):
        g += [1 + b] * lat_tiles
    return jnp.asarray(np.array(g, dtype=np.int32))


def _hgrn_layer(x, mod, g_pre, g_post, w_in, lb, g_norm, w_out, s0, geom):
    t, d = x.shape
    n_heads = d // HGRN_DK
    n_seq = s0.shape[0]
    n_tiles = t // TM
    w_bwd = jnp.concatenate([w_in[:, 0:d], w_in[:, 2 * d:3 * d], w_in[:, 3 * d:4 * d]], axis=1).astype(BF16)
    w_fwd = jnp.concatenate([w_in[:, 0:d], w_in[:, d:2 * d], w_in[:, 3 * d:5 * d]], axis=1).astype(BF16)
    state_spec = pl.BlockSpec((None, n_heads, HGRN_DK, HGRN_DK), lambda i, m: (m[1, i], 0, 0, 0))
    tile_spec = pl.BlockSpec((TM, d), lambda i, m: (m[0, i], 0))
    mod_spec = pl.BlockSpec((None, 6, d), lambda i, m: (m[4, i], 0, 0))
    next_tile = pl.BlockSpec((TM, d), lambda i, m: (m[5, i], 0))
    next_mod = pl.BlockSpec((None, 6, d), lambda i, m: (m[6, i], 0, 0))
    first_tile = pl.BlockSpec((TM, d), lambda i, m: (m[0, 0], 0))
    first_mod = pl.BlockSpec((None, 6, d), lambda i, m: (m[4, 0], 0, 0))
    vec_spec = pl.BlockSpec((1, d), lambda i, m: (0, 0))
    project_specs = [next_tile, next_mod, first_tile, first_mod, vec_spec]

    def scratch(n_bufs):
        return ([pltpu.VMEM((n_heads, HGRN_DK, HGRN_DK), F32)]
                + [pltpu.VMEM((TM, d), F32)] * (1 + 2 * n_bufs))

    meta_b = _seq_meta(*geom, 1)
    o_bwd, s_bwd = pl.pallas_call(
        _hgrn_bwd_kernel,
        grid_spec=pltpu.PrefetchScalarGridSpec(
            num_scalar_prefetch=1, grid=(n_tiles,),
            in_specs=project_specs + [pl.BlockSpec((d, 3 * d), lambda i, m: (0, 0)), vec_spec, state_spec],
            out_specs=[tile_spec, state_spec],
            scratch_shapes=scratch(4)),
        out_shape=[jax.ShapeDtypeStruct((t, d), BF16),
                   jax.ShapeDtypeStruct((n_seq, n_heads, HGRN_DK, HGRN_DK), F32)],
        compiler_params=_params("arbitrary"),
        name="hgrn_bwd",
    )(meta_b, x, mod, x, mod, g_pre.reshape(1, d), w_bwd, lb[1].reshape(1, d), s0[:, 1])

    meta_f = _seq_meta(*geom, 0)
    x_new, s_fwd = pl.pallas_call(
        _hgrn_fwd_kernel,
        grid_spec=pltpu.PrefetchScalarGridSpec(
            num_scalar_prefetch=1, grid=(n_tiles,),
            in_specs=project_specs + [pl.BlockSpec((d, 4 * d), lambda i, m: (0, 0)), vec_spec, state_spec,
                                      tile_spec, mod_spec, tile_spec,
                                      pl.BlockSpec((1, HGRN_DK), lambda i, m: (0, 0)),
                                      pl.BlockSpec((d, d), lambda i, m: (0, 0)), vec_spec],
            out_specs=[tile_spec, state_spec],
            scratch_shapes=scratch(5)),
        out_shape=[jax.ShapeDtypeStruct((t, d), F32),
                   jax.ShapeDtypeStruct((n_seq, n_heads, HGRN_DK, HGRN_DK), F32)],
        compiler_params=_params("arbitrary"),
        name="hgrn_fwd",
    )(meta_f, x, mod, x, mod, g_pre.reshape(1, d), w_fwd, lb[0].reshape(1, d), s0[:, 0],
      x, mod, o_bwd, g_norm.reshape(1, HGRN_DK), w_out.astype(BF16), g_post.reshape(1, d))
    return x_new, s_fwd, s_bwd


def _rope_tables(n_tok):
    axis_dim = HEAD_DIM // 2
    half = axis_dim // 2
    pos = np.arange(n_tok)
    row, col = pos // GRID_W, pos % GRID_W
    inv_freq = ROPE_THETA ** (-np.arange(0, axis_dim, 2, dtype=np.float64) / axis_dim)
    lane = np.arange(HEAD_DIM)
    p = np.where(lane[None, :] < axis_dim, row[:, None], col[:, None]).astype(np.float64)
    ang = p * inv_freq[(lane % axis_dim) % half][None, :]
    sign = np.where((lane % axis_dim) < half, -1.0, 1.0)[None, :]
    return jnp.asarray(np.cos(ang), F32), jnp.asarray(np.sin(ang) * sign, F32)


def _rope(xh, cos, sin_signed):
    axis_dim = HEAD_DIM // 2
    half = axis_dim // 2
    lane = lax.broadcasted_iota(I32, xh.shape, 1)
    upper = pltpu.roll(xh, HEAD_DIM - half, 1)
    lower = pltpu.roll(xh, half, 1)
    partner = jnp.where((lane % axis_dim) < half, upper, lower)
    return xh * cos + partner * sin_signed


def _qkv_kernel(grp, x_ref, mod_ref, gpre_ref, w_ref, qg_ref, kg_ref, *rest, rope):
    del grp
    if rope:
        cos_ref, sin_ref, q_out, k_out, v_out = rest
    else:
        q_out, k_out, v_out = rest
    d = x_ref.shape[1]
    nk = N_KV_HEADS * HEAD_DIM
    h = (_rms(x_ref[...], gpre_ref[...]) * (1.0 + mod_ref[1:2]) + mod_ref[0:1]).astype(BF16)
    q = _dot(h, w_ref[:, 0:d])
    k = _dot(h, w_ref[:, d:d + nk])
    v_out[...] = _dot(h, w_ref[:, d + nk:d + 2 * nk]).astype(v_out.dtype)
    scale = HEAD_DIM ** -0.5
    for hh in range(d // HEAD_DIM):
        sl = slice(hh * HEAD_DIM, (hh + 1) * HEAD_DIM)
        qh = _rms(q[:, sl], qg_ref[...])
        if rope:
            qh = _rope(qh, cos_ref[...], sin_ref[...])
        q_out[:, sl] = (qh * scale).astype(q_out.dtype)
    for hh in range(N_KV_HEADS):
        sl = slice(hh * HEAD_DIM, (hh + 1) * HEAD_DIM)
        kh = _rms(k[:, sl], kg_ref[...])
        if rope:
            kh = _rope(kh, cos_ref[...], sin_ref[...])
        k_out[:, sl] = kh.astype(k_out.dtype)


def _qkv(x, mod, grp, g_pre, w_qkv, q_gain, k_gain, tile0, n_tiles, rope_len, kv_dtype):
    t, d = x.shape
    nk = N_KV_HEADS * HEAD_DIM
    n_out = n_tiles * TM
    tile_in = pl.BlockSpec((TM, d), lambda i, g: (tile0 + i, 0))
    in_specs = [tile_in,
                pl.BlockSpec((None, 6, d), lambda i, g: (g[tile0 + i], 0, 0)),
                pl.BlockSpec((1, d), lambda i, g: (0, 0)),
                pl.BlockSpec((d, d + 2 * nk), lambda i, g: (0, 0)),
                pl.BlockSpec((1, HEAD_DIM), lambda i, g: (0, 0)),
                pl.BlockSpec((1, HEAD_DIM), lambda i, g: (0, 0))]
    args = [grp, x, mod, g_pre.reshape(1, d), w_qkv, q_gain.reshape(1, HEAD_DIM), k_gain.reshape(1, HEAD_DIM)]
    if rope_len:
        per_seq = rope_len // TM
        cos, sin = _rope_tables(rope_len)
        in_specs += [pl.BlockSpec((TM, HEAD_DIM), lambda i, g: (i % per_seq, 0))] * 2
        args += [cos, sin]
    return pl.pallas_call(
        functools.partial(_qkv_kernel, rope=bool(rope_len)),
        grid_spec=pltpu.PrefetchScalarGridSpec(
            num_scalar_prefetch=1, grid=(n_tiles,), in_specs=in_specs,
            out_specs=[pl.BlockSpec((TM, d), lambda i, g: (i, 0)),
                       pl.BlockSpec((TM, nk), lambda i, g: (i, 0)),
                       pl.BlockSpec((TM, nk), lambda i, g: (i, 0))]),
        out_shape=[jax.ShapeDtypeStruct((n_out, d), BF16),
                   jax.ShapeDtypeStruct((n_out, nk), kv_dtype),
                   jax.ShapeDtypeStruct((n_out, nk), kv_dtype)],
        compiler_params=_params("arbitrary"),
        name="qkv_rope" if rope_len else "qkv",
    )(*args)


def _attn_kernel(q_ref, *rest, n_pieces):
    kv, o_ref = rest[:-1], rest[-1]
    ks = [kv[2 * p][...].astype(BF16) for p in range(n_pieces)]
    vs = [kv[2 * p + 1][...].astype(BF16) for p in range(n_pieces)]
    group = q_ref.shape[1] // HEAD_DIM
    for g in range(group):
        sl = slice(g * HEAD_DIM, (g + 1) * HEAD_DIM)
        qh = q_ref[:, sl]
        ss = [lax.dot_general(qh, kp, NT_DIMS, preferred_element_type=F32) for kp in ks]
        m = ss[0].max(axis=-1, keepdims=True)
        for s in ss[1:]:
            m = jnp.maximum(m, s.max(axis=-1, keepdims=True))
        ps = [jnp.exp(s - m) for s in ss]
        denom = ps[0].sum(axis=-1, keepdims=True)
        for p in ps[1:]:
            denom = denom + p.sum(axis=-1, keepdims=True)
        o = _dot(ps[0].astype(BF16), vs[0])
        for p, vp in zip(ps[1:], vs[1:]):
            o = o + _dot(p.astype(BF16), vp)
        o_ref[:, sl] = (o / denom).astype(o_ref.dtype)


def _attention(q, k_new, v_new, n_seq, seq_len, cache=None):
    t, d = q.shape
    group_w = d // N_KV_HEADS
    nq = seq_len // TM
    q_spec = pl.BlockSpec((TM, group_w), lambda b, i, kv: (b * nq + i, kv))
    new_spec = pl.BlockSpec((seq_len, HEAD_DIM), lambda b, i, kv: (b, kv))
    in_specs, args = [q_spec], [q]
    if cache is not None:
        past = cache[0].shape[1]
        c_spec = pl.BlockSpec((None, past, HEAD_DIM), lambda b, i, kv: (b, 0, kv))
        in_specs += [c_spec, c_spec]
        args += list(cache)
    in_specs += [new_spec, new_spec]
    args += [k_new, v_new]
    return pl.pallas_call(
        functools.partial(_attn_kernel, n_pieces=len(args) // 2),
        grid=(n_seq, nq, N_KV_HEADS),
        in_specs=in_specs,
        out_specs=q_spec,
        out_shape=jax.ShapeDtypeStruct((t, d), BF16),
        compiler_params=_params("arbitrary", "arbitrary", "arbitrary"),
        name="attention",
    )(*args)


def _attn_out_kernel(grp, x_ref, oc_ref, ol_ref, mod_ref, w_ref, gpost_ref, xo_ref, *, n_ctx_tiles):
    del grp
    o = jnp.where(pl.program_id(0) < n_ctx_tiles, oc_ref[...], ol_ref[...])
    out = _dot(o, w_ref[...])
    xo_ref[...] = x_ref[...] + mod_ref[2:3] * _rms(out, gpost_ref[...])


def _attn_out(x, o_ctx, o_lat, mod, grp, w_o, g_post):
    t, d = x.shape
    nct = o_ctx.shape[0] // TM
    tile = pl.BlockSpec((TM, d), lambda i, g: (i, 0))
    return pl.pallas_call(
        functools.partial(_attn_out_kernel, n_ctx_tiles=nct),
        grid_spec=pltpu.PrefetchScalarGridSpec(
            num_scalar_prefetch=1, grid=(t // TM,),
            in_specs=[tile,
                      pl.BlockSpec((TM, d), lambda i, g: (jnp.minimum(i, nct - 1), 0)),
                      pl.BlockSpec((TM, d), lambda i, g: (jnp.maximum(i - nct, 0), 0)),
                      pl.BlockSpec((None, 6, d), lambda i, g: (g[i], 0, 0)),
                      pl.BlockSpec((d, d), lambda i, g: (0, 0)),
                      pl.BlockSpec((1, d), lambda i, g: (0, 0))],
            out_specs=tile),
        out_shape=jax.ShapeDtypeStruct((t, d), F32),
        input_output_aliases={1: 0},
        compiler_params=_params("arbitrary"),
        name="attn_out",
    )(grp, x, o_ctx, o_lat, mod, w_o, g_post.reshape(1, d))


def _to_token_tiles(ref, x):
    rows = x.shape[0]
    for s in range(x.shape[1] // LANES):
        ref[pl.ds(s, rows, stride=ROW_TILE), :] = x[:, s * LANES:(s + 1) * LANES]


def _from_token_tiles(ref, rows):
    n = ROW_TILE
    return jnp.concatenate([ref[pl.ds(s, rows, stride=n), :] for s in range(n)], axis=1)


def _router_kernel(grp, x_ref, mod_ref, g_ref, wr_ref, br_ref,
                   h_ref, route_ref, gate_ref, cnt_ref, carry):
    del grp
    @pl.when(pl.program_id(0) == 0)
    def _():
        carry[...] = jnp.zeros_like(carry)
    h = _rms(x_ref[...], g_ref[...]) * (1.0 + mod_ref[4:5]) + mod_ref[3:4]
    _to_token_tiles(h_ref, h)
    logits = _dot(h.astype(BF16), wr_ref[...]) + br_ref[...]
    lane = lax.broadcasted_iota(I32, logits.shape, 1)
    lane_f = lane.astype(F32)
    vals, idxs = [], []
    work = logits
    for _ in range(TOP_K):
        m = work.max(axis=-1, keepdims=True)
        idx = jnp.where(work == m, lane_f, float(LANES)).min(axis=-1, keepdims=True).astype(I32)
        vals.append(m)
        idxs.append(idx)
        work = jnp.where(lane == idx, -jnp.inf, work)
    es = [jnp.exp(v - vals[0]) for v in vals]
    denom = es[0]
    for e in es[1:]:
        denom = denom + e
    onehot = jnp.zeros(logits.shape, F32)
    for idx in idxs:
        onehot = onehot + jnp.where(lane == idx, 1.0, 0.0)
    ri = lax.broadcasted_iota(I32, (TM, TM), 0)
    ci = lax.broadcasted_iota(I32, (TM, TM), 1)
    before = jnp.where(ci < ri, 1.0, 0.0).astype(BF16)
    rank_all = _dot(before, onehot.astype(BF16)) + carry[0:1]
    route = jnp.zeros(logits.shape, I32)
    gate = jnp.zeros(logits.shape, F32)
    for kk in range(TOP_K):
        rank = jnp.where(lane == idxs[kk], rank_all, 0.0).sum(axis=-1, keepdims=True)
        route = jnp.where(lane == kk, idxs[kk], route)
        route = jnp.where(lane == TOP_K + kk, rank.astype(I32), route)
        gate = jnp.where(lane == kk, es[kk] / denom, gate)
    route_ref[...] = route
    gate_ref[...] = gate
    carry[...] = carry[...] + onehot.sum(axis=0, keepdims=True)
    cnt_ref[...] = carry[...]


def _router(x, mod, grp, g_pre, w_router, b_router):
    t, d = x.shape
    n_exp = w_router.shape[1]
    wr = jnp.pad(w_router.astype(BF16), ((0, 0), (0, LANES - n_exp)))
    br = jnp.pad(b_router.reshape(1, n_exp), ((0, 0), (0, LANES - n_exp)), constant_values=NEG_BIG)
    tile = pl.BlockSpec((TM, d), lambda i, g: (i, 0))
    lane_tile = pl.BlockSpec((TM, LANES), lambda i, g: (i, 0))
    return pl.pallas_call(
        _router_kernel,
        grid_spec=pltpu.PrefetchScalarGridSpec(
            num_scalar_prefetch=1, grid=(t // TM,),
            in_specs=[tile,
                      pl.BlockSpec((None, 6, d), lambda i, g: (g[i], 0, 0)),
                      pl.BlockSpec((1, d), lambda i, g: (0, 0)),
                      pl.BlockSpec((d, LANES), lambda i, g: (0, 0)),
                      pl.BlockSpec((1, LANES), lambda i, g: (0, 0))],
            out_specs=[pl.BlockSpec((TM * ROW_TILE, LANES), lambda i, g: (i, 0)), lane_tile, lane_tile,
                       pl.BlockSpec((8, LANES), lambda i, g: (0, 0))],
            scratch_shapes=[pltpu.VMEM((8, LANES), F32)]),
        out_shape=[jax.ShapeDtypeStruct((t * ROW_TILE, LANES), F32),
                   jax.ShapeDtypeStruct((t, LANES), I32),
                   jax.ShapeDtypeStruct((t, LANES), F32),
                   jax.ShapeDtypeStruct((8, LANES), F32)],
        compiler_params=_params("arbitrary"),
        name="router",
    )(grp, x, mod, g_pre.reshape(1, d), wr, br)


def _tile_copy(src, src_row, dst, dst_row, sem):
    n = ROW_TILE
    return pltpu.make_async_copy(src.at[pl.ds(pl.multiple_of(src_row * n, n), n)],
                                 dst.at[pl.ds(pl.multiple_of(dst_row * n, n), n)], sem)


def _dispatch_kernel(pad_lo, pad_hi, dest_ref, h_ref, xs_hbm, zero_s, sem, zsem):
    def issue(g, c):
        for j in range(DMA_UNROLL):
            t = g * DMA_UNROLL + j
            for kk in range(TOP_K):
                _tile_copy(h_ref, t, xs_hbm, dest_ref[t * TOP_K + kk], sem).start(priority=kk % 2)
        return c
    lax.fori_loop(0, TM // DMA_UNROLL, issue, 0)

    @pl.when(pl.program_id(0) == 0)
    def _():
        zero_s[...] = jnp.zeros_like(zero_s)

        def per_expert(e, c):
            def zstart(s, c2):
                _tile_copy(zero_s, 0, xs_hbm, s, zsem).start()
                return c2
            lax.fori_loop(pad_lo[e], pad_hi[e], zstart, 0)

            def zwait(s, c2):
                _tile_copy(zero_s, 0, xs_hbm, s, zsem).wait()
                return c2
            lax.fori_loop(pad_lo[e], pad_hi[e], zwait, 0)
            return c
        lax.fori_loop(0, pad_lo.shape[0], per_expert, 0)

    for _ in range(TOP_K):
        pltpu.make_async_copy(h_ref, xs_hbm.at[pl.ds(0, TM * ROW_TILE)], sem).wait()


def _dispatch(h3, dest, pad_lo, pad_hi, n_slots):
    t = h3.shape[0] // ROW_TILE
    return pl.pallas_call(
        _dispatch_kernel,
        grid_spec=pltpu.PrefetchScalarGridSpec(
            num_scalar_prefetch=2, grid=(t // TM,),
            in_specs=[pl.BlockSpec((TM * TOP_K,), lambda i, lo, hi: (i,), memory_space=pltpu.SMEM),
                      pl.BlockSpec((TM * ROW_TILE, LANES), lambda i, lo, hi: (i, 0))],
            out_specs=pl.BlockSpec(memory_space=pl.ANY),
            scratch_shapes=[pltpu.VMEM((ROW_TILE, LANES), F32),
                            pltpu.SemaphoreType.DMA, pltpu.SemaphoreType.DMA]),
        out_shape=jax.ShapeDtypeStruct((n_slots * ROW_TILE, LANES), F32),
        compiler_params=_params("arbitrary"),
        name="moe_dispatch",
    )(pad_lo, pad_hi, dest, h3)


def _expert_kernel(blk0, nblk, xs_hbm, wg_ref, bg_ref, wu_ref, bu_ref, wd_ref, bd_ref,
                   y_hbm, wg_s, wu_s, wd_s, xbuf, ybuf, in_sem, out_sem):
    e = pl.program_id(0)
    first = blk0[e]
    n = nblk[e]
    rows = MOE_BM * ROW_TILE

    def in_copy(b, slot):
        return pltpu.make_async_copy(
            xs_hbm.at[pl.ds(pl.multiple_of((first + b) * rows, rows), rows)], xbuf.at[slot], in_sem.at[slot])

    def out_copy(b, slot):
        return pltpu.make_async_copy(
            ybuf.at[slot], y_hbm.at[pl.ds(pl.multiple_of((first + b) * rows, rows), rows)], out_sem.at[slot])

    @pl.when(n > 0)
    def _():
        in_copy(0, 0).start()
        wg_s[...] = wg_ref[...].astype(BF16)
        wu_s[...] = wu_ref[...].astype(BF16)
        wd_s[...] = wd_ref[...].astype(BF16)

    def block(b, c):
        slot = b % 2
        in_copy(b, slot).wait()

        @pl.when(b + 1 < n)
        def _():
            in_copy(b + 1, 1 - slot).start()

        @pl.when(b >= 2)
        def _():
            out_copy(b - 2, slot).wait()

        x = _from_token_tiles(xbuf.at[slot], MOE_BM).astype(BF16)
        hg = jnp.minimum(_dot(x, wg_s[...]) + bg_ref[...], SWIGLU_LIMIT)
        hu = jnp.clip(_dot(x, wu_s[...]) + bu_ref[...], -SWIGLU_LIMIT, SWIGLU_LIMIT)
        hh = (hu + 1.0) * hg * _sigmoid(SWIGLU_ALPHA * hg)
        _to_token_tiles(ybuf.at[slot], _dot(hh.astype(BF16), wd_s[...]) + bd_ref[...])
        out_copy(b, slot).start()
        return c
    lax.fori_loop(0, n, block, 0)

    @pl.when(n >= 2)
    def _():
        out_copy(n - 2, n % 2).wait()

    @pl.when(n >= 1)
    def _():
        out_copy(n - 1, (n - 1) % 2).wait()


def _experts(xs, blk0, nblk, layer, w_gate, b_gate, w_up, b_up, w_down, b_down):
    depth, n_exp, d, f = w_gate.shape

    def w_spec(k, n):
        return pl.BlockSpec((None, None, k, n), lambda e, b0, nb: (layer, e, 0, 0))

    rows = MOE_BM * ROW_TILE
    return pl.pallas_call(
        _expert_kernel,
        grid_spec=pltpu.PrefetchScalarGridSpec(
            num_scalar_prefetch=2, grid=(n_exp,),
            in_specs=[pl.BlockSpec(memory_space=pl.ANY),
                      w_spec(d, f), w_spec(1, f), w_spec(d, f), w_spec(1, f), w_spec(f, d), w_spec(1, d)],
            out_specs=pl.BlockSpec(memory_space=pl.ANY),
            scratch_shapes=[pltpu.VMEM((d, f), BF16), pltpu.VMEM((d, f), BF16), pltpu.VMEM((f, d), BF16),
                            pltpu.VMEM((2, rows, LANES), F32), pltpu.VMEM((2, rows, LANES), F32),
                            pltpu.SemaphoreType.DMA((2,)), pltpu.SemaphoreType.DMA((2,))]),
        out_shape=jax.ShapeDtypeStruct(xs.shape, F32),
        compiler_params=_params("arbitrary"),
        name="moe_experts",
    )(blk0, nblk, xs, w_gate, b_gate.reshape(depth, n_exp, 1, f), w_up, b_up.reshape(depth, n_exp, 1, f),
      w_down, b_down.reshape(depth, n_exp, 1, d))


def _combine_kernel(grp, dest_ref, dest_next_ref, x_ref, gate_ref, mod_ref, gpost_ref, y_hbm,
                    *rest, split):
    del grp
    outs, (ybuf, sem, acc_s) = rest[:-3], rest[-3:]
    i = pl.program_id(0)
    n = pl.num_programs(0)
    rows = TM * ROW_TILE

    def issue(dref, slot):
        def body(g, c):
            for j in range(DMA_UNROLL):
                t = g * DMA_UNROLL + j
                for kk in range(TOP_K):
                    _tile_copy(y_hbm, dref[t * TOP_K + kk], ybuf.at[slot, kk], t, sem.at[slot]).start(
                        priority=kk % 2)
            return c
        lax.fori_loop(0, TM // DMA_UNROLL, body, 0)

    @pl.when(i == 0)
    def _():
        issue(dest_ref, 0)

    @pl.when(i + 1 < n)
    def _():
        issue(dest_next_ref, (i + 1) % 2)

    slot = i % 2

    for kk in range(TOP_K):
        pltpu.make_async_copy(y_hbm.at[pl.ds(0, rows)], ybuf.at[slot, kk], sem.at[slot]).wait()

    gates = gate_ref[...]
    for s in range(ROW_TILE):
        sl = slice(s * LANES, (s + 1) * LANES)
        acc = gates[:, 0:1] * ybuf[slot, 0, pl.ds(s, TM, stride=ROW_TILE), :]
        for kk in range(1, TOP_K):
            acc = acc + gates[:, kk:kk + 1] * ybuf[slot, kk, pl.ds(s, TM, stride=ROW_TILE), :]
        acc_s[:, sl] = acc
    result = x_ref[...] + mod_ref[5:6] * _rms(acc_s[...], gpost_ref[...])
    if split is None:
        outs[0][...] = result
    else:
        @pl.when(i < split)
        def _():
            outs[0][...] = result

        @pl.when(i >= split)
        def _():
            outs[1][...] = result


def _combine(x, y, dest, gates, mod, grp, g_post, split):
    t, d = x.shape
    n_tiles = t // TM
    tile = pl.BlockSpec((TM, d), lambda i, g: (i, 0))
    if split is None:
        out_specs, out_shape, aliases = tile, jax.ShapeDtypeStruct((t, d), F32), {3: 0}
    else:
        out_specs = [pl.BlockSpec((TM, d), lambda i, g: (jnp.minimum(i, split - 1), 0)),
                     pl.BlockSpec((TM, d), lambda i, g: (jnp.maximum(i - split, 0), 0))]
        out_shape = [jax.ShapeDtypeStruct((split * TM, d), F32),
                     jax.ShapeDtypeStruct((t - split * TM, d), F32)]
        aliases = {}
    return pl.pallas_call(
        functools.partial(_combine_kernel, split=split),
        grid_spec=pltpu.PrefetchScalarGridSpec(
            num_scalar_prefetch=1, grid=(n_tiles,),
            in_specs=[pl.BlockSpec((TM * TOP_K,), lambda i, g: (i,), memory_space=pltpu.SMEM),
                      pl.BlockSpec((TM * TOP_K,), lambda i, g: (jnp.minimum(i + 1, n_tiles - 1),),
                                   memory_space=pltpu.SMEM),
                      tile,
                      pl.BlockSpec((TM, LANES), lambda i, g: (i, 0)),
                      pl.BlockSpec((None, 6, d), lambda i, g: (g[i], 0, 0)),
                      pl.BlockSpec((1, d), lambda i, g: (0, 0)),
                      pl.BlockSpec(memory_space=pl.ANY)],
            out_specs=out_specs,
            scratch_shapes=[pltpu.VMEM((2, TOP_K, TM * ROW_TILE, LANES), F32),
                            pltpu.SemaphoreType.DMA((2,)), pltpu.VMEM((TM, d), F32)]),
        out_shape=out_shape,
        input_output_aliases=aliases,
        compiler_params=_params("arbitrary"),
        name="moe_combine",
    )(grp, dest, dest, x, gates, mod, g_post.reshape(1, d), y)


def _moe_layer(x, mod, grp, layer, g_pre, g_post, w_router, b_router,
               w_gate, b_gate, w_up, b_up, w_down, b_down, split=None):
    t, d = x.shape
    n_exp = w_router.shape[1]
    assert d == ROW_TILE * LANES
    h3, route, gates, cnt = _router(x, mod, grp, g_pre, w_router, b_router)
    idx = route[:, 0:TOP_K]
    rank = route[:, TOP_K:2 * TOP_K]
    counts = cnt[0, :n_exp].astype(I32)
    padded = (counts + MOE_BM - 1) // MOE_BM * MOE_BM
    pad_end = jnp.cumsum(padded)
    pad_start = pad_end - padded
    experts = jnp.arange(n_exp, dtype=I32)
    dest = (rank + jnp.sum(jnp.where(idx[..., None] == experts, pad_start, 0), axis=-1)).reshape(-1)
    n_blocks = t * TOP_K // MOE_BM + n_exp
    xs = _dispatch(h3, dest, pad_start + counts, pad_end, n_blocks * MOE_BM)
    y = _experts(xs, pad_start // MOE_BM, padded // MOE_BM, layer,
                 w_gate, b_gate, w_up, b_up, w_down, b_down)
    return _combine(x, y, dest, gates, mod, grp, g_post, split)


def kernel(x_prompt, x_sample, state_hgrn, cache_k, cache_v, c, c_ctx, w_mod, b_mod, g_pre_mix, g_post_mix, g_pre_ffn, g_post_ffn, hgrn_w_in, hgrn_gamma, hgrn_g_norm, hgrn_w_out, attn_w_qkv, attn_q_gain, attn_k_gain, attn_w_o, moe_w_router, moe_b_router, moe_w_gate, moe_b_gate, moe_w_up, moe_b_up, moe_w_down, moe_b_down):
    bp, tp, d = x_prompt.shape
    bs, ts, _ = x_sample.shape
    depth = w_mod.shape[0]
    n_heads = d // HGRN_DK
    nk = N_KV_HEADS * HEAD_DIM
    assert tp % TM == 0 and ts % TM == 0 and ts % GRID_W == 0 and d % LANES == 0
    n_ctx = bp * tp
    ctx_tiles, lat_tiles = tp // TM, ts // TM
    geom = (bp, ctx_tiles, bs, lat_tiles)
    grp = _tile_groups(bp * ctx_tiles, bs, lat_tiles)

    x = jnp.concatenate([x_prompt.reshape(n_ctx, d), x_sample.reshape(bs * ts, d)], axis=0)

    rows = -(-(1 + bs) // 8) * 8
    cvec = jnp.zeros((rows, d), F32).at[0].set(c_ctx).at[1:1 + bs].set(c)
    mod_all = _mod_table(cvec, w_mod, b_mod).reshape(depth, rows, 6, d)

    lb_all = jnp.cumsum(jax.nn.softmax(hgrn_gamma.astype(F32), axis=0), axis=0)

    new_states, new_k, new_v = [], [], []
    for l in range(depth):
        mod = mod_all[l]
        if l % 2 == 0:
            a = l // 2
            s0 = jnp.concatenate(
                [jnp.zeros((bp, 2, n_heads, HGRN_DK, HGRN_DK), F32), state_hgrn[:, a].astype(F32)], axis=0)
            x, s_f, s_b = _hgrn_layer(x, mod, g_pre_mix[l], g_post_mix[l], hgrn_w_in[a], lb_all[l],
                                      hgrn_g_norm[a], hgrn_w_out[a], s0, geom)
            new_states.append(jnp.stack([s_f[:bp], s_b[:bp]], axis=1))
        else:
            m = l // 2
            w_qkv = attn_w_qkv[m].astype(BF16)
            q_c, k_c, v_c = _qkv(x, mod, grp, g_pre_mix[l], w_qkv, attn_q_gain[m], attn_k_gain[m],
                                 0, bp * ctx_tiles, 0, F32)
            q_l, k_l, v_l = _qkv(x, mod, grp, g_pre_mix[l], w_qkv, attn_q_gain[m], attn_k_gain[m],
                                 bp * ctx_tiles, bs * lat_tiles, ts, BF16)
            o_c = _attention(q_c, k_c, v_c, bp, tp)
            past = cache_k.shape[2]
            o_l = _attention(q_l, k_l, v_l, bs, ts,
                             cache=(cache_k[:, m].reshape(bs, past, nk), cache_v[:, m].reshape(bs, past, nk)))
            new_k.append(k_c.reshape(bp, tp, N_KV_HEADS, HEAD_DIM))
            new_v.append(v_c.reshape(bp, tp, N_KV_HEADS, HEAD_DIM))
            x = _attn_out(x, o_c, o_l, mod, grp, attn_w_o[m].astype(BF16), g_post_mix[l])
        x = _moe_layer(x, mod, grp, l, g_pre_ffn[l], g_post_ffn[l], moe_w_router[l], moe_b_router[l],
                       moe_w_gate, moe_b_gate, moe_w_up, moe_b_up, moe_w_down, moe_b_down,
                       split=bp * ctx_tiles if l == depth - 1 else None)

    y_prompt, y_sample = x
    return (y_prompt.reshape(bp, tp, d), y_sample.reshape(bs, ts, d), jnp.stack(new_states, axis=1),
            jnp.stack(new_k, axis=1), jnp.stack(new_v, axis=1))
```

```python
import functools

import numpy as np
import jax
import jax.numpy as jnp
from jax import lax
from jax.experimental import pallas as pl
from jax.experimental.pallas import tpu as pltpu

F32 = jnp.float32
BF16 = jnp.bfloat16
I32 = jnp.int32

EPS = 1e-6
GRID_W = 64
ROPE_THETA = 10000.0
HGRN_DK = 128
HEAD_DIM = 128
N_KV_HEADS = 2
N_EXPERTS = 32
TOP_K = 4
SWIGLU_LIMIT = 7.0
SWIGLU_ALPHA = 1.702

LANES = 128
TM = 256
MM_TILE = 512
GLA_CHUNK = 64
MOE_BM = 512
ROW_TILE = 8
DMA_UNROLL = 8
MOD_COLS = 512
VMEM_LIMIT = 56 * 1024 * 1024
NEG_BIG = -1e30

NT_DIMS = (((1,), (1,)), ((), ()))
TN_DIMS = (((0,), (0,)), ((), ()))


def _params(*sem):
    return pltpu.CompilerParams(dimension_semantics=sem, vmem_limit_bytes=VMEM_LIMIT)


def _sigmoid(x):
    return 1.0 / (1.0 + jnp.exp(-x))


def _silu(x):
    return x * _sigmoid(x)


def _rms(x, gain):
    ms = jnp.mean(x * x, axis=-1, keepdims=True)
    return x * lax.rsqrt(ms + EPS) * gain


def _dot(a, b):
    return jnp.dot(a, b, preferred_element_type=F32)


def _mod_kernel(c_ref, w_ref, b_ref, o_ref):
    s = _silu(c_ref[...]).astype(BF16)
    o_ref[...] = _dot(s, w_ref[...].astype(BF16)) + b_ref[...]


def _mod_table(cvec, w_mod, b_mod):
    depth, d, n = w_mod.shape
    rows = cvec.shape[0]
    return pl.pallas_call(
        _mod_kernel,
        grid=(depth, n // MOD_COLS),
        in_specs=[
            pl.BlockSpec((rows, d), lambda l, j: (0, 0)),
            pl.BlockSpec((None, d, MOD_COLS), lambda l, j: (l, 0, j)),
            pl.BlockSpec((None, 1, MOD_COLS), lambda l, j: (l, 0, j)),
        ],
        out_specs=pl.BlockSpec((None, rows, MOD_COLS), lambda l, j: (l, 0, j)),
        out_shape=jax.ShapeDtypeStruct((depth, rows, n), F32),
        compiler_params=_params("arbitrary", "arbitrary"),
        name="mod_table",
    )(cvec, w_mod, b_mod.reshape(depth, 1, n))


def _gla_tile(q_s, k_s, lf_s, v_s, st_ref, o_s, direction):
    c = GLA_CHUNK
    d_model = q_s.shape[1]
    n_heads = d_model // HGRN_DK
    ri = lax.broadcasted_iota(I32, (c, c), 0)
    ci = lax.broadcasted_iota(I32, (c, c), 1)
    keep = (ci <= ri) if direction == 0 else (ci >= ri)
    ones_tri = jnp.where(keep, 1.0, 0.0).astype(BF16)
    chunks = range(TM // c)
    if direction == 1:
        chunks = reversed(chunks)
    for ch in chunks:
        rows = pl.ds(ch * c, c)
        lf = lf_s[rows, :]
        lf_hi = lf.astype(BF16)
        lf_lo = (lf - lf_hi.astype(F32)).astype(BF16)
        cum = _dot(ones_tri, lf_hi) + _dot(ones_tri, lf_lo)
        if direction == 0:
            btot, bmid = cum[c - 1:c], cum[c // 2 - 1:c // 2]
        else:
            btot, bmid = cum[0:1], cum[c // 2:c // 2 + 1]
        q_dec = q_s[rows, :] * jnp.exp(cum - bmid)
        k_inv = k_s[rows, :] * jnp.exp(bmid - cum)
        q_in = (q_dec * jnp.exp(bmid)).astype(BF16)
        k_end = (k_inv * jnp.exp(btot - bmid)).astype(BF16)
        q_dec = q_dec.astype(BF16)
        k_inv = k_inv.astype(BF16)
        e_tot = jnp.exp(btot)
        vc = v_s[rows, :].astype(BF16)
        for h in range(n_heads):
            sl = slice(h * HGRN_DK, (h + 1) * HGRN_DK)
            a = lax.dot_general(q_dec[:, sl], k_inv[:, sl], NT_DIMS, preferred_element_type=F32)
            a = jnp.where(keep, a, 0.0).astype(BF16)
            st = st_ref[h]
            o = _dot(a, vc[:, sl]) + lax.dot_general(
                q_in[:, sl], st.astype(BF16), NT_DIMS, preferred_element_type=F32)
            o_s[rows, sl] = o
            st_ref[h] = st * e_tot[:, sl] + lax.dot_general(
                vc[:, sl], k_end[:, sl], TN_DIMS, preferred_element_type=F32)


def _hgrn_gates(h, w_ref, lb, q_s, k_s, lf_s, v_s):
    d = h.shape[1]
    q_s[...] = _silu(_dot(h, w_ref[:, 0:d]))
    f = lb + (1.0 - lb) * _sigmoid(_dot(h, w_ref[:, d:2 * d]))
    lf_s[...] = jnp.log(f)
    k_s[...] = 1.0 - f
    v_s[...] = _dot(h, w_ref[:, 2 * d:3 * d])


def _load_state(meta, s0_ref, st_s):
    @pl.when(meta[2, pl.program_id(0)] == 1)
    def _():
        for h in range(st_s.shape[0]):
            st_s[h] = s0_ref[h].T


def _store_state(meta, sfin_ref, st_s):
    @pl.when(meta[3, pl.program_id(0)] == 1)
    def _():
        for h in range(st_s.shape[0]):
            sfin_ref[h] = st_s[h].T


def _hgrn_project(x_ref, mod_ref, gpre_ref, w_ref, lb_ref, bufs):
    h = (_rms(x_ref[...], gpre_ref[...]) * (1.0 + mod_ref[1:2]) + mod_ref[0:1]).astype(BF16)
    _hgrn_gates(h, w_ref, lb_ref[...], *bufs[:4])
    if len(bufs) > 4:
        d = h.shape[1]
        bufs[4][...] = _silu(_dot(h, w_ref[:, 3 * d:4 * d]))


def _two_phase(first_project, step, set_a, set_b):
    s = pl.program_id(0)

    @pl.when(s == 0)
    def _():
        first_project(set_a)

    @pl.when(s % 2 == 0)
    def _():
        step(set_a, set_b)

    @pl.when(s % 2 == 1)
    def _():
        step(set_b, set_a)


def _hgrn_bwd_kernel(meta, xn_ref, modn_ref, x0_ref, mod0_ref, gpre_ref, w_ref, lb_ref, s0_ref,
                     ob_ref, sfin_ref, st_s, o_s, *bufs):
    _load_state(meta, s0_ref, st_s)

    def step(cur, nxt):
        _hgrn_project(xn_ref, modn_ref, gpre_ref, w_ref, lb_ref, nxt)
        _gla_tile(*cur, st_s, o_s, 1)
        ob_ref[...] = o_s[...].astype(BF16)

    _two_phase(lambda b: _hgrn_project(x0_ref, mod0_ref, gpre_ref, w_ref, lb_ref, b),
               step, bufs[:4], bufs[4:])
    _store_state(meta, sfin_ref, st_s)


def _hgrn_fwd_kernel(meta, xn_ref, modn_ref, x0_ref, mod0_ref, gpre_ref, w_ref, lb_ref, s0_ref,
                     x_ref, mod_ref, ob_ref, gnorm_ref, wout_ref, gpost_ref,
                     xo_ref, sfin_ref, st_s, o_s, *bufs):
    _load_state(meta, s0_ref, st_s)
    d = x_ref.shape[1]

    def step(cur, nxt):
        _hgrn_project(xn_ref, modn_ref, gpre_ref, w_ref, lb_ref, nxt)
        _gla_tile(*cur[:4], st_s, o_s, 0)
        gnorm = gnorm_ref[...]
        for hh in range(d // HGRN_DK):
            sl = slice(hh * HGRN_DK, (hh + 1) * HGRN_DK)
            o = o_s[:, sl] + ob_ref[:, sl].astype(F32)
            o_s[:, sl] = _rms(o, gnorm) * cur[4][:, sl]
        out = _dot(o_s[...].astype(BF16), wout_ref[...])
        xo_ref[...] = x_ref[...] + mod_ref[2:3] * _rms(out, gpost_ref[...])

    _two_phase(lambda b: _hgrn_project(x0_ref, mod0_ref, gpre_ref, w_ref, lb_ref, b),
               step, bufs[:5], bufs[5:])
    _store_state(meta, sfin_ref, st_s)


def _seq_meta(n_ctx_seq, ctx_tiles, n_lat_seq, lat_tiles, direction):
    cols = []
    tile0 = 0
    for s in range(n_ctx_seq + n_lat_seq):
        n = ctx_tiles if s < n_ctx_seq else lat_tiles
        grp = 0 if s < n_ctx_seq else 1 + s - n_ctx_seq
        order = range(n) if direction == 0 else range(n - 1, -1, -1)
        for j, t in enumerate(order):
            cols.append([tile0 + t, s, int(j == 0), int(j == n - 1), grp])
        tile0 += n
    for i, col in enumerate(cols):
        nxt = cols[min(i + 1, len(cols) - 1)]
        col += [nxt[0], nxt[4]]
    return jnp.asarray(np.array(cols, dtype=np.int32).T)


def _tile_groups(n_ctx_tiles, n_lat_seq, lat_tiles):
    g = [0] * n_ctx_tiles
    for b in range(n_lat_seq):
        g += [1 + b] * lat_tiles
    return jnp.asarray(np.array(g, dtype=np.int32))


def _hgrn_layer(x, mod, g_pre, g_post, w_in, lb, g_norm, w_out, s0, geom):
    t, d = x.shape
    n_heads = d // HGRN_DK
    n_seq = s0.shape[0]
    n_tiles = t // TM
    w_bwd = jnp.concatenate([w_in[:, 0:d], w_in[:, 2 * d:3 * d], w_in[:, 3 * d:4 * d]], axis=1).astype(BF16)
    w_fwd = jnp.concatenate([w_in[:, 0:d], w_in[:, d:2 * d], w_in[:, 3 * d:5 * d]], axis=1).astype(BF16)
    state_spec = pl.BlockSpec((None, n_heads, HGRN_DK, HGRN_DK), lambda i, m: (m[1, i], 0, 0, 0))
    tile_spec = pl.BlockSpec((TM, d), lambda i, m: (m[0, i], 0))
    mod_spec = pl.BlockSpec((None, 6, d), lambda i, m: (m[4, i], 0, 0))
    next_tile = pl.BlockSpec((TM, d), lambda i, m: (m[5, i], 0))
    next_mod = pl.BlockSpec((None, 6, d), lambda i, m: (m[6, i], 0, 0))
    first_tile = pl.BlockSpec((TM, d), lambda i, m: (m[0, 0], 0))
    first_mod = pl.BlockSpec((None, 6, d), lambda i, m: (m[4, 0], 0, 0))
    vec_spec = pl.BlockSpec((1, d), lambda i, m: (0, 0))
    project_specs = [next_tile, next_mod, first_tile, first_mod, vec_spec]

    def scratch(n_bufs):
        return ([pltpu.VMEM((n_heads, HGRN_DK, HGRN_DK), F32)]
                + [pltpu.VMEM((TM, d), F32)] * (1 + 2 * n_bufs))

    meta_b = _seq_meta(*geom, 1)
    o_bwd, s_bwd = pl.pallas_call(
        _hgrn_bwd_kernel,
        grid_spec=pltpu.PrefetchScalarGridSpec(
            num_scalar_prefetch=1, grid=(n_tiles,),
            in_specs=project_specs + [pl.BlockSpec((d, 3 * d), lambda i, m: (0, 0)), vec_spec, state_spec],
            out_specs=[tile_spec, state_spec],
            scratch_shapes=scratch(4)),
        out_shape=[jax.ShapeDtypeStruct((t, d), BF16),
                   jax.ShapeDtypeStruct((n_seq, n_heads, HGRN_DK, HGRN_DK), F32)],
        compiler_params=_params("arbitrary"),
        name="hgrn_bwd",
    )(meta_b, x, mod, x, mod, g_pre.reshape(1, d), w_bwd, lb[1].reshape(1, d), s0[:, 1])

    meta_f = _seq_meta(*geom, 0)
    x_new, s_fwd = pl.pallas_call(
        _hgrn_fwd_kernel,
        grid_spec=pltpu.PrefetchScalarGridSpec(
            num_scalar_prefetch=1, grid=(n_tiles,),
            in_specs=project_specs + [pl.BlockSpec((d, 4 * d), lambda i, m: (0, 0)), vec_spec, state_spec,
                                      tile_spec, mod_spec, tile_spec,
                                      pl.BlockSpec((1, HGRN_DK), lambda i, m: (0, 0)),
                                      pl.BlockSpec((d, d), lambda i, m: (0, 0)), vec_spec],
            out_specs=[tile_spec, state_spec],
            scratch_shapes=scratch(5)),
        out_shape=[jax.ShapeDtypeStruct((t, d), F32),
                   jax.ShapeDtypeStruct((n_seq, n_heads, HGRN_DK, HGRN_DK), F32)],
        compiler_params=_params("arbitrary"),
        name="hgrn_fwd",
    )(meta_f, x, mod, x, mod, g_pre.reshape(1, d), w_fwd, lb[0].reshape(1, d), s0[:, 0],
      x, mod, o_bwd, g_norm.reshape(1, HGRN_DK), w_out.astype(BF16), g_post.reshape(1, d))
    return x_new, s_fwd, s_bwd


def _rope_tables(n_tok):
    axis_dim = HEAD_DIM // 2
    half = axis_dim // 2
    pos = np.arange(n_tok)
    row, col = pos // GRID_W, pos % GRID_W
    inv_freq = ROPE_THETA ** (-np.arange(0, axis_dim, 2, dtype=np.float64) / axis_dim)
    lane = np.arange(HEAD_DIM)
    p = np.where(lane[None, :] < axis_dim, row[:, None], col[:, None]).astype(np.float64)
    ang = p * inv_freq[(lane % axis_dim) % half][None, :]
    sign = np.where((lane % axis_dim) < half, -1.0, 1.0)[None, :]
    return jnp.asarray(np.cos(ang), F32), jnp.asarray(np.sin(ang) * sign, F32)


def _rope(xh, cos, sin_signed):
    axis_dim = HEAD_DIM // 2
    half = axis_dim // 2
    lane = lax.broadcasted_iota(I32, xh.shape, 1)
    upper = pltpu.roll(xh, HEAD_DIM - half, 1)
    lower = pltpu.roll(xh, half, 1)
    partner = jnp.where((lane % axis_dim) < half, upper, lower)
    return xh * cos + partner * sin_signed


def _qkv_kernel(grp, x_ref, mod_ref, gpre_ref, w_ref, qg_ref, kg_ref, *rest, rope):
    del grp
    if rope:
        cos_ref, sin_ref, q_out, k_out, v_out = rest
    else:
        q_out, k_out, v_out = rest
    d = x_ref.shape[1]
    nk = N_KV_HEADS * HEAD_DIM
    h = (_rms(x_ref[...], gpre_ref[...]) * (1.0 + mod_ref[1:2]) + mod_ref[0:1]).astype(BF16)
    q = _dot(h, w_ref[:, 0:d])
    k = _dot(h, w_ref[:, d:d + nk])
    v_out[...] = _dot(h, w_ref[:, d + nk:d + 2 * nk]).astype(v_out.dtype)
    scale = HEAD_DIM ** -0.5
    for hh in range(d // HEAD_DIM):
        sl = slice(hh * HEAD_DIM, (hh + 1) * HEAD_DIM)
        qh = _rms(q[:, sl], qg_ref[...])
        if rope:
            qh = _rope(qh, cos_ref[...], sin_ref[...])
        q_out[:, sl] = (qh * scale).astype(q_out.dtype)
    for hh in range(N_KV_HEADS):
        sl = slice(hh * HEAD_DIM, (hh + 1) * HEAD_DIM)
        kh = _rms(k[:, sl], kg_ref[...])
        if rope:
            kh = _rope(kh, cos_ref[...], sin_ref[...])
        k_out[:, sl] = kh.astype(k_out.dtype)


def _qkv(x, mod, grp, g_pre, w_qkv, q_gain, k_gain, tile0, n_tiles, rope_len, kv_dtype):
    t, d = x.shape
    nk = N_KV_HEADS * HEAD_DIM
    n_out = n_tiles * MM_TILE
    tile_in = pl.BlockSpec((MM_TILE, d), lambda i, g: (tile0 + i, 0))
    in_specs = [tile_in,
                pl.BlockSpec((None, 6, d), lambda i, g: (g[tile0 + i], 0, 0)),
                pl.BlockSpec((1, d), lambda i, g: (0, 0)),
                pl.BlockSpec((d, d + 2 * nk), lambda i, g: (0, 0)),
                pl.BlockSpec((1, HEAD_DIM), lambda i, g: (0, 0)),
                pl.BlockSpec((1, HEAD_DIM), lambda i, g: (0, 0))]
    args = [grp, x, mod, g_pre.reshape(1, d), w_qkv, q_gain.reshape(1, HEAD_DIM), k_gain.reshape(1, HEAD_DIM)]
    if rope_len:
        per_seq = rope_len // MM_TILE
        cos, sin = _rope_tables(rope_len)
        in_specs += [pl.BlockSpec((MM_TILE, HEAD_DIM), lambda i, g: (i % per_seq, 0))] * 2
        args += [cos, sin]
    return pl.pallas_call(
        functools.partial(_qkv_kernel, rope=bool(rope_len)),
        grid_spec=pltpu.PrefetchScalarGridSpec(
            num_scalar_prefetch=1, grid=(n_tiles,), in_specs=in_specs,
            out_specs=[pl.BlockSpec((MM_TILE, d), lambda i, g: (i, 0)),
                       pl.BlockSpec((MM_TILE, nk), lambda i, g: (i, 0)),
                       pl.BlockSpec((MM_TILE, nk), lambda i, g: (i, 0))]),
        out_shape=[jax.ShapeDtypeStruct((n_out, d), BF16),
                   jax.ShapeDtypeStruct((n_out, nk), kv_dtype),
                   jax.ShapeDtypeStruct((n_out, nk), kv_dtype)],
        compiler_params=_params("arbitrary"),
        name="qkv_rope" if rope_len else "qkv",
    )(*args)


def _attn_kernel(q_ref, *rest, n_pieces):
    kv, o_ref = rest[:-1], rest[-1]
    ks = [kv[2 * p][...].astype(BF16) for p in range(n_pieces)]
    vs = [kv[2 * p + 1][...].astype(BF16) for p in range(n_pieces)]
    group = q_ref.shape[1] // HEAD_DIM
    for g in range(group):
        sl = slice(g * HEAD_DIM, (g + 1) * HEAD_DIM)
        qh = q_ref[:, sl]
        ss = [lax.dot_general(qh, kp, NT_DIMS, preferred_element_type=F32) for kp in ks]
        m = ss[0].max(axis=-1, keepdims=True)
        for s in ss[1:]:
            m = jnp.maximum(m, s.max(axis=-1, keepdims=True))
        ps = [jnp.exp(s - m) for s in ss]
        denom = ps[0].sum(axis=-1, keepdims=True)
        for p in ps[1:]:
            denom = denom + p.sum(axis=-1, keepdims=True)
        o = _dot(ps[0].astype(BF16), vs[0])
        for p, vp in zip(ps[1:], vs[1:]):
            o = o + _dot(p.astype(BF16), vp)
        o_ref[:, sl] = (o / denom).astype(o_ref.dtype)


def _attention(q, k_new, v_new, n_seq, seq_len, cache=None):
    t, d = q.shape
    group_w = d // N_KV_HEADS
    nq = seq_len // TM
    q_spec = pl.BlockSpec((TM, group_w), lambda b, i, kv: (b * nq + i, kv))
    new_spec = pl.BlockSpec((seq_len, HEAD_DIM), lambda b, i, kv: (b, kv))
    in_specs, args = [q_spec], [q]
    if cache is not None:
        past = cache[0].shape[1]
        c_spec = pl.BlockSpec((None, past, HEAD_DIM), lambda b, i, kv: (b, 0, kv))
        in_specs += [c_spec, c_spec]
        args += list(cache)
    in_specs += [new_spec, new_spec]
    args += [k_new, v_new]
    return pl.pallas_call(
        functools.partial(_attn_kernel, n_pieces=len(args) // 2),
        grid=(n_seq, nq, N_KV_HEADS),
        in_specs=in_specs,
        out_specs=q_spec,
        out_shape=jax.ShapeDtypeStruct((t, d), BF16),
        compiler_params=_params("arbitrary", "arbitrary", "arbitrary"),
        name="attention",
    )(*args)


def _attn_out_kernel(grp, x_ref, oc_ref, ol_ref, mod_ref, w_ref, gpost_ref, xo_ref, *, n_ctx_tiles):
    del grp
    o = jnp.where(pl.program_id(0) < n_ctx_tiles, oc_ref[...], ol_ref[...])
    out = _dot(o, w_ref[...])
    xo_ref[...] = x_ref[...] + mod_ref[2:3] * _rms(out, gpost_ref[...])


def _attn_out(x, o_ctx, o_lat, mod, grp, w_o, g_post):
    t, d = x.shape
    nct = o_ctx.shape[0] // MM_TILE
    tile = pl.BlockSpec((MM_TILE, d), lambda i, g: (i, 0))
    return pl.pallas_call(
        functools.partial(_attn_out_kernel, n_ctx_tiles=nct),
        grid_spec=pltpu.PrefetchScalarGridSpec(
            num_scalar_prefetch=1, grid=(t // MM_TILE,),
            in_specs=[tile,
                      pl.BlockSpec((MM_TILE, d), lambda i, g: (jnp.minimum(i, nct - 1), 0)),
                      pl.BlockSpec((MM_TILE, d), lambda i, g: (jnp.maximum(i - nct, 0), 0)),
                      pl.BlockSpec((None, 6, d), lambda i, g: (g[i], 0, 0)),
                      pl.BlockSpec((d, d), lambda i, g: (0, 0)),
                      pl.BlockSpec((1, d), lambda i, g: (0, 0))],
            out_specs=tile),
        out_shape=jax.ShapeDtypeStruct((t, d), F32),
        input_output_aliases={1: 0},
        compiler_params=_params("arbitrary"),
        name="attn_out",
    )(grp, x, o_ctx, o_lat, mod, w_o, g_post.reshape(1, d))


def _to_token_tiles(ref, x):
    rows = x.shape[0]
    for s in range(x.shape[1] // LANES):
        ref[pl.ds(s, rows, stride=ROW_TILE), :] = x[:, s * LANES:(s + 1) * LANES]


def _from_token_tiles(ref, rows):
    n = ROW_TILE
    return jnp.concatenate([ref[pl.ds(s, rows, stride=n), :] for s in range(n)], axis=1)


def _router_kernel(grp, x_ref, mod_ref, g_ref, wr_ref, br_ref,
                   h_ref, route_ref, gate_ref, cnt_ref, carry):
    del grp
    @pl.when(pl.program_id(0) == 0)
    def _():
        carry[...] = jnp.zeros_like(carry)
    h = _rms(x_ref[...], g_ref[...]) * (1.0 + mod_ref[4:5]) + mod_ref[3:4]
    _to_token_tiles(h_ref, h)
    logits = _dot(h.astype(BF16), wr_ref[...]) + br_ref[...]
    lane = lax.broadcasted_iota(I32, logits.shape, 1)
    lane_f = lane.astype(F32)
    vals, idxs = [], []
    work = logits
    for _ in range(TOP_K):
        m = work.max(axis=-1, keepdims=True)
        idx = jnp.where(work == m, lane_f, float(LANES)).min(axis=-1, keepdims=True).astype(I32)
        vals.append(m)
        idxs.append(idx)
        work = jnp.where(lane == idx, -jnp.inf, work)
    es = [jnp.exp(v - vals[0]) for v in vals]
    denom = es[0]
    for e in es[1:]:
        denom = denom + e
    onehot = jnp.zeros(logits.shape, F32)
    for idx in idxs:
        onehot = onehot + jnp.where(lane == idx, 1.0, 0.0)
    ri = lax.broadcasted_iota(I32, (MM_TILE, MM_TILE), 0)
    ci = lax.broadcasted_iota(I32, (MM_TILE, MM_TILE), 1)
    before = jnp.where(ci < ri, 1.0, 0.0).astype(BF16)
    rank_all = _dot(before, onehot.astype(BF16)) + carry[0:1]
    route = jnp.zeros(logits.shape, I32)
    gate = jnp.zeros(logits.shape, F32)
    for kk in range(TOP_K):
        rank = jnp.where(lane == idxs[kk], rank_all, 0.0).sum(axis=-1, keepdims=True)
        route = jnp.where(lane == kk, idxs[kk], route)
        route = jnp.where(lane == TOP_K + kk, rank.astype(I32), route)
        gate = jnp.where(lane == kk, es[kk] / denom, gate)
    route_ref[...] = route
    gate_ref[...] = gate
    carry[...] = carry[...] + onehot.sum(axis=0, keepdims=True)
    cnt_ref[...] = carry[...]


def _router(x, mod, grp, g_pre, w_router, b_router):
    t, d = x.shape
    n_exp = w_router.shape[1]
    wr = jnp.pad(w_router.astype(BF16), ((0, 0), (0, LANES - n_exp)))
    br = jnp.pad(b_router.reshape(1, n_exp), ((0, 0), (0, LANES - n_exp)), constant_values=NEG_BIG)
    tile = pl.BlockSpec((MM_TILE, d), lambda i, g: (i, 0))
    lane_tile = pl.BlockSpec((MM_TILE, LANES), lambda i, g: (i, 0))
    return pl.pallas_call(
        _router_kernel,
        grid_spec=pltpu.PrefetchScalarGridSpec(
            num_scalar_prefetch=1, grid=(t // MM_TILE,),
            in_specs=[tile,
                      pl.BlockSpec((None, 6, d), lambda i, g: (g[i], 0, 0)),
                      pl.BlockSpec((1, d), lambda i, g: (0, 0)),
                      pl.BlockSpec((d, LANES), lambda i, g: (0, 0)),
                      pl.BlockSpec((1, LANES), lambda i, g: (0, 0))],
            out_specs=[pl.BlockSpec((MM_TILE * ROW_TILE, LANES), lambda i, g: (i, 0)), lane_tile, lane_tile,
                       pl.BlockSpec((8, LANES), lambda i, g: (0, 0))],
            scratch_shapes=[pltpu.VMEM((8, LANES), F32)]),
        out_shape=[jax.ShapeDtypeStruct((t * ROW_TILE, LANES), F32),
                   jax.ShapeDtypeStruct((t, LANES), I32),
                   jax.ShapeDtypeStruct((t, LANES), F32),
                   jax.ShapeDtypeStruct((8, LANES), F32)],
        compiler_params=_params("arbitrary"),
        name="router",
    )(grp, x, mod, g_pre.reshape(1, d), wr, br)


def _tile_copy(src, src_row, dst, dst_row, sem):
    n = ROW_TILE
    return pltpu.make_async_copy(src.at[pl.ds(pl.multiple_of(src_row * n, n), n)],
                                 dst.at[pl.ds(pl.multiple_of(dst_row * n, n), n)], sem)


def _dispatch_kernel(pad_lo, pad_hi, dest_ref, h_ref, xs_hbm, zero_s, sem, zsem):
    def issue(g, c):
        for j in range(DMA_UNROLL):
            t = g * DMA_UNROLL + j
            for kk in range(TOP_K):
                _tile_copy(h_ref, t, xs_hbm, dest_ref[t * TOP_K + kk], sem).start(priority=kk % 2)
        return c
    lax.fori_loop(0, TM // DMA_UNROLL, issue, 0)

    @pl.when(pl.program_id(0) == 0)
    def _():
        zero_s[...] = jnp.zeros_like(zero_s)

        def per_expert(e, c):
            def zstart(s, c2):
                _tile_copy(zero_s, 0, xs_hbm, s, zsem).start()
                return c2
            lax.fori_loop(pad_lo[e], pad_hi[e], zstart, 0)

            def zwait(s, c2):
                _tile_copy(zero_s, 0, xs_hbm, s, zsem).wait()
                return c2
            lax.fori_loop(pad_lo[e], pad_hi[e], zwait, 0)
            return c
        lax.fori_loop(0, pad_lo.shape[0], per_expert, 0)

    for _ in range(TOP_K):
        pltpu.make_async_copy(h_ref, xs_hbm.at[pl.ds(0, TM * ROW_TILE)], sem).wait()


def _dispatch(h3, dest, pad_lo, pad_hi, n_slots):
    t = h3.shape[0] // ROW_TILE
    return pl.pallas_call(
        _dispatch_kernel,
        grid_spec=pltpu.PrefetchScalarGridSpec(
            num_scalar_prefetch=2, grid=(t // TM,),
            in_specs=[pl.BlockSpec((TM * TOP_K,), lambda i, lo, hi: (i,), memory_space=pltpu.SMEM),
                      pl.BlockSpec((TM * ROW_TILE, LANES), lambda i, lo, hi: (i, 0))],
            out_specs=pl.BlockSpec(memory_space=pl.ANY),
            scratch_shapes=[pltpu.VMEM((ROW_TILE, LANES), F32),
                            pltpu.SemaphoreType.DMA, pltpu.SemaphoreType.DMA]),
        out_shape=jax.ShapeDtypeStruct((n_slots * ROW_TILE, LANES), F32),
        compiler_params=_params("arbitrary"),
        name="moe_dispatch",
    )(pad_lo, pad_hi, dest, h3)


def _expert_kernel(blk0, nblk, xs_hbm, wg_ref, bg_ref, wu_ref, bu_ref, wd_ref, bd_ref,
                   y_hbm, wg_s, wu_s, wd_s, xbuf, ybuf, in_sem, out_sem):
    e = pl.program_id(0)
    first = blk0[e]
    n = nblk[e]
    rows = MOE_BM * ROW_TILE

    def in_copy(b, slot):
        return pltpu.make_async_copy(
            xs_hbm.at[pl.ds(pl.multiple_of((first + b) * rows, rows), rows)], xbuf.at[slot], in_sem.at[slot])

    def out_copy(b, slot):
        return pltpu.make_async_copy(
            ybuf.at[slot], y_hbm.at[pl.ds(pl.multiple_of((first + b) * rows, rows), rows)], out_sem.at[slot])

    @pl.when(n > 0)
    def _():
        in_copy(0, 0).start()
        wg_s[...] = wg_ref[...].astype(BF16)
        wu_s[...] = wu_ref[...].astype(BF16)
        wd_s[...] = wd_ref[...].astype(BF16)

    def block(b, c):
        slot = b % 2
        in_copy(b, slot).wait()

        @pl.when(b + 1 < n)
        def _():
            in_copy(b + 1, 1 - slot).start()

        @pl.when(b >= 2)
        def _():
            out_copy(b - 2, slot).wait()

        x = _from_token_tiles(xbuf.at[slot], MOE_BM).astype(BF16)
        hg = jnp.minimum(_dot(x, wg_s[...]) + bg_ref[...], SWIGLU_LIMIT)
        hu = jnp.clip(_dot(x, wu_s[...]) + bu_ref[...], -SWIGLU_LIMIT, SWIGLU_LIMIT)
        hh = (hu + 1.0) * hg * _sigmoid(SWIGLU_ALPHA * hg)
        _to_token_tiles(ybuf.at[slot], _dot(hh.astype(BF16), wd_s[...]) + bd_ref[...])
        out_copy(b, slot).start()
        return c
    lax.fori_loop(0, n, block, 0)

    @pl.when(n >= 2)
    def _():
        out_copy(n - 2, n % 2).wait()

    @pl.when(n >= 1)
    def _():
        out_copy(n - 1, (n - 1) % 2).wait()


def _experts(xs, blk0, nblk, layer, w_gate, b_gate, w_up, b_up, w_down, b_down):
    depth, n_exp, d, f = w_gate.shape

    def w_spec(k, n):
        return pl.BlockSpec((None, None, k, n), lambda e, b0, nb: (layer, e, 0, 0))

    rows = MOE_BM * ROW_TILE
    return pl.pallas_call(
        _expert_kernel,
        grid_spec=pltpu.PrefetchScalarGridSpec(
            num_scalar_prefetch=2, grid=(n_exp,),
            in_specs=[pl.BlockSpec(memory_space=pl.ANY),
                      w_spec(d, f), w_spec(1, f), w_spec(d, f), w_spec(1, f), w_spec(f, d), w_spec(1, d)],
            out_specs=pl.BlockSpec(memory_space=pl.ANY),
            scratch_shapes=[pltpu.VMEM((d, f), BF16), pltpu.VMEM((d, f), BF16), pltpu.VMEM((f, d), BF16),
                            pltpu.VMEM((2, rows, LANES), F32), pltpu.VMEM((2, rows, LANES), F32),
                            pltpu.SemaphoreType.DMA((2,)), pltpu.SemaphoreType.DMA((2,))]),
        out_shape=jax.ShapeDtypeStruct(xs.shape, F32),
        compiler_params=_params("arbitrary"),
        name="moe_experts",
    )(blk0, nblk, xs, w_gate, b_gate.reshape(depth, n_exp, 1, f), w_up, b_up.reshape(depth, n_exp, 1, f),
      w_down, b_down.reshape(depth, n_exp, 1, d))


def _combine_kernel(grp, dest_ref, dest_next_ref, x_ref, gate_ref, mod_ref, gpost_ref, y_hbm,
                    *rest, split):
    del grp
    outs, (ybuf, sem, acc_s) = rest[:-3], rest[-3:]
    i = pl.program_id(0)
    n = pl.num_programs(0)
    rows = TM * ROW_TILE

    def issue(dref, slot):
        def body(g, c):
            for j in range(DMA_UNROLL):
                t = g * DMA_UNROLL + j
                for kk in range(TOP_K):
                    _tile_copy(y_hbm, dref[t * TOP_K + kk], ybuf.at[slot, kk], t, sem.at[slot]).start(
                        priority=kk % 2)
            return c
        lax.fori_loop(0, TM // DMA_UNROLL, body, 0)

    @pl.when(i == 0)
    def _():
        issue(dest_ref, 0)

    @pl.when(i + 1 < n)
    def _():
        issue(dest_next_ref, (i + 1) % 2)

    slot = i % 2

    for kk in range(TOP_K):
        pltpu.make_async_copy(y_hbm.at[pl.ds(0, rows)], ybuf.at[slot, kk], sem.at[slot]).wait()

    gates = gate_ref[...]
    for s in range(ROW_TILE):
        sl = slice(s * LANES, (s + 1) * LANES)
        acc = gates[:, 0:1] * ybuf[slot, 0, pl.ds(s, TM, stride=ROW_TILE), :]
        for kk in range(1, TOP_K):
            acc = acc + gates[:, kk:kk + 1] * ybuf[slot, kk, pl.ds(s, TM, stride=ROW_TILE), :]
        acc_s[:, sl] = acc
    result = x_ref[...] + mod_ref[5:6] * _rms(acc_s[...], gpost_ref[...])
    if split is None:
        outs[0][...] = result
    else:
        @pl.when(i < split)
        def _():
            outs[0][...] = result

        @pl.when(i >= split)
        def _():
            outs[1][...] = result


def _combine(x, y, dest, gates, mod, grp, g_post, split):
    t, d = x.shape
    n_tiles = t // TM
    tile = pl.BlockSpec((TM, d), lambda i, g: (i, 0))
    if split is None:
        out_specs, out_shape, aliases = tile, jax.ShapeDtypeStruct((t, d), F32), {3: 0}
    else:
        out_specs = [pl.BlockSpec((TM, d), lambda i, g: (jnp.minimum(i, split - 1), 0)),
                     pl.BlockSpec((TM, d), lambda i, g: (jnp.maximum(i - split, 0), 0))]
        out_shape = [jax.ShapeDtypeStruct((split * TM, d), F32),
                     jax.ShapeDtypeStruct((t - split * TM, d), F32)]
        aliases = {}
    return pl.pallas_call(
        functools.partial(_combine_kernel, split=split),
        grid_spec=pltpu.PrefetchScalarGridSpec(
            num_scalar_prefetch=1, grid=(n_tiles,),
            in_specs=[pl.BlockSpec((TM * TOP_K,), lambda i, g: (i,), memory_space=pltpu.SMEM),
                      pl.BlockSpec((TM * TOP_K,), lambda i, g: (jnp.minimum(i + 1, n_tiles - 1),),
                                   memory_space=pltpu.SMEM),
                      tile,
                      pl.BlockSpec((TM, LANES), lambda i, g: (i, 0)),
                      pl.BlockSpec((None, 6, d), lambda i, g: (g[i], 0, 0)),
                      pl.BlockSpec((1, d), lambda i, g: (0, 0)),
                      pl.BlockSpec(memory_space=pl.ANY)],
            out_specs=out_specs,
            scratch_shapes=[pltpu.VMEM((2, TOP_K, TM * ROW_TILE, LANES), F32),
                            pltpu.SemaphoreType.DMA((2,)), pltpu.VMEM((TM, d), F32)]),
        out_shape=out_shape,
        input_output_aliases=aliases,
        compiler_params=_params("arbitrary"),
        name="moe_combine",
    )(grp, dest, dest, x, gates, mod, g_post.reshape(1, d), y)


def _moe_layer(x, mod, grp, grp_mm, layer, g_pre, g_post, w_router, b_router,
               w_gate, b_gate, w_up, b_up, w_down, b_down, split=None):
    t, d = x.shape
    n_exp = w_router.shape[1]
    assert d == ROW_TILE * LANES
    h3, route, gates, cnt = _router(x, mod, grp_mm, g_pre, w_router, b_router)
    idx = route[:, 0:TOP_K]
    rank = route[:, TOP_K:2 * TOP_K]
    counts = cnt[0, :n_exp].astype(I32)
    padded = (counts + MOE_BM - 1) // MOE_BM * MOE_BM
    pad_end = jnp.cumsum(padded)
    pad_start = pad_end - padded
    experts = jnp.arange(n_exp, dtype=I32)
    dest = (rank + jnp.sum(jnp.where(idx[..., None] == experts, pad_start, 0), axis=-1)).reshape(-1)
    n_blocks = t * TOP_K // MOE_BM + n_exp
    xs = _dispatch(h3, dest, pad_start + counts, pad_end, n_blocks * MOE_BM)
    y = _experts(xs, pad_start // MOE_BM, padded // MOE_BM, layer,
                 w_gate, b_gate, w_up, b_up, w_down, b_down)
    return _combine(x, y, dest, gates, mod, grp, g_post, split)


def kernel(x_prompt, x_sample, state_hgrn, cache_k, cache_v, c, c_ctx, w_mod, b_mod, g_pre_mix, g_post_mix, g_pre_ffn, g_post_ffn, hgrn_w_in, hgrn_gamma, hgrn_g_norm, hgrn_w_out, attn_w_qkv, attn_q_gain, attn_k_gain, attn_w_o, moe_w_router, moe_b_router, moe_w_gate, moe_b_gate, moe_w_up, moe_b_up, moe_w_down, moe_b_down):
    bp, tp, d = x_prompt.shape
    bs, ts, _ = x_sample.shape
    depth = w_mod.shape[0]
    n_heads = d // HGRN_DK
    nk = N_KV_HEADS * HEAD_DIM
    assert tp % TM == 0 and ts % MM_TILE == 0 and ts % GRID_W == 0 and d % LANES == 0
    n_ctx = bp * tp
    assert n_ctx % MM_TILE == 0
    ctx_tiles, lat_tiles = tp // TM, ts // TM
    geom = (bp, ctx_tiles, bs, lat_tiles)
    grp = _tile_groups(bp * ctx_tiles, bs, lat_tiles)
    grp_mm = _tile_groups(n_ctx // MM_TILE, bs, ts // MM_TILE)

    x = jnp.concatenate([x_prompt.reshape(n_ctx, d), x_sample.reshape(bs * ts, d)], axis=0)

    rows = -(-(1 + bs) // 8) * 8
    cvec = jnp.zeros((rows, d), F32).at[0].set(c_ctx).at[1:1 + bs].set(c)
    mod_all = _mod_table(cvec, w_mod, b_mod).reshape(depth, rows, 6, d)

    lb_all = jnp.cumsum(jax.nn.softmax(hgrn_gamma.astype(F32), axis=0), axis=0)

    new_states, new_k, new_v = [], [], []
    for l in range(depth):
        mod = mod_all[l]
        if l % 2 == 0:
            a = l // 2
            s0 = jnp.concatenate(
                [jnp.zeros((bp, 2, n_heads, HGRN_DK, HGRN_DK), F32), state_hgrn[:, a].astype(F32)], axis=0)
            x, s_f, s_b = _hgrn_layer(x, mod, g_pre_mix[l], g_post_mix[l], hgrn_w_in[a], lb_all[l],
                                      hgrn_g_norm[a], hgrn_w_out[a], s0, geom)
            new_states.append(jnp.stack([s_f[:bp], s_b[:bp]], axis=1))
        else:
            m = l // 2
            w_qkv = attn_w_qkv[m].astype(BF16)
            q_c, k_c, v_c = _qkv(x, mod, grp_mm, g_pre_mix[l], w_qkv, attn_q_gain[m], attn_k_gain[m],
                                 0, n_ctx // MM_TILE, 0, F32)
            q_l, k_l, v_l = _qkv(x, mod, grp_mm, g_pre_mix[l], w_qkv, attn_q_gain[m], attn_k_gain[m],
                                 n_ctx // MM_TILE, bs * ts // MM_TILE, ts, BF16)
            o_c = _attention(q_c, k_c, v_c, bp, tp)
            past = cache_k.shape[2]
            o_l = _attention(q_l, k_l, v_l, bs, ts,
                             cache=(cache_k[:, m].reshape(bs, past, nk), cache_v[:, m].reshape(bs, past, nk)))
            new_k.append(k_c.reshape(bp, tp, N_KV_HEADS, HEAD_DIM))
            new_v.append(v_c.reshape(bp, tp, N_KV_HEADS, HEAD_DIM))
            x = _attn_out(x, o_c, o_l, mod, grp_mm, attn_w_o[m].astype(BF16), g_post_mix[l])
        x = _moe_layer(x, mod, grp, grp_mm, l, g_pre_ffn[l], g_post_ffn[l], moe_w_router[l], moe_b_router[l],
                       moe_w_gate, moe_b_gate, moe_w_up, moe_b_up, moe_w_down, moe_b_down,
                       split=bp * ctx_tiles if l == depth - 1 else None)

    y_prompt, y_sample = x
    return (y_prompt.reshape(bp, tp, d), y_sample.reshape(bs, ts, d), jnp.stack(new_states, axis=1),
            jnp.stack(new_k, axis=1), jnp.stack(new_v, axis=1))
```

```python
import functools

import numpy as np
import jax
import jax.numpy as jnp
from jax import lax
from jax.experimental import pallas as pl
from jax.experimental.pallas import tpu as pltpu

F32 = jnp.float32
BF16 = jnp.bfloat16
I32 = jnp.int32

EPS = 1e-6
GRID_W = 64
ROPE_THETA = 10000.0
HGRN_DK = 128
HEAD_DIM = 128
N_KV_HEADS = 2
N_EXPERTS = 32
TOP_K = 4
SWIGLU_LIMIT = 7.0
SWIGLU_ALPHA = 1.702

LANES = 128
TM = 256
MM_TILE = 1024
ATT_TQ = 1024
GLA_CHUNK = 64
GLA_GROUP = 4
SEQ_PAR = 2
MOE_BM = 512
ROW_TILE = 8
DMA_UNROLL = 8
ZERO_GROUP = 16
MOD_COLS = 512
VMEM_LIMIT = 56 * 1024 * 1024
NEG_BIG = -1e30

NT_DIMS = (((1,), (1,)), ((), ()))
TN_DIMS = (((0,), (0,)), ((), ()))


def _params(*sem):
    return pltpu.CompilerParams(dimension_semantics=sem, vmem_limit_bytes=VMEM_LIMIT)


def _sigmoid(x):
    return 1.0 / (1.0 + jnp.exp(-x))


def _silu(x):
    return x * _sigmoid(x)


def _rms(x, gain):
    ms = jnp.mean(x * x, axis=-1, keepdims=True)
    return x * lax.rsqrt(ms + EPS) * gain


def _dot(a, b):
    return jnp.dot(a, b, preferred_element_type=F32)


def _mod_kernel(c_ref, w_ref, b_ref, o_ref):
    s = _silu(c_ref[...]).astype(BF16)
    o_ref[...] = _dot(s, w_ref[...].astype(BF16)) + b_ref[...]


def _mod_table(cvec, w_mod, b_mod):
    depth, d, n = w_mod.shape
    rows = cvec.shape[0]
    return pl.pallas_call(
        _mod_kernel,
        grid=(depth, n // MOD_COLS),
        in_specs=[
            pl.BlockSpec((rows, d), lambda l, j: (0, 0)),
            pl.BlockSpec((None, d, MOD_COLS), lambda l, j: (l, 0, j)),
            pl.BlockSpec((None, 1, MOD_COLS), lambda l, j: (l, 0, j)),
        ],
        out_specs=pl.BlockSpec((None, rows, MOD_COLS), lambda l, j: (l, 0, j)),
        out_shape=jax.ShapeDtypeStruct((depth, rows, n), F32),
        compiler_params=_params("arbitrary", "arbitrary"),
        name="mod_table",
    )(cvec, w_mod, b_mod.reshape(depth, 1, n))


def _gla_tiles(q_s, k_s, lf_s, v_s, st_ref, o_s, direction):
    c = GLA_CHUNK
    pair = GLA_GROUP * c
    d_model = q_s.shape[1]
    n_heads = d_model // HGRN_DK
    ri = lax.broadcasted_iota(I32, (pair, pair), 0)
    ci = lax.broadcasted_iota(I32, (pair, pair), 1)
    causal = (ci <= ri) if direction == 0 else (ci >= ri)
    shift = c.bit_length() - 1
    assert c == 1 << shift
    keep = jnp.logical_and(causal, (ri >> shift) == (ci >> shift))
    ones_tri = jnp.where(keep, 1.0, 0.0).astype(BF16)
    pairs = list(range(TM // pair))
    halves = list(range(GLA_GROUP))
    if direction == 1:
        pairs.reverse()
        halves.reverse()
    for pr, seq in [(pr, seq) for pr in pairs for seq in range(SEQ_PAR)]:
        row0 = seq * TM + pr * pair
        rows = pl.ds(row0, pair)
        lf = lf_s[rows, :]
        lf_hi = lf.astype(BF16)
        lf_lo = (lf - lf_hi.astype(F32)).astype(BF16)
        cum2 = _dot(ones_tri, lf_hi) + _dot(ones_tri, lf_lo)
        q_dec, k_inv, q_in, k_end, e_tot = [], [], [], [], []
        for hf in range(GLA_GROUP):
            cum = cum2[hf * c:(hf + 1) * c]
            part = pl.ds(row0 + hf * c, c)
            if direction == 0:
                btot, bmid = cum[c - 1:c], cum[c // 2 - 1:c // 2]
            else:
                btot, bmid = cum[0:1], cum[c // 2:c // 2 + 1]
            qd = q_s[part, :] * jnp.exp(cum - bmid)
            ki = k_s[part, :] * jnp.exp(bmid - cum)
            q_in.append((qd * jnp.exp(bmid)).astype(BF16))
            k_end.append((ki * jnp.exp(btot - bmid)).astype(BF16))
            q_dec.append(qd.astype(BF16))
            k_inv.append(ki.astype(BF16))
            e_tot.append(jnp.exp(btot))
        q_dec = jnp.concatenate(q_dec, axis=0)
        k_inv = jnp.concatenate(k_inv, axis=0)
        vc = v_s[rows, :].astype(BF16)
        for h in range(n_heads):
            sl = slice(h * HGRN_DK, (h + 1) * HGRN_DK)
            a = lax.dot_general(q_dec[:, sl], k_inv[:, sl], NT_DIMS, preferred_element_type=F32)
            a = jnp.where(keep, a, 0.0).astype(BF16)
            o_intra = _dot(a, vc[:, sl])
            for hf in halves:
                part = slice(hf * c, (hf + 1) * c)
                st = st_ref[seq, h]
                o_s[pl.ds(row0 + hf * c, c), sl] = o_intra[part] + lax.dot_general(
                    q_in[hf][:, sl], st.astype(BF16), NT_DIMS, preferred_element_type=F32)
                st_ref[seq, h] = st * e_tot[hf][:, sl] + lax.dot_general(
                    vc[part, sl], k_end[hf][:, sl], TN_DIMS, preferred_element_type=F32)


def _hgrn_gates(h, w_ref, lb, q_s, k_s, lf_s, v_s):
    d = h.shape[1]
    q_s[...] = _silu(_dot(h, w_ref[:, 0:d]))
    f = lb + (1.0 - lb) * _sigmoid(_dot(h, w_ref[:, d:2 * d]))
    lf_s[...] = jnp.log(f)
    k_s[...] = 1.0 - f
    v_s[...] = _dot(h, w_ref[:, 2 * d:3 * d])


def _load_state(s0_ref, st_s):
    @pl.when(pl.program_id(1) == 0)
    def _():
        for s in range(st_s.shape[0]):
            for h in range(st_s.shape[1]):
                st_s[s, h] = s0_ref[s, h].T


def _store_state(sfin_ref, st_s):
    @pl.when(pl.program_id(1) == pl.num_programs(1) - 1)
    def _():
        for s in range(st_s.shape[0]):
            for h in range(st_s.shape[1]):
                sfin_ref[s, h] = st_s[s, h].T


def _hgrn_norm(x_ref, mod_ref, gpre_ref):
    hs = [(_rms(x_ref[s], gpre_ref[...]) * (1.0 + mod_ref[s, 1:2]) + mod_ref[s, 0:1]).astype(BF16)
          for s in range(SEQ_PAR)]
    return jnp.concatenate(hs, axis=0)


def _hgrn_bwd_kernel(x_ref, mod_ref, gpre_ref, w_ref, lb_ref, s0_ref,
                     ob_ref, sfin_ref, st_s, o_s, q_s, k_s, lf_s, v_s):
    _load_state(s0_ref, st_s)
    h = _hgrn_norm(x_ref, mod_ref, gpre_ref)
    _hgrn_gates(h, w_ref, lb_ref[...], q_s, k_s, lf_s, v_s)
    _gla_tiles(q_s, k_s, lf_s, v_s, st_s, o_s, 1)
    for s in range(SEQ_PAR):
        ob_ref[s] = o_s[s * TM:(s + 1) * TM, :].astype(BF16)
    _store_state(sfin_ref, st_s)


def _hgrn_fwd_kernel(x_ref, mod_ref, gpre_ref, w_ref, lb_ref, s0_ref, ob_ref,
                     gnorm_ref, wout_ref, gpost_ref,
                     xo_ref, sfin_ref, st_s, o_s, q_s, k_s, lf_s, v_s):
    _load_state(s0_ref, st_s)
    d = x_ref.shape[2]
    h = _hgrn_norm(x_ref, mod_ref, gpre_ref)
    _hgrn_gates(h, w_ref, lb_ref[...], q_s, k_s, lf_s, v_s)
    _gla_tiles(q_s, k_s, lf_s, v_s, st_s, o_s, 0)
    _store_state(sfin_ref, st_s)
    gate = _silu(_dot(h, w_ref[:, 3 * d:4 * d]))
    gnorm = gnorm_ref[...]
    o_b = jnp.concatenate([ob_ref[s] for s in range(SEQ_PAR)], axis=0).astype(F32)
    for hh in range(d // HGRN_DK):
        sl = slice(hh * HGRN_DK, (hh + 1) * HGRN_DK)
        o_s[:, sl] = _rms(o_s[:, sl] + o_b[:, sl], gnorm) * gate[:, sl]
    out = _dot(o_s[...].astype(BF16), wout_ref[...])
    for s in range(SEQ_PAR):
        xo_ref[s] = x_ref[s] + mod_ref[s, 2:3] * _rms(out[s * TM:(s + 1) * TM], gpost_ref[...])


def _tile_groups(n_ctx_tiles, n_lat_seq, lat_tiles):
    g = [0] * n_ctx_tiles
    for b in range(n_lat_seq):
        g += [1 + b] * lat_tiles
    return jnp.asarray(np.array(g, dtype=np.int32))


def _hgrn_layer(x, mod_seq, g_pre, g_post, w_in, lb, g_norm, w_out, s0):
    n_seq, seq_len, d = x.shape
    n_heads = d // HGRN_DK
    tiles = seq_len // TM
    assert n_seq % SEQ_PAR == 0
    x4 = x.reshape(n_seq, tiles, TM, d)
    w_bwd = jnp.concatenate([w_in[:, 0:d], w_in[:, 2 * d:3 * d], w_in[:, 3 * d:4 * d]], axis=1).astype(BF16)
    w_fwd = jnp.concatenate([w_in[:, 0:d], w_in[:, d:2 * d], w_in[:, 3 * d:5 * d]], axis=1).astype(BF16)

    def specs(direction):
        def tile_spec():
            if direction == 0:
                return pl.BlockSpec((SEQ_PAR, None, TM, d), lambda p, j: (p, j, 0, 0))
            return pl.BlockSpec((SEQ_PAR, None, TM, d), lambda p, j: (p, tiles - 1 - j, 0, 0))
        state = pl.BlockSpec((SEQ_PAR, n_heads, HGRN_DK, HGRN_DK), lambda p, j: (p, 0, 0, 0))
        mods = pl.BlockSpec((SEQ_PAR, 6, d), lambda p, j: (p, 0, 0))
        vec = pl.BlockSpec((1, d), lambda p, j: (0, 0))
        return tile_spec, state, mods, vec

    def const(shape):
        return pl.BlockSpec(shape, lambda p, j: (0, 0))

    rows = SEQ_PAR * TM
    scratch = ([pltpu.VMEM((SEQ_PAR, n_heads, HGRN_DK, HGRN_DK), F32)]
               + [pltpu.VMEM((rows, d), F32)] * 5)
    grid = (n_seq // SEQ_PAR, tiles)
    state_shape = jax.ShapeDtypeStruct((n_seq, n_heads, HGRN_DK, HGRN_DK), F32)

    tile_spec, state, mods, vec = specs(1)
    o_bwd, s_bwd = pl.pallas_call(
        _hgrn_bwd_kernel,
        grid=grid,
        in_specs=[tile_spec(), mods, vec, const((d, 3 * d)), vec, state],
        out_specs=[tile_spec(), state],
        scratch_shapes=scratch,
        out_shape=[jax.ShapeDtypeStruct(x4.shape, BF16), state_shape],
        compiler_params=_params("arbitrary", "arbitrary"),
        name="hgrn_bwd",
    )(x4, mod_seq, g_pre.reshape(1, d), w_bwd, lb[1].reshape(1, d), s0[:, 1])

    tile_spec, state, mods, vec = specs(0)
    x_new, s_fwd = pl.pallas_call(
        _hgrn_fwd_kernel,
        grid=grid,
        in_specs=[tile_spec(), mods, vec, const((d, 4 * d)), vec, state, tile_spec(),
                  const((1, HGRN_DK)), const((d, d)), vec],
        out_specs=[tile_spec(), state],
        scratch_shapes=scratch,
        out_shape=[jax.ShapeDtypeStruct(x4.shape, F32), state_shape],
        compiler_params=_params("arbitrary", "arbitrary"),
        name="hgrn_fwd",
    )(x4, mod_seq, g_pre.reshape(1, d), w_fwd, lb[0].reshape(1, d), s0[:, 0],
      o_bwd, g_norm.reshape(1, HGRN_DK), w_out.astype(BF16), g_post.reshape(1, d))
    return x_new.reshape(n_seq * seq_len, d), s_fwd, s_bwd


def _rope_tables(n_tok):
    axis_dim = HEAD_DIM // 2
    half = axis_dim // 2
    pos = np.arange(n_tok)
    row, col = pos // GRID_W, pos % GRID_W
    inv_freq = ROPE_THETA ** (-np.arange(0, axis_dim, 2, dtype=np.float64) / axis_dim)
    lane = np.arange(HEAD_DIM)
    p = np.where(lane[None, :] < axis_dim, row[:, None], col[:, None]).astype(np.float64)
    ang = p * inv_freq[(lane % axis_dim) % half][None, :]
    sign = np.where((lane % axis_dim) < half, -1.0, 1.0)[None, :]
    return jnp.asarray(np.cos(ang), F32), jnp.asarray(np.sin(ang) * sign, F32)


def _rope(xh, cos, sin_signed):
    axis_dim = HEAD_DIM // 2
    half = axis_dim // 2
    lane = lax.broadcasted_iota(I32, xh.shape, 1)
    upper = pltpu.roll(xh, HEAD_DIM - half, 1)
    lower = pltpu.roll(xh, half, 1)
    partner = jnp.where((lane % axis_dim) < half, upper, lower)
    return xh * cos + partner * sin_signed


def _qkv_kernel(grp, x_ref, mod_ref, gpre_ref, w_ref, qg_ref, kg_ref, *rest, rope):
    del grp
    if rope:
        cos_ref, sin_ref, q_out, k_out, v_out = rest
    else:
        q_out, k_out, v_out = rest
    d = x_ref.shape[1]
    nk = N_KV_HEADS * HEAD_DIM
    h = (_rms(x_ref[...], gpre_ref[...]) * (1.0 + mod_ref[1:2]) + mod_ref[0:1]).astype(BF16)
    q = _dot(h, w_ref[:, 0:d])
    k = _dot(h, w_ref[:, d:d + nk])
    v_out[...] = _dot(h, w_ref[:, d + nk:d + 2 * nk]).astype(v_out.dtype)
    scale = HEAD_DIM ** -0.5
    for hh in range(d // HEAD_DIM):
        sl = slice(hh * HEAD_DIM, (hh + 1) * HEAD_DIM)
        qh = _rms(q[:, sl], qg_ref[...])
        if rope:
            qh = _rope(qh, cos_ref[...], sin_ref[...])
        q_out[:, sl] = (qh * scale).astype(q_out.dtype)
    for hh in range(N_KV_HEADS):
        sl = slice(hh * HEAD_DIM, (hh + 1) * HEAD_DIM)
        kh = _rms(k[:, sl], kg_ref[...])
        if rope:
            kh = _rope(kh, cos_ref[...], sin_ref[...])
        k_out[:, sl] = kh.astype(k_out.dtype)


def _qkv(x, mod, grp, g_pre, w_qkv, q_gain, k_gain, tile0, n_tiles, rope_len, kv_dtype):
    t, d = x.shape
    nk = N_KV_HEADS * HEAD_DIM
    n_out = n_tiles * TM
    tile_in = pl.BlockSpec((TM, d), lambda i, g: (tile0 + i, 0))
    in_specs = [tile_in,
                pl.BlockSpec((None, 6, d), lambda i, g: (g[tile0 + i], 0, 0)),
                pl.BlockSpec((1, d), lambda i, g: (0, 0)),
                pl.BlockSpec((d, d + 2 * nk), lambda i, g: (0, 0)),
                pl.BlockSpec((1, HEAD_DIM), lambda i, g: (0, 0)),
                pl.BlockSpec((1, HEAD_DIM), lambda i, g: (0, 0))]
    args = [grp, x, mod, g_pre.reshape(1, d), w_qkv, q_gain.reshape(1, HEAD_DIM), k_gain.reshape(1, HEAD_DIM)]
    if rope_len:
        per_seq = rope_len // TM
        cos, sin = _rope_tables(rope_len)
        in_specs += [pl.BlockSpec((TM, HEAD_DIM), lambda i, g: (i % per_seq, 0))] * 2
        args += [cos, sin]
    return pl.pallas_call(
        functools.partial(_qkv_kernel, rope=bool(rope_len)),
        grid_spec=pltpu.PrefetchScalarGridSpec(
            num_scalar_prefetch=1, grid=(n_tiles,), in_specs=in_specs,
            out_specs=[pl.BlockSpec((TM, d), lambda i, g: (i, 0)),
                       pl.BlockSpec((TM, nk), lambda i, g: (i, 0)),
                       pl.BlockSpec((TM, nk), lambda i, g: (i, 0))]),
        out_shape=[jax.ShapeDtypeStruct((n_out, d), BF16),
                   jax.ShapeDtypeStruct((n_out, nk), kv_dtype),
                   jax.ShapeDtypeStruct((n_out, nk), kv_dtype)],
        compiler_params=_params("arbitrary"),
        name="qkv_rope" if rope_len else "qkv",
    )(*args)


def _attn_kernel(q_ref, *rest, n_pieces):
    kv, o_ref = rest[:-1], rest[-1]
    ks = [kv[2 * p][...].astype(BF16) for p in range(n_pieces)]
    vs = [kv[2 * p + 1][...].astype(BF16) for p in range(n_pieces)]
    group = q_ref.shape[1] // HEAD_DIM
    for g in range(group):
        sl = slice(g * HEAD_DIM, (g + 1) * HEAD_DIM)
        qh = q_ref[:, sl]
        ss = [lax.dot_general(qh, kp, NT_DIMS, preferred_element_type=F32) for kp in ks]
        m = ss[0].max(axis=-1, keepdims=True)
        for s in ss[1:]:
            m = jnp.maximum(m, s.max(axis=-1, keepdims=True))
        ps = [jnp.exp(s - m) for s in ss]
        denom = ps[0].sum(axis=-1, keepdims=True)
        for p in ps[1:]:
            denom = denom + p.sum(axis=-1, keepdims=True)
        o = _dot(ps[0].astype(BF16), vs[0])
        for p, vp in zip(ps[1:], vs[1:]):
            o = o + _dot(p.astype(BF16), vp)
        o_ref[:, sl] = (o / denom).astype(o_ref.dtype)


def _attention(q, k_new, v_new, n_seq, seq_len, cache=None):
    t, d = q.shape
    group_w = d // N_KV_HEADS
    tq = min(ATT_TQ, seq_len)
    nq = seq_len // tq
    q_spec = pl.BlockSpec((tq, group_w), lambda b, i, kv: (b * nq + i, kv))
    new_spec = pl.BlockSpec((seq_len, HEAD_DIM), lambda b, i, kv: (b, kv))
    in_specs, args = [q_spec], [q]
    if cache is not None:
        past = cache[0].shape[1]
        c_spec = pl.BlockSpec((None, past, HEAD_DIM), lambda b, i, kv: (b, 0, kv))
        in_specs += [c_spec, c_spec]
        args += list(cache)
    in_specs += [new_spec, new_spec]
    args += [k_new, v_new]
    return pl.pallas_call(
        functools.partial(_attn_kernel, n_pieces=len(args) // 2),
        grid=(n_seq, nq, N_KV_HEADS),
        in_specs=in_specs,
        out_specs=q_spec,
        out_shape=jax.ShapeDtypeStruct((t, d), BF16),
        compiler_params=_params("arbitrary", "arbitrary", "arbitrary"),
        name="attention",
    )(*args)


def _attn_out_kernel(grp, x_ref, oc_ref, ol_ref, mod_ref, w_ref, gpost_ref, xo_ref, *, n_ctx_tiles):
    del grp
    o = jnp.where(pl.program_id(0) < n_ctx_tiles, oc_ref[...], ol_ref[...])
    out = _dot(o, w_ref[...])
    xo_ref[...] = x_ref[...] + mod_ref[2:3] * _rms(out, gpost_ref[...])


def _attn_out(x, o_ctx, o_lat, mod, grp, w_o, g_post):
    t, d = x.shape
    nct = o_ctx.shape[0] // MM_TILE
    tile = pl.BlockSpec((MM_TILE, d), lambda i, g: (i, 0))
    return pl.pallas_call(
        functools.partial(_attn_out_kernel, n_ctx_tiles=nct),
        grid_spec=pltpu.PrefetchScalarGridSpec(
            num_scalar_prefetch=1, grid=(t // MM_TILE,),
            in_specs=[tile,
                      pl.BlockSpec((MM_TILE, d), lambda i, g: (jnp.minimum(i, nct - 1), 0)),
                      pl.BlockSpec((MM_TILE, d), lambda i, g: (jnp.maximum(i - nct, 0), 0)),
                      pl.BlockSpec((None, 6, d), lambda i, g: (g[i], 0, 0)),
                      pl.BlockSpec((d, d), lambda i, g: (0, 0)),
                      pl.BlockSpec((1, d), lambda i, g: (0, 0))],
            out_specs=tile),
        out_shape=jax.ShapeDtypeStruct((t, d), F32),
        input_output_aliases={1: 0},
        compiler_params=_params("arbitrary"),
        name="attn_out",
    )(grp, x, o_ctx, o_lat, mod, w_o, g_post.reshape(1, d))


def _to_token_tiles(ref, x):
    rows = x.shape[0]
    for s in range(x.shape[1] // LANES):
        ref[pl.ds(s, rows, stride=ROW_TILE), :] = x[:, s * LANES:(s + 1) * LANES]


def _from_token_tiles(ref, rows):
    n = ROW_TILE
    return jnp.concatenate([ref[pl.ds(s, rows, stride=n), :] for s in range(n)], axis=1)


def _router_kernel(grp, x_ref, mod_ref, g_ref, wr_ref, br_ref,
                   h_ref, route_ref, gate_ref, cnt_ref, carry):
    del grp
    @pl.when(pl.program_id(0) == 0)
    def _():
        carry[...] = jnp.zeros_like(carry)
    h = _rms(x_ref[...], g_ref[...]) * (1.0 + mod_ref[4:5]) + mod_ref[3:4]
    _to_token_tiles(h_ref, h)
    logits = _dot(h.astype(BF16), wr_ref[...]) + br_ref[...]
    lane = lax.broadcasted_iota(I32, logits.shape, 1)
    lane_f = lane.astype(F32)
    vals, idxs = [], []
    work = logits
    for _ in range(TOP_K):
        m = work.max(axis=-1, keepdims=True)
        idx = jnp.where(work == m, lane_f, float(LANES)).min(axis=-1, keepdims=True).astype(I32)
        vals.append(m)
        idxs.append(idx)
        work = jnp.where(lane == idx, -jnp.inf, work)
    es = [jnp.exp(v - vals[0]) for v in vals]
    denom = es[0]
    for e in es[1:]:
        denom = denom + e
    onehot = jnp.zeros(logits.shape, F32)
    for idx in idxs:
        onehot = onehot + jnp.where(lane == idx, 1.0, 0.0)
    ri = lax.broadcasted_iota(I32, (MM_TILE, MM_TILE), 0)
    ci = lax.broadcasted_iota(I32, (MM_TILE, MM_TILE), 1)
    before = jnp.where(ci < ri, 1.0, 0.0).astype(BF16)
    rank_all = _dot(before, onehot.astype(BF16)) + carry[0:1]
    route = jnp.zeros(logits.shape, I32)
    gate = jnp.zeros(logits.shape, F32)
    for kk in range(TOP_K):
        rank = jnp.where(lane == idxs[kk], rank_all, 0.0).sum(axis=-1, keepdims=True)
        route = jnp.where(lane == kk, idxs[kk], route)
        route = jnp.where(lane == TOP_K + kk, rank.astype(I32), route)
        gate = jnp.where(lane == kk, es[kk] / denom, gate)
    route_ref[...] = route
    gate_ref[...] = gate
    carry[...] = carry[...] + onehot.sum(axis=0, keepdims=True)
    cnt_ref[...] = carry[...]


def _router(x, mod, grp, g_pre, w_router, b_router):
    t, d = x.shape
    n_exp = w_router.shape[1]
    wr = jnp.pad(w_router.astype(BF16), ((0, 0), (0, LANES - n_exp)))
    br = jnp.pad(b_router.reshape(1, n_exp), ((0, 0), (0, LANES - n_exp)), constant_values=NEG_BIG)
    tile = pl.BlockSpec((MM_TILE, d), lambda i, g: (i, 0))
    lane_tile = pl.BlockSpec((MM_TILE, LANES), lambda i, g: (i, 0))
    return pl.pallas_call(
        _router_kernel,
        grid_spec=pltpu.PrefetchScalarGridSpec(
            num_scalar_prefetch=1, grid=(t // MM_TILE,),
            in_specs=[tile,
                      pl.BlockSpec((None, 6, d), lambda i, g: (g[i], 0, 0)),
                      pl.BlockSpec((1, d), lambda i, g: (0, 0)),
                      pl.BlockSpec((d, LANES), lambda i, g: (0, 0)),
                      pl.BlockSpec((1, LANES), lambda i, g: (0, 0))],
            out_specs=[pl.BlockSpec((MM_TILE * ROW_TILE, LANES), lambda i, g: (i, 0)), lane_tile, lane_tile,
                       pl.BlockSpec((8, LANES), lambda i, g: (0, 0))],
            scratch_shapes=[pltpu.VMEM((8, LANES), F32)]),
        out_shape=[jax.ShapeDtypeStruct((t * ROW_TILE, LANES), F32),
                   jax.ShapeDtypeStruct((t, LANES), I32),
                   jax.ShapeDtypeStruct((t, LANES), F32),
                   jax.ShapeDtypeStruct((8, LANES), F32)],
        compiler_params=_params("arbitrary"),
        name="router",
    )(grp, x, mod, g_pre.reshape(1, d), wr, br)


def _tile_copy(src, src_row, dst, dst_row, sem):
    n = ROW_TILE
    return pltpu.make_async_copy(src.at[pl.ds(pl.multiple_of(src_row * n, n), n)],
                                 dst.at[pl.ds(pl.multiple_of(dst_row * n, n), n)], sem)


def _dispatch_kernel(pad_lo, pad_hi, dest_ref, h_ref, xs_hbm, zero_s, sem, zsem):
    def issue(g, c):
        for j in range(DMA_UNROLL):
            t = g * DMA_UNROLL + j
            for kk in range(TOP_K):
                _tile_copy(h_ref, t, xs_hbm, dest_ref[t * TOP_K + kk], sem).start(priority=kk % 2)
        return c
    lax.fori_loop(0, TM // DMA_UNROLL, issue, 0)

    @pl.when(pl.program_id(0) == 0)
    def _():
        zero_s[...] = jnp.zeros_like(zero_s)
        group_rows = ZERO_GROUP * ROW_TILE

        def single(s):
            return _tile_copy(zero_s, 0, xs_hbm, s, zsem)

        def group(g):
            return pltpu.make_async_copy(
                zero_s, xs_hbm.at[pl.ds(pl.multiple_of(g * group_rows, group_rows), group_rows)], zsem)

        def per_expert(e, c):
            lo, hi = pad_lo[e], pad_hi[e]
            mid = jnp.minimum(hi, (lo + ZERO_GROUP - 1) // ZERO_GROUP * ZERO_GROUP)
            g_lo, g_hi = mid // ZERO_GROUP, hi // ZERO_GROUP
            lax.fori_loop(lo, mid, lambda s, c2: (single(s).start(), c2)[1], 0)
            lax.fori_loop(g_lo, g_hi, lambda g, c2: (group(g).start(), c2)[1], 0)
            lax.fori_loop(lo, mid, lambda s, c2: (single(s).wait(), c2)[1], 0)
            lax.fori_loop(g_lo, g_hi, lambda g, c2: (group(g).wait(), c2)[1], 0)
            return c
        lax.fori_loop(0, pad_lo.shape[0], per_expert, 0)

    for _ in range(TOP_K):
        pltpu.make_async_copy(h_ref, xs_hbm.at[pl.ds(0, TM * ROW_TILE)], sem).wait()


def _dispatch(h3, dest, pad_lo, pad_hi, n_slots):
    t = h3.shape[0] // ROW_TILE
    return pl.pallas_call(
        _dispatch_kernel,
        grid_spec=pltpu.PrefetchScalarGridSpec(
            num_scalar_prefetch=2, grid=(t // TM,),
            in_specs=[pl.BlockSpec((TM * TOP_K,), lambda i, lo, hi: (i,), memory_space=pltpu.SMEM),
                      pl.BlockSpec((TM * ROW_TILE, LANES), lambda i, lo, hi: (i, 0))],
            out_specs=pl.BlockSpec(memory_space=pl.ANY),
            scratch_shapes=[pltpu.VMEM((ZERO_GROUP * ROW_TILE, LANES), F32),
                            pltpu.SemaphoreType.DMA, pltpu.SemaphoreType.DMA]),
        out_shape=jax.ShapeDtypeStruct((n_slots * ROW_TILE, LANES), F32),
        compiler_params=_params("arbitrary"),
        name="moe_dispatch",
    )(pad_lo, pad_hi, dest, h3)


def _expert_kernel(blk0, nblk, xs_hbm, wg_ref, bg_ref, wu_ref, bu_ref, wd_ref, bd_ref,
                   y_hbm, wg_s, wu_s, wd_s, xbuf, ybuf, in_sem, out_sem):
    e = pl.program_id(0)
    first = blk0[e]
    n = nblk[e]
    rows = MOE_BM * ROW_TILE

    def in_copy(b, slot):
        return pltpu.make_async_copy(
            xs_hbm.at[pl.ds(pl.multiple_of((first + b) * rows, rows), rows)], xbuf.at[slot], in_sem.at[slot])

    def out_copy(b, slot):
        return pltpu.make_async_copy(
            ybuf.at[slot], y_hbm.at[pl.ds(pl.multiple_of((first + b) * rows, rows), rows)], out_sem.at[slot])

    @pl.when(n > 0)
    def _():
        in_copy(0, 0).start()
        wg_s[...] = wg_ref[...].astype(BF16)
        wu_s[...] = wu_ref[...].astype(BF16)
        wd_s[...] = wd_ref[...].astype(BF16)

    def block(b, c):
        slot = b % 2
        in_copy(b, slot).wait()

        @pl.when(b + 1 < n)
        def _():
            in_copy(b + 1, 1 - slot).start()

        @pl.when(b >= 2)
        def _():
            out_copy(b - 2, slot).wait()

        x = _from_token_tiles(xbuf.at[slot], MOE_BM).astype(BF16)
        hg = jnp.minimum(_dot(x, wg_s[...]) + bg_ref[...], SWIGLU_LIMIT)
        hu = jnp.clip(_dot(x, wu_s[...]) + bu_ref[...], -SWIGLU_LIMIT, SWIGLU_LIMIT)
        hh = (hu + 1.0) * hg * _sigmoid(SWIGLU_ALPHA * hg)
        _to_token_tiles(ybuf.at[slot], _dot(hh.astype(BF16), wd_s[...]) + bd_ref[...])
        out_copy(b, slot).start()
        return c
    lax.fori_loop(0, n, block, 0)

    @pl.when(n >= 2)
    def _():
        out_copy(n - 2, n % 2).wait()

    @pl.when(n >= 1)
    def _():
        out_copy(n - 1, (n - 1) % 2).wait()


def _experts(xs, blk0, nblk, layer, w_gate, b_gate, w_up, b_up, w_down, b_down):
    depth, n_exp, d, f = w_gate.shape

    def w_spec(k, n):
        return pl.BlockSpec((None, None, k, n), lambda e, b0, nb: (layer, e, 0, 0))

    rows = MOE_BM * ROW_TILE
    return pl.pallas_call(
        _expert_kernel,
        grid_spec=pltpu.PrefetchScalarGridSpec(
            num_scalar_prefetch=2, grid=(n_exp,),
            in_specs=[pl.BlockSpec(memory_space=pl.ANY),
                      w_spec(d, f), w_spec(1, f), w_spec(d, f), w_spec(1, f), w_spec(f, d), w_spec(1, d)],
            out_specs=pl.BlockSpec(memory_space=pl.ANY),
            scratch_shapes=[pltpu.VMEM((d, f), BF16), pltpu.VMEM((d, f), BF16), pltpu.VMEM((f, d), BF16),
                            pltpu.VMEM((2, rows, LANES), F32), pltpu.VMEM((2, rows, LANES), F32),
                            pltpu.SemaphoreType.DMA((2,)), pltpu.SemaphoreType.DMA((2,))]),
        out_shape=jax.ShapeDtypeStruct(xs.shape, F32),
        compiler_params=_params("arbitrary"),
        name="moe_experts",
    )(blk0, nblk, xs, w_gate, b_gate.reshape(depth, n_exp, 1, f), w_up, b_up.reshape(depth, n_exp, 1, f),
      w_down, b_down.reshape(depth, n_exp, 1, d))


def _combine_kernel(grp, dest_ref, dest_next_ref, x_ref, gate_ref, mod_ref, gpost_ref, y_hbm,
                    *rest, split):
    del grp
    outs, (ybuf, sem, acc_s) = rest[:-3], rest[-3:]
    i = pl.program_id(0)
    n = pl.num_programs(0)
    rows = TM * ROW_TILE

    def issue(dref, slot):
        def body(g, c):
            for j in range(DMA_UNROLL):
                t = g * DMA_UNROLL + j
                for kk in range(TOP_K):
                    _tile_copy(y_hbm, dref[t * TOP_K + kk], ybuf.at[slot, kk], t, sem.at[slot]).start(
                        priority=kk % 2)
            return c
        lax.fori_loop(0, TM // DMA_UNROLL, body, 0)

    @pl.when(i == 0)
    def _():
        issue(dest_ref, 0)

    @pl.when(i + 1 < n)
    def _():
        issue(dest_next_ref, (i + 1) % 2)

    slot = i % 2

    for kk in range(TOP_K):
        pltpu.make_async_copy(y_hbm.at[pl.ds(0, rows)], ybuf.at[slot, kk], sem.at[slot]).wait()

    gates = gate_ref[...]
    for s in range(ROW_TILE):
        sl = slice(s * LANES, (s + 1) * LANES)
        acc = gates[:, 0:1] * ybuf[slot, 0, pl.ds(s, TM, stride=ROW_TILE), :]
        for kk in range(1, TOP_K):
            acc = acc + gates[:, kk:kk + 1] * ybuf[slot, kk, pl.ds(s, TM, stride=ROW_TILE), :]
        acc_s[:, sl] = acc
    result = x_ref[...] + mod_ref[5:6] * _rms(acc_s[...], gpost_ref[...])
    if split is None:
        outs[0][...] = result
    else:
        @pl.when(i < split)
        def _():
            outs[0][...] = result

        @pl.when(i >= split)
        def _():
            outs[1][...] = result


def _combine(x, y, dest, gates, mod, grp, g_post, split):
    t, d = x.shape
    n_tiles = t // TM
    tile = pl.BlockSpec((TM, d), lambda i, g: (i, 0))
    if split is None:
        out_specs, out_shape, aliases = tile, jax.ShapeDtypeStruct((t, d), F32), {3: 0}
    else:
        out_specs = [pl.BlockSpec((TM, d), lambda i, g: (jnp.minimum(i, split - 1), 0)),
                     pl.BlockSpec((TM, d), lambda i, g: (jnp.maximum(i - split, 0), 0))]
        out_shape = [jax.ShapeDtypeStruct((split * TM, d), F32),
                     jax.ShapeDtypeStruct((t - split * TM, d), F32)]
        aliases = {}
    return pl.pallas_call(
        functools.partial(_combine_kernel, split=split),
        grid_spec=pltpu.PrefetchScalarGridSpec(
            num_scalar_prefetch=1, grid=(n_tiles,),
            in_specs=[pl.BlockSpec((TM * TOP_K,), lambda i, g: (i,), memory_space=pltpu.SMEM),
                      pl.BlockSpec((TM * TOP_K,), lambda i, g: (jnp.minimum(i + 1, n_tiles - 1),),
                                   memory_space=pltpu.SMEM),
                      tile,
                      pl.BlockSpec((TM, LANES), lambda i, g: (i, 0)),
                      pl.BlockSpec((None, 6, d), lambda i, g: (g[i], 0, 0)),
                      pl.BlockSpec((1, d), lambda i, g: (0, 0)),
                      pl.BlockSpec(memory_space=pl.ANY)],
            out_specs=out_specs,
            scratch_shapes=[pltpu.VMEM((2, TOP_K, TM * ROW_TILE, LANES), F32),
                            pltpu.SemaphoreType.DMA((2,)), pltpu.VMEM((TM, d), F32)]),
        out_shape=out_shape,
        input_output_aliases=aliases,
        compiler_params=_params("arbitrary"),
        name="moe_combine",
    )(grp, dest, dest, x, gates, mod, g_post.reshape(1, d), y)


def _moe_layer(x, mod, grp, grp_mm, layer, g_pre, g_post, w_router, b_router,
               w_gate, b_gate, w_up, b_up, w_down, b_down, split=None):
    t, d = x.shape
    n_exp = w_router.shape[1]
    assert d == ROW_TILE * LANES
    h3, route, gates, cnt = _router(x, mod, grp_mm, g_pre, w_router, b_router)
    idx = route[:, 0:TOP_K]
    rank = route[:, TOP_K:2 * TOP_K]
    counts = cnt[0, :n_exp].astype(I32)
    padded = (counts + MOE_BM - 1) // MOE_BM * MOE_BM
    pad_end = jnp.cumsum(padded)
    pad_start = pad_end - padded
    experts = jnp.arange(n_exp, dtype=I32)
    dest = (rank + jnp.sum(jnp.where(idx[..., None] == experts, pad_start, 0), axis=-1)).reshape(-1)
    n_blocks = t * TOP_K // MOE_BM + n_exp
    xs = _dispatch(h3, dest, pad_start + counts, pad_end, n_blocks * MOE_BM)
    y = _experts(xs, pad_start // MOE_BM, padded // MOE_BM, layer,
                 w_gate, b_gate, w_up, b_up, w_down, b_down)
    return _combine(x, y, dest, gates, mod, grp, g_post, split)


def kernel(x_prompt, x_sample, state_hgrn, cache_k, cache_v, c, c_ctx, w_mod, b_mod, g_pre_mix, g_post_mix, g_pre_ffn, g_post_ffn, hgrn_w_in, hgrn_gamma, hgrn_g_norm, hgrn_w_out, attn_w_qkv, attn_q_gain, attn_k_gain, attn_w_o, moe_w_router, moe_b_router, moe_w_gate, moe_b_gate, moe_w_up, moe_b_up, moe_w_down, moe_b_down):
    bp, tp, d = x_prompt.shape
    bs, ts, _ = x_sample.shape
    depth = w_mod.shape[0]
    n_heads = d // HGRN_DK
    nk = N_KV_HEADS * HEAD_DIM
    assert tp % TM == 0 and ts % MM_TILE == 0 and ts % GRID_W == 0 and d % LANES == 0
    n_ctx = bp * tp
    assert n_ctx % MM_TILE == 0
    ctx_tiles, lat_tiles = tp // TM, ts // TM
    grp = _tile_groups(bp * ctx_tiles, bs, lat_tiles)
    grp_mm = _tile_groups(n_ctx // MM_TILE, bs, ts // MM_TILE)

    rows = -(-(1 + bs) // 8) * 8
    cvec = jnp.zeros((rows, d), F32).at[0].set(c_ctx).at[1:1 + bs].set(c)
    mod_all = _mod_table(cvec, w_mod, b_mod).reshape(depth, rows, 6, d)

    lb_all = jnp.cumsum(jax.nn.softmax(hgrn_gamma.astype(F32), axis=0), axis=0)

    new_states, new_k, new_v = [], [], []
    x = None
    x_parts = (x_prompt, x_sample)
    for l in range(depth):
        mod = mod_all[l]
        if l % 2 == 0:
            a = l // 2
            if x is not None:
                x_parts = (x[:n_ctx].reshape(bp, tp, d), x[n_ctx:].reshape(bs, ts, d))
            hgrn = functools.partial(_hgrn_layer, g_pre=g_pre_mix[l], g_post=g_post_mix[l], w_in=hgrn_w_in[a],
                                     lb=lb_all[l], g_norm=hgrn_g_norm[a], w_out=hgrn_w_out[a])
            x_c, s_f, s_b = hgrn(x_parts[0], jnp.broadcast_to(mod[0], (bp, 6, d)),
                                 s0=jnp.zeros((bp, 2, n_heads, HGRN_DK, HGRN_DK), F32))
            x_l, _, _ = hgrn(x_parts[1], mod[1:1 + bs], s0=state_hgrn[:, a].astype(F32))
            x = jnp.concatenate([x_c, x_l], axis=0)
            new_states.append(jnp.stack([s_f, s_b], axis=1))
        else:
            m = l // 2
            w_qkv = attn_w_qkv[m].astype(BF16)
            q_c, k_c, v_c = _qkv(x, mod, grp, g_pre_mix[l], w_qkv, attn_q_gain[m], attn_k_gain[m],
                                 0, n_ctx // TM, 0, F32)
            q_l, k_l, v_l = _qkv(x, mod, grp, g_pre_mix[l], w_qkv, attn_q_gain[m], attn_k_gain[m],
                                 n_ctx // TM, bs * ts // TM, ts, BF16)
            o_c = _attention(q_c, k_c, v_c, bp, tp)
            past = cache_k.shape[2]
            o_l = _attention(q_l, k_l, v_l, bs, ts,
                             cache=(cache_k[:, m].reshape(bs, past, nk), cache_v[:, m].reshape(bs, past, nk)))
            new_k.append(k_c.reshape(bp, tp, N_KV_HEADS, HEAD_DIM))
            new_v.append(v_c.reshape(bp, tp, N_KV_HEADS, HEAD_DIM))
            x = _attn_out(x, o_c, o_l, mod, grp_mm, attn_w_o[m].astype(BF16), g_post_mix[l])
        x = _moe_layer(x, mod, grp, grp_mm, l, g_pre_ffn[l], g_post_ffn[l], moe_w_router[l], moe_b_router[l],
                       moe_w_gate, moe_b_gate, moe_w_up, moe_b_up, moe_w_down, moe_b_down,
                       split=bp * ctx_tiles if l == depth - 1 else None)

    y_prompt, y_sample = x
    return (y_prompt.reshape(bp, tp, d), y_sample.reshape(bs, ts, d), jnp.stack(new_states, axis=1),
            jnp.stack(new_k, axis=1), jnp.stack(new_v, axis=1))
```

```python
import functools

import numpy as np
import jax
import jax.numpy as jnp
from jax import lax
from jax.experimental import pallas as pl
from jax.experimental.pallas import tpu as pltpu

F32 = jnp.float32
BF16 = jnp.bfloat16
I32 = jnp.int32

EPS = 1e-6
GRID_W = 64
ROPE_THETA = 10000.0
HGRN_DK = 128
HEAD_DIM = 128
N_KV_HEADS = 2
N_EXPERTS = 32
TOP_K = 4
SWIGLU_LIMIT = 7.0
SWIGLU_ALPHA = 1.702

LANES = 128
TM = 256
MM_TILE = 1024
ATT_TQ = 1024
GLA_CHUNK = 64
GLA_GROUP = 4
SEQ_PAR = 2
MOE_BM = 512
ROW_TILE = 8
DMA_UNROLL = 8
ZERO_GROUP = 16
MOD_COLS = 512
VMEM_LIMIT = 56 * 1024 * 1024
NEG_BIG = -1e30

NT_DIMS = (((1,), (1,)), ((), ()))
TN_DIMS = (((0,), (0,)), ((), ()))


def _params(*sem):
    return pltpu.CompilerParams(dimension_semantics=sem, vmem_limit_bytes=VMEM_LIMIT)


def _sigmoid(x):
    return 1.0 / (1.0 + jnp.exp(-x))


def _silu(x):
    return x * _sigmoid(x)


def _rms(x, gain):
    ms = jnp.mean(x * x, axis=-1, keepdims=True)
    return x * lax.rsqrt(ms + EPS) * gain


def _dot(a, b):
    return jnp.dot(a, b, preferred_element_type=F32)


def _mod_kernel(c_ref, w_ref, b_ref, o_ref):
    s = _silu(c_ref[...]).astype(BF16)
    o_ref[...] = _dot(s, w_ref[...].astype(BF16)) + b_ref[...]


def _mod_table(cvec, w_mod, b_mod):
    depth, d, n = w_mod.shape
    rows = cvec.shape[0]
    return pl.pallas_call(
        _mod_kernel,
        grid=(depth, n // MOD_COLS),
        in_specs=[
            pl.BlockSpec((rows, d), lambda l, j: (0, 0)),
            pl.BlockSpec((None, d, MOD_COLS), lambda l, j: (l, 0, j)),
            pl.BlockSpec((None, 1, MOD_COLS), lambda l, j: (l, 0, j)),
        ],
        out_specs=pl.BlockSpec((None, rows, MOD_COLS), lambda l, j: (l, 0, j)),
        out_shape=jax.ShapeDtypeStruct((depth, rows, n), F32),
        compiler_params=_params("arbitrary", "arbitrary"),
        name="mod_table",
    )(cvec, w_mod, b_mod.reshape(depth, 1, n))


def _gla_tiles(q_s, k_s, lf_s, v_s, st_ref, o_s, direction):
    c = GLA_CHUNK
    pair = GLA_GROUP * c
    d_model = q_s.shape[1]
    n_heads = d_model // HGRN_DK
    ri = lax.broadcasted_iota(I32, (pair, pair), 0)
    ci = lax.broadcasted_iota(I32, (pair, pair), 1)
    causal = (ci <= ri) if direction == 0 else (ci >= ri)
    shift = c.bit_length() - 1
    assert c == 1 << shift
    keep = jnp.logical_and(causal, (ri >> shift) == (ci >> shift))
    ones_tri = jnp.where(keep, 1.0, 0.0).astype(BF16)
    pairs = list(range(TM // pair))
    halves = list(range(GLA_GROUP))
    if direction == 1:
        pairs.reverse()
        halves.reverse()
    for pr, seq in [(pr, seq) for pr in pairs for seq in range(SEQ_PAR)]:
        row0 = seq * TM + pr * pair
        rows = pl.ds(row0, pair)
        lf = lf_s[rows, :]
        lf_hi = lf.astype(BF16)
        lf_lo = (lf - lf_hi.astype(F32)).astype(BF16)
        cum2 = _dot(ones_tri, lf_hi) + _dot(ones_tri, lf_lo)
        q_dec, k_inv, q_in, k_end, e_tot = [], [], [], [], []
        for hf in range(GLA_GROUP):
            cum = cum2[hf * c:(hf + 1) * c]
            part = pl.ds(row0 + hf * c, c)
            if direction == 0:
                btot, bmid = cum[c - 1:c], cum[c // 2 - 1:c // 2]
            else:
                btot, bmid = cum[0:1], cum[c // 2:c // 2 + 1]
            qd = q_s[part, :] * jnp.exp(cum - bmid)
            ki = k_s[part, :] * jnp.exp(bmid - cum)
            q_in.append((qd * jnp.exp(bmid)).astype(BF16))
            k_end.append((ki * jnp.exp(btot - bmid)).astype(BF16))
            q_dec.append(qd.astype(BF16))
            k_inv.append(ki.astype(BF16))
            e_tot.append(jnp.exp(btot))
        q_dec = jnp.concatenate(q_dec, axis=0)
        k_inv = jnp.concatenate(k_inv, axis=0)
        vc = v_s[rows, :].astype(BF16)
        for h in range(n_heads):
            sl = slice(h * HGRN_DK, (h + 1) * HGRN_DK)
            a = lax.dot_general(q_dec[:, sl], k_inv[:, sl], NT_DIMS, preferred_element_type=F32)
            a = jnp.where(keep, a, 0.0).astype(BF16)
            o_intra = _dot(a, vc[:, sl])
            for hf in halves:
                part = slice(hf * c, (hf + 1) * c)
                st = st_ref[seq, h]
                o_s[pl.ds(row0 + hf * c, c), sl] = o_intra[part] + lax.dot_general(
                    q_in[hf][:, sl], st.astype(BF16), NT_DIMS, preferred_element_type=F32)
                st_ref[seq, h] = st * e_tot[hf][:, sl] + lax.dot_general(
                    vc[part, sl], k_end[hf][:, sl], TN_DIMS, preferred_element_type=F32)


def _hgrn_gates(h, w_ref, lb, q_s, k_s, lf_s, v_s):
    d = h.shape[1]
    q_s[...] = _silu(_dot(h, w_ref[:, 0:d]))
    f = lb + (1.0 - lb) * _sigmoid(_dot(h, w_ref[:, d:2 * d]))
    lf_s[...] = jnp.log(f)
    k_s[...] = 1.0 - f
    v_s[...] = _dot(h, w_ref[:, 2 * d:3 * d])


def _load_state(s0_ref, st_s):
    @pl.when(pl.program_id(1) == 0)
    def _():
        for s in range(st_s.shape[0]):
            for h in range(st_s.shape[1]):
                st_s[s, h] = s0_ref[s, h].T


def _store_state(sfin_ref, st_s):
    @pl.when(pl.program_id(1) == pl.num_programs(1) - 1)
    def _():
        for s in range(st_s.shape[0]):
            for h in range(st_s.shape[1]):
                sfin_ref[s, h] = st_s[s, h].T


def _hgrn_norm(x_ref, mod_ref, gpre_ref):
    hs = [(_rms(x_ref[s], gpre_ref[...]) * (1.0 + mod_ref[s, 1:2]) + mod_ref[s, 0:1]).astype(BF16)
          for s in range(SEQ_PAR)]
    return jnp.concatenate(hs, axis=0)


def _hgrn_bwd_kernel(x_ref, mod_ref, gpre_ref, w_ref, lb_ref, s0_ref,
                     ob_ref, sfin_ref, st_s, o_s, q_s, k_s, lf_s, v_s):
    _load_state(s0_ref, st_s)
    h = _hgrn_norm(x_ref, mod_ref, gpre_ref)
    _hgrn_gates(h, w_ref, lb_ref[...], q_s, k_s, lf_s, v_s)
    _gla_tiles(q_s, k_s, lf_s, v_s, st_s, o_s, 1)
    for s in range(SEQ_PAR):
        ob_ref[s] = o_s[s * TM:(s + 1) * TM, :].astype(BF16)
    _store_state(sfin_ref, st_s)


def _hgrn_fwd_kernel(x_ref, mod_ref, gpre_ref, w_ref, lb_ref, s0_ref, ob_ref,
                     gnorm_ref, wout_ref, gpost_ref, *rest):
    xo_ref, sfin_ref, st_s, o_s, q_s, k_s, lf_s, v_s = rest[-8:]
    _load_state(s0_ref, st_s)
    d = x_ref.shape[2]
    h = _hgrn_norm(x_ref, mod_ref, gpre_ref)
    _hgrn_gates(h, w_ref, lb_ref[...], q_s, k_s, lf_s, v_s)
    _gla_tiles(q_s, k_s, lf_s, v_s, st_s, o_s, 0)
    _store_state(sfin_ref, st_s)
    gate = _silu(_dot(h, w_ref[:, 3 * d:4 * d]))
    gnorm = gnorm_ref[...]
    o_b = jnp.concatenate([ob_ref[s] for s in range(SEQ_PAR)], axis=0).astype(F32)
    for hh in range(d // HGRN_DK):
        sl = slice(hh * HGRN_DK, (hh + 1) * HGRN_DK)
        o_s[:, sl] = _rms(o_s[:, sl] + o_b[:, sl], gnorm) * gate[:, sl]
    out = _dot(o_s[...].astype(BF16), wout_ref[...])
    for s in range(SEQ_PAR):
        xo_ref[s] = x_ref[s] + mod_ref[s, 2:3] * _rms(out[s * TM:(s + 1) * TM], gpost_ref[...])


def _tile_groups(n_ctx_tiles, n_lat_seq, lat_tiles):
    g = [0] * n_ctx_tiles
    for b in range(n_lat_seq):
        g += [1 + b] * lat_tiles
    return jnp.asarray(np.array(g, dtype=np.int32))


def _hgrn_layer(x, mod_seq, g_pre, g_post, w_in, lb, g_norm, w_out, s0, stream, group0):
    n_seq, seq_len, d = x.shape
    n_heads = d // HGRN_DK
    tiles = seq_len // TM
    per_group = stream.shape[1]
    assert n_seq % SEQ_PAR == 0 and per_group % SEQ_PAR == 0 and tiles in (1, per_group)
    x4 = x.reshape(n_seq, tiles, TM, d)
    w_bwd = jnp.concatenate([w_in[:, 0:d], w_in[:, 2 * d:3 * d], w_in[:, 3 * d:4 * d]], axis=1).astype(BF16)
    w_fwd = jnp.concatenate([w_in[:, 0:d], w_in[:, d:2 * d], w_in[:, 3 * d:5 * d]], axis=1).astype(BF16)

    def specs(direction):
        def tile_spec():
            if direction == 0:
                return pl.BlockSpec((SEQ_PAR, None, TM, d), lambda p, j: (p, j, 0, 0))
            return pl.BlockSpec((SEQ_PAR, None, TM, d), lambda p, j: (p, tiles - 1 - j, 0, 0))
        state = pl.BlockSpec((SEQ_PAR, n_heads, HGRN_DK, HGRN_DK), lambda p, j: (p, 0, 0, 0))
        mods = pl.BlockSpec((SEQ_PAR, 6, d), lambda p, j: (p, 0, 0))
        vec = pl.BlockSpec((1, d), lambda p, j: (0, 0))
        return tile_spec, state, mods, vec

    def const(shape):
        return pl.BlockSpec(shape, lambda p, j: (0, 0))

    rows = SEQ_PAR * TM
    scratch = ([pltpu.VMEM((SEQ_PAR, n_heads, HGRN_DK, HGRN_DK), F32)]
               + [pltpu.VMEM((rows, d), F32)] * 5)
    grid = (n_seq // SEQ_PAR, tiles)
    state_shape = jax.ShapeDtypeStruct((n_seq, n_heads, HGRN_DK, HGRN_DK), F32)

    tile_spec, state, mods, vec = specs(1)
    o_bwd, s_bwd = pl.pallas_call(
        _hgrn_bwd_kernel,
        grid=grid,
        in_specs=[tile_spec(), mods, vec, const((d, 3 * d)), vec, state],
        out_specs=[tile_spec(), state],
        scratch_shapes=scratch,
        out_shape=[jax.ShapeDtypeStruct(x4.shape, BF16), state_shape],
        compiler_params=_params("arbitrary", "arbitrary"),
        name="hgrn_bwd",
    )(x4, mod_seq, g_pre.reshape(1, d), w_bwd, lb[1].reshape(1, d), s0[:, 1])

    tile_spec, state, mods, vec = specs(0)
    if tiles == per_group:
        stream_spec = pl.BlockSpec((SEQ_PAR, None, TM, d), lambda p, j: (group0 // SEQ_PAR + p, j, 0, 0))
    else:
        per = per_group // SEQ_PAR
        stream_spec = pl.BlockSpec((None, SEQ_PAR, TM, d), lambda p, j: (group0 + p // per, p % per, 0, 0))
    in_specs = [tile_spec(), mods, vec, const((d, 4 * d)), vec, state, tile_spec(),
                const((1, HGRN_DK)), const((d, d)), vec]
    args = [x4, mod_seq, g_pre.reshape(1, d), w_fwd, lb[0].reshape(1, d), s0[:, 0],
            o_bwd, g_norm.reshape(1, HGRN_DK), w_out.astype(BF16), g_post.reshape(1, d)]
    aliases = {}
    if not isinstance(stream, jax.ShapeDtypeStruct):
        aliases = {len(args): 0}
        in_specs.append(pl.BlockSpec(memory_space=pl.ANY))
        args.append(stream)
    stream, s_fwd = pl.pallas_call(
        _hgrn_fwd_kernel,
        grid=grid,
        in_specs=in_specs,
        out_specs=[stream_spec, state],
        scratch_shapes=scratch,
        out_shape=[jax.ShapeDtypeStruct(stream.shape, F32), state_shape],
        input_output_aliases=aliases,
        compiler_params=_params("arbitrary", "arbitrary"),
        name="hgrn_fwd",
    )(*args)
    return stream, s_fwd, s_bwd


def _rope_tables(n_tok):
    axis_dim = HEAD_DIM // 2
    half = axis_dim // 2
    pos = np.arange(n_tok)
    row, col = pos // GRID_W, pos % GRID_W
    inv_freq = ROPE_THETA ** (-np.arange(0, axis_dim, 2, dtype=np.float64) / axis_dim)
    lane = np.arange(HEAD_DIM)
    p = np.where(lane[None, :] < axis_dim, row[:, None], col[:, None]).astype(np.float64)
    ang = p * inv_freq[(lane % axis_dim) % half][None, :]
    sign = np.where((lane % axis_dim) < half, -1.0, 1.0)[None, :]
    return jnp.asarray(np.cos(ang), F32), jnp.asarray(np.sin(ang) * sign, F32)


def _rope(xh, cos, sin_signed):
    axis_dim = HEAD_DIM // 2
    half = axis_dim // 2
    lane = lax.broadcasted_iota(I32, xh.shape, 1)
    upper = pltpu.roll(xh, HEAD_DIM - half, 1)
    lower = pltpu.roll(xh, half, 1)
    partner = jnp.where((lane % axis_dim) < half, upper, lower)
    return xh * cos + partner * sin_signed


def _qkv_kernel(grp, x_ref, mod_ref, gpre_ref, w_ref, qg_ref, kg_ref, *rest, rope):
    del grp
    if rope:
        cos_ref, sin_ref, q_out, k_out, v_out = rest
    else:
        q_out, k_out, v_out = rest
    d = x_ref.shape[1]
    nk = N_KV_HEADS * HEAD_DIM
    h = (_rms(x_ref[...], gpre_ref[...]) * (1.0 + mod_ref[1:2]) + mod_ref[0:1]).astype(BF16)
    q = _dot(h, w_ref[:, 0:d])
    k = _dot(h, w_ref[:, d:d + nk])
    v_out[...] = _dot(h, w_ref[:, d + nk:d + 2 * nk]).astype(v_out.dtype)
    scale = HEAD_DIM ** -0.5
    for hh in range(d // HEAD_DIM):
        sl = slice(hh * HEAD_DIM, (hh + 1) * HEAD_DIM)
        qh = _rms(q[:, sl], qg_ref[...])
        if rope:
            qh = _rope(qh, cos_ref[...], sin_ref[...])
        q_out[:, sl] = (qh * scale).astype(q_out.dtype)
    for hh in range(N_KV_HEADS):
        sl = slice(hh * HEAD_DIM, (hh + 1) * HEAD_DIM)
        kh = _rms(k[:, sl], kg_ref[...])
        if rope:
            kh = _rope(kh, cos_ref[...], sin_ref[...])
        k_out[:, sl] = kh.astype(k_out.dtype)


def _qkv(x, mod, grp, g_pre, w_qkv, q_gain, k_gain, tile0, n_tiles, rope_len, kv_dtype):
    t, d = x.shape
    nk = N_KV_HEADS * HEAD_DIM
    n_out = n_tiles * TM
    tile_in = pl.BlockSpec((TM, d), lambda i, g: (tile0 + i, 0))
    in_specs = [tile_in,
                pl.BlockSpec((None, 6, d), lambda i, g: (g[tile0 + i], 0, 0)),
                pl.BlockSpec((1, d), lambda i, g: (0, 0)),
                pl.BlockSpec((d, d + 2 * nk), lambda i, g: (0, 0)),
                pl.BlockSpec((1, HEAD_DIM), lambda i, g: (0, 0)),
                pl.BlockSpec((1, HEAD_DIM), lambda i, g: (0, 0))]
    args = [grp, x, mod, g_pre.reshape(1, d), w_qkv, q_gain.reshape(1, HEAD_DIM), k_gain.reshape(1, HEAD_DIM)]
    if rope_len:
        per_seq = rope_len // TM
        cos, sin = _rope_tables(rope_len)
        in_specs += [pl.BlockSpec((TM, HEAD_DIM), lambda i, g: (i % per_seq, 0))] * 2
        args += [cos, sin]
    return pl.pallas_call(
        functools.partial(_qkv_kernel, rope=bool(rope_len)),
        grid_spec=pltpu.PrefetchScalarGridSpec(
            num_scalar_prefetch=1, grid=(n_tiles,), in_specs=in_specs,
            out_specs=[pl.BlockSpec((TM, d), lambda i, g: (i, 0)),
                       pl.BlockSpec((TM, nk), lambda i, g: (i, 0)),
                       pl.BlockSpec((TM, nk), lambda i, g: (i, 0))]),
        out_shape=[jax.ShapeDtypeStruct((n_out, d), BF16),
                   jax.ShapeDtypeStruct((n_out, nk), kv_dtype),
                   jax.ShapeDtypeStruct((n_out, nk), kv_dtype)],
        compiler_params=_params("arbitrary"),
        name="qkv_rope" if rope_len else "qkv",
    )(*args)


def _attn_kernel(q_ref, *rest, n_pieces, n_sub):
    kv, o_ref = rest[:-1], rest[-1]
    q_rows = q_ref.shape[0] // n_sub
    kv_rows = kv[-1].shape[0] // n_sub
    group = q_ref.shape[1] // HEAD_DIM
    for sub in range(n_sub):
        qr = slice(sub * q_rows, (sub + 1) * q_rows)
        if n_sub == 1:
            ks = [kv[2 * p][...].astype(BF16) for p in range(n_pieces)]
            vs = [kv[2 * p + 1][...].astype(BF16) for p in range(n_pieces)]
        else:
            kr = slice(sub * kv_rows, (sub + 1) * kv_rows)
            ks, vs = [kv[0][kr, :].astype(BF16)], [kv[1][kr, :].astype(BF16)]
        for g in range(group):
            sl = slice(g * HEAD_DIM, (g + 1) * HEAD_DIM)
            qh = q_ref[qr, sl]
            ss = [lax.dot_general(qh, kp, NT_DIMS, preferred_element_type=F32) for kp in ks]
            m = ss[0].max(axis=-1, keepdims=True)
            for s in ss[1:]:
                m = jnp.maximum(m, s.max(axis=-1, keepdims=True))
            ps = [jnp.exp(s - m) for s in ss]
            denom = ps[0].sum(axis=-1, keepdims=True)
            for p in ps[1:]:
                denom = denom + p.sum(axis=-1, keepdims=True)
            o = _dot(ps[0].astype(BF16), vs[0])
            for p, vp in zip(ps[1:], vs[1:]):
                o = o + _dot(p.astype(BF16), vp)
            o_ref[qr, sl] = (o / denom).astype(o_ref.dtype)


def _attention(q, k_new, v_new, n_seq, seq_len, cache=None):
    t, d = q.shape
    group_w = d // N_KV_HEADS
    n_sub = 1
    if cache is None and seq_len < ATT_TQ:
        n_sub = max(s for s in range(1, ATT_TQ // seq_len + 1) if n_seq % s == 0)
    if n_sub > 1:
        n_seq, seq_len = n_seq // n_sub, seq_len * n_sub
    tq = min(ATT_TQ, seq_len)
    nq = seq_len // tq
    q_spec = pl.BlockSpec((tq, group_w), lambda b, i, kv: (b * nq + i, kv))
    new_spec = pl.BlockSpec((seq_len, HEAD_DIM), lambda b, i, kv: (b, kv))
    in_specs, args = [q_spec], [q]
    if cache is not None:
        past = cache[0].shape[1]
        c_spec = pl.BlockSpec((None, past, HEAD_DIM), lambda b, i, kv: (b, 0, kv))
        in_specs += [c_spec, c_spec]
        args += list(cache)
    in_specs += [new_spec, new_spec]
    args += [k_new, v_new]
    return pl.pallas_call(
        functools.partial(_attn_kernel, n_pieces=len(args) // 2, n_sub=n_sub),
        grid=(n_seq, nq, N_KV_HEADS),
        in_specs=in_specs,
        out_specs=q_spec,
        out_shape=jax.ShapeDtypeStruct((t, d), BF16),
        compiler_params=_params("arbitrary", "arbitrary", "arbitrary"),
        name="attention",
    )(*args)


def _attn_out_kernel(grp, x_ref, oc_ref, ol_ref, mod_ref, w_ref, gpost_ref, xo_ref, *, n_ctx_tiles):
    del grp
    o = jnp.where(pl.program_id(0) < n_ctx_tiles, oc_ref[...], ol_ref[...])
    out = _dot(o, w_ref[...])
    xo_ref[...] = x_ref[...] + mod_ref[2:3] * _rms(out, gpost_ref[...])


def _attn_out(x, o_ctx, o_lat, mod, grp, w_o, g_post):
    t, d = x.shape
    nct = o_ctx.shape[0] // MM_TILE
    tile = pl.BlockSpec((MM_TILE, d), lambda i, g: (i, 0))
    return pl.pallas_call(
        functools.partial(_attn_out_kernel, n_ctx_tiles=nct),
        grid_spec=pltpu.PrefetchScalarGridSpec(
            num_scalar_prefetch=1, grid=(t // MM_TILE,),
            in_specs=[tile,
                      pl.BlockSpec((MM_TILE, d), lambda i, g: (jnp.minimum(i, nct - 1), 0)),
                      pl.BlockSpec((MM_TILE, d), lambda i, g: (jnp.maximum(i - nct, 0), 0)),
                      pl.BlockSpec((None, 6, d), lambda i, g: (g[i], 0, 0)),
                      pl.BlockSpec((d, d), lambda i, g: (0, 0)),
                      pl.BlockSpec((1, d), lambda i, g: (0, 0))],
            out_specs=tile),
        out_shape=jax.ShapeDtypeStruct((t, d), F32),
        input_output_aliases={1: 0},
        compiler_params=_params("arbitrary"),
        name="attn_out",
    )(grp, x, o_ctx, o_lat, mod, w_o, g_post.reshape(1, d))


def _to_token_tiles(ref, x):
    rows = x.shape[0]
    for s in range(x.shape[1] // LANES):
        ref[pl.ds(s, rows, stride=ROW_TILE), :] = x[:, s * LANES:(s + 1) * LANES]


def _from_token_tiles(ref, rows):
    n = ROW_TILE
    return jnp.concatenate([ref[pl.ds(s, rows, stride=n), :] for s in range(n)], axis=1)


def _router_kernel(grp, x_ref, mod_ref, g_ref, wr_ref, br_ref,
                   h_ref, route_ref, gate_ref, cnt_ref, carry):
    del grp
    @pl.when(pl.program_id(0) == 0)
    def _():
        carry[...] = jnp.zeros_like(carry)
    h = _rms(x_ref[...], g_ref[...]) * (1.0 + mod_ref[4:5]) + mod_ref[3:4]
    _to_token_tiles(h_ref, h)
    logits = _dot(h.astype(BF16), wr_ref[...]) + br_ref[...]
    lane = lax.broadcasted_iota(I32, logits.shape, 1)
    lane_f = lane.astype(F32)
    vals, idxs = [], []
    work = logits
    for _ in range(TOP_K):
        m = work.max(axis=-1, keepdims=True)
        idx = jnp.where(work == m, lane_f, float(LANES)).min(axis=-1, keepdims=True).astype(I32)
        vals.append(m)
        idxs.append(idx)
        work = jnp.where(lane == idx, -jnp.inf, work)
    es = [jnp.exp(v - vals[0]) for v in vals]
    denom = es[0]
    for e in es[1:]:
        denom = denom + e
    onehot = jnp.zeros(logits.shape, F32)
    for idx in idxs:
        onehot = onehot + jnp.where(lane == idx, 1.0, 0.0)
    ri = lax.broadcasted_iota(I32, (MM_TILE, MM_TILE), 0)
    ci = lax.broadcasted_iota(I32, (MM_TILE, MM_TILE), 1)
    before = jnp.where(ci < ri, 1.0, 0.0).astype(BF16)
    rank_all = _dot(before, onehot.astype(BF16)) + carry[0:1]
    route = jnp.zeros(logits.shape, I32)
    gate = jnp.zeros(logits.shape, F32)
    for kk in range(TOP_K):
        rank = jnp.where(lane == idxs[kk], rank_all, 0.0).sum(axis=-1, keepdims=True)
        route = jnp.where(lane == kk, idxs[kk], route)
        route = jnp.where(lane == TOP_K + kk, rank.astype(I32), route)
        gate = jnp.where(lane == kk, es[kk] / denom, gate)
    route_ref[...] = route
    gate_ref[...] = gate
    carry[...] = carry[...] + onehot.sum(axis=0, keepdims=True)
    cnt_ref[...] = carry[...]


def _router(x, mod, grp, g_pre, w_router, b_router):
    t, d = x.shape
    n_exp = w_router.shape[1]
    wr = jnp.pad(w_router.astype(BF16), ((0, 0), (0, LANES - n_exp)))
    br = jnp.pad(b_router.reshape(1, n_exp), ((0, 0), (0, LANES - n_exp)), constant_values=NEG_BIG)
    tile = pl.BlockSpec((MM_TILE, d), lambda i, g: (i, 0))
    lane_tile = pl.BlockSpec((MM_TILE, LANES), lambda i, g: (i, 0))
    return pl.pallas_call(
        _router_kernel,
        grid_spec=pltpu.PrefetchScalarGridSpec(
            num_scalar_prefetch=1, grid=(t // MM_TILE,),
            in_specs=[tile,
                      pl.BlockSpec((None, 6, d), lambda i, g: (g[i], 0, 0)),
                      pl.BlockSpec((1, d), lambda i, g: (0, 0)),
                      pl.BlockSpec((d, LANES), lambda i, g: (0, 0)),
                      pl.BlockSpec((1, LANES), lambda i, g: (0, 0))],
            out_specs=[pl.BlockSpec((MM_TILE * ROW_TILE, LANES), lambda i, g: (i, 0)), lane_tile, lane_tile,
                       pl.BlockSpec((8, LANES), lambda i, g: (0, 0))],
            scratch_shapes=[pltpu.VMEM((8, LANES), F32)]),
        out_shape=[jax.ShapeDtypeStruct((t * ROW_TILE, LANES), F32),
                   jax.ShapeDtypeStruct((t, LANES), I32),
                   jax.ShapeDtypeStruct((t, LANES), F32),
                   jax.ShapeDtypeStruct((8, LANES), F32)],
        compiler_params=_params("arbitrary"),
        name="router",
    )(grp, x, mod, g_pre.reshape(1, d), wr, br)


def _tile_copy(src, src_row, dst, dst_row, sem):
    n = ROW_TILE
    return pltpu.make_async_copy(src.at[pl.ds(pl.multiple_of(src_row * n, n), n)],
                                 dst.at[pl.ds(pl.multiple_of(dst_row * n, n), n)], sem)


def _dispatch_kernel(pad_lo, pad_hi, dest_ref, h_ref, xs_hbm, zero_s, sem, zsem):
    def issue(g, c):
        for j in range(DMA_UNROLL):
            t = g * DMA_UNROLL + j
            for kk in range(TOP_K):
                _tile_copy(h_ref, t, xs_hbm, dest_ref[t * TOP_K + kk], sem).start(priority=kk % 2)
        return c
    lax.fori_loop(0, TM // DMA_UNROLL, issue, 0)

    @pl.when(pl.program_id(0) == 0)
    def _():
        zero_s[...] = jnp.zeros_like(zero_s)
        group_rows = ZERO_GROUP * ROW_TILE

        def single(s):
            return _tile_copy(zero_s, 0, xs_hbm, s, zsem)

        def group(g):
            return pltpu.make_async_copy(
                zero_s, xs_hbm.at[pl.ds(pl.multiple_of(g * group_rows, group_rows), group_rows)], zsem)

        def per_expert(e, c):
            lo, hi = pad_lo[e], pad_hi[e]
            mid = jnp.minimum(hi, (lo + ZERO_GROUP - 1) // ZERO_GROUP * ZERO_GROUP)
            g_lo, g_hi = mid // ZERO_GROUP, hi // ZERO_GROUP
            lax.fori_loop(lo, mid, lambda s, c2: (single(s).start(), c2)[1], 0)
            lax.fori_loop(g_lo, g_hi, lambda g, c2: (group(g).start(), c2)[1], 0)
            lax.fori_loop(lo, mid, lambda s, c2: (single(s).wait(), c2)[1], 0)
            lax.fori_loop(g_lo, g_hi, lambda g, c2: (group(g).wait(), c2)[1], 0)
            return c
        lax.fori_loop(0, pad_lo.shape[0], per_expert, 0)

    for _ in range(TOP_K):
        pltpu.make_async_copy(h_ref, xs_hbm.at[pl.ds(0, TM * ROW_TILE)], sem).wait()


def _dispatch(h3, dest, pad_lo, pad_hi, n_slots):
    t = h3.shape[0] // ROW_TILE
    return pl.pallas_call(
        _dispatch_kernel,
        grid_spec=pltpu.PrefetchScalarGridSpec(
            num_scalar_prefetch=2, grid=(t // TM,),
            in_specs=[pl.BlockSpec((TM * TOP_K,), lambda i, lo, hi: (i,), memory_space=pltpu.SMEM),
                      pl.BlockSpec((TM * ROW_TILE, LANES), lambda i, lo, hi: (i, 0))],
            out_specs=pl.BlockSpec(memory_space=pl.ANY),
            scratch_shapes=[pltpu.VMEM((ZERO_GROUP * ROW_TILE, LANES), F32),
                            pltpu.SemaphoreType.DMA, pltpu.SemaphoreType.DMA]),
        out_shape=jax.ShapeDtypeStruct((n_slots * ROW_TILE, LANES), F32),
        compiler_params=_params("arbitrary"),
        name="moe_dispatch",
    )(pad_lo, pad_hi, dest, h3)


def _expert_kernel(blk0, nblk, xs_hbm, wg_ref, bg_ref, wu_ref, bu_ref, wd_ref, bd_ref,
                   y_hbm, wg_s, wu_s, wd_s, xbuf, ybuf, in_sem, out_sem):
    e = pl.program_id(0)
    first = blk0[e]
    n = nblk[e]
    rows = MOE_BM * ROW_TILE

    def in_copy(b, slot):
        return pltpu.make_async_copy(
            xs_hbm.at[pl.ds(pl.multiple_of((first + b) * rows, rows), rows)], xbuf.at[slot], in_sem.at[slot])

    def out_copy(b, slot):
        return pltpu.make_async_copy(
            ybuf.at[slot], y_hbm.at[pl.ds(pl.multiple_of((first + b) * rows, rows), rows)], out_sem.at[slot])

    @pl.when(n > 0)
    def _():
        in_copy(0, 0).start()
        wg_s[...] = wg_ref[...].astype(BF16)
        wu_s[...] = wu_ref[...].astype(BF16)
        wd_s[...] = wd_ref[...].astype(BF16)

    def block(b, c):
        slot = b % 2
        in_copy(b, slot).wait()

        @pl.when(b + 1 < n)
        def _():
            in_copy(b + 1, 1 - slot).start()

        @pl.when(b >= 2)
        def _():
            out_copy(b - 2, slot).wait()

        x = _from_token_tiles(xbuf.at[slot], MOE_BM).astype(BF16)
        hg = jnp.minimum(_dot(x, wg_s[...]) + bg_ref[...], SWIGLU_LIMIT)
        hu = jnp.clip(_dot(x, wu_s[...]) + bu_ref[...], -SWIGLU_LIMIT, SWIGLU_LIMIT)
        hh = (hu + 1.0) * hg * _sigmoid(SWIGLU_ALPHA * hg)
        _to_token_tiles(ybuf.at[slot], _dot(hh.astype(BF16), wd_s[...]) + bd_ref[...])
        out_copy(b, slot).start()
        return c
    lax.fori_loop(0, n, block, 0)

    @pl.when(n >= 2)
    def _():
        out_copy(n - 2, n % 2).wait()

    @pl.when(n >= 1)
    def _():
        out_copy(n - 1, (n - 1) % 2).wait()


def _experts(xs, blk0, nblk, layer, w_gate, b_gate, w_up, b_up, w_down, b_down):
    depth, n_exp, d, f = w_gate.shape

    def w_spec(k, n):
        return pl.BlockSpec((None, None, k, n), lambda e, b0, nb: (layer, e, 0, 0))

    rows = MOE_BM * ROW_TILE
    return pl.pallas_call(
        _expert_kernel,
        grid_spec=pltpu.PrefetchScalarGridSpec(
            num_scalar_prefetch=2, grid=(n_exp,),
            in_specs=[pl.BlockSpec(memory_space=pl.ANY),
                      w_spec(d, f), w_spec(1, f), w_spec(d, f), w_spec(1, f), w_spec(f, d), w_spec(1, d)],
            out_specs=pl.BlockSpec(memory_space=pl.ANY),
            scratch_shapes=[pltpu.VMEM((d, f), BF16), pltpu.VMEM((d, f), BF16), pltpu.VMEM((f, d), BF16),
                            pltpu.VMEM((2, rows, LANES), F32), pltpu.VMEM((2, rows, LANES), F32),
                            pltpu.SemaphoreType.DMA((2,)), pltpu.SemaphoreType.DMA((2,))]),
        out_shape=jax.ShapeDtypeStruct(xs.shape, F32),
        compiler_params=_params("arbitrary"),
        name="moe_experts",
    )(blk0, nblk, xs, w_gate, b_gate.reshape(depth, n_exp, 1, f), w_up, b_up.reshape(depth, n_exp, 1, f),
      w_down, b_down.reshape(depth, n_exp, 1, d))


def _combine_kernel(grp, dest_ref, dest_next_ref, x_ref, gate_ref, mod_ref, gpost_ref, y_hbm,
                    *rest, split):
    del grp
    outs, (ybuf, sem, acc_s) = rest[:-3], rest[-3:]
    i = pl.program_id(0)
    n = pl.num_programs(0)
    rows = TM * ROW_TILE

    def issue(dref, slot):
        def body(g, c):
            for j in range(DMA_UNROLL):
                t = g * DMA_UNROLL + j
                for kk in range(TOP_K):
                    _tile_copy(y_hbm, dref[t * TOP_K + kk], ybuf.at[slot, kk], t, sem.at[slot]).start(
                        priority=kk % 2)
            return c
        lax.fori_loop(0, TM // DMA_UNROLL, body, 0)

    @pl.when(i == 0)
    def _():
        issue(dest_ref, 0)

    @pl.when(i + 1 < n)
    def _():
        issue(dest_next_ref, (i + 1) % 2)

    slot = i % 2

    for kk in range(TOP_K):
        pltpu.make_async_copy(y_hbm.at[pl.ds(0, rows)], ybuf.at[slot, kk], sem.at[slot]).wait()

    gates = gate_ref[...]
    for s in range(ROW_TILE):
        sl = slice(s * LANES, (s + 1) * LANES)
        acc = gates[:, 0:1] * ybuf[slot, 0, pl.ds(s, TM, stride=ROW_TILE), :]
        for kk in range(1, TOP_K):
            acc = acc + gates[:, kk:kk + 1] * ybuf[slot, kk, pl.ds(s, TM, stride=ROW_TILE), :]
        acc_s[:, sl] = acc
    result = x_ref[...] + mod_ref[5:6] * _rms(acc_s[...], gpost_ref[...])
    if split is None:
        outs[0][...] = result
    else:
        @pl.when(i < split)
        def _():
            outs[0][...] = result

        @pl.when(i >= split)
        def _():
            outs[1][...] = result


def _combine(x, y, dest, gates, mod, grp, g_post, split):
    t, d = x.shape
    n_tiles = t // TM
    tile = pl.BlockSpec((TM, d), lambda i, g: (i, 0))
    if split is None:
        out_specs, out_shape, aliases = tile, jax.ShapeDtypeStruct((t, d), F32), {3: 0}
    else:
        out_specs = [pl.BlockSpec((TM, d), lambda i, g: (jnp.minimum(i, split - 1), 0)),
                     pl.BlockSpec((TM, d), lambda i, g: (jnp.maximum(i - split, 0), 0))]
        out_shape = [jax.ShapeDtypeStruct((split * TM, d), F32),
                     jax.ShapeDtypeStruct((t - split * TM, d), F32)]
        aliases = {}
    return pl.pallas_call(
        functools.partial(_combine_kernel, split=split),
        grid_spec=pltpu.PrefetchScalarGridSpec(
            num_scalar_prefetch=1, grid=(n_tiles,),
            in_specs=[pl.BlockSpec((TM * TOP_K,), lambda i, g: (i,), memory_space=pltpu.SMEM),
                      pl.BlockSpec((TM * TOP_K,), lambda i, g: (jnp.minimum(i + 1, n_tiles - 1),),
                                   memory_space=pltpu.SMEM),
                      tile,
                      pl.BlockSpec((TM, LANES), lambda i, g: (i, 0)),
                      pl.BlockSpec((None, 6, d), lambda i, g: (g[i], 0, 0)),
                      pl.BlockSpec((1, d), lambda i, g: (0, 0)),
                      pl.BlockSpec(memory_space=pl.ANY)],
            out_specs=out_specs,
            scratch_shapes=[pltpu.VMEM((2, TOP_K, TM * ROW_TILE, LANES), F32),
                            pltpu.SemaphoreType.DMA((2,)), pltpu.VMEM((TM, d), F32)]),
        out_shape=out_shape,
        input_output_aliases=aliases,
        compiler_params=_params("arbitrary"),
        name="moe_combine",
    )(grp, dest, dest, x, gates, mod, g_post.reshape(1, d), y)


def _moe_layer(x, mod, grp, grp_mm, layer, g_pre, g_post, w_router, b_router,
               w_gate, b_gate, w_up, b_up, w_down, b_down, split=None):
    t, d = x.shape
    n_exp = w_router.shape[1]
    assert d == ROW_TILE * LANES
    h3, route, gates, cnt = _router(x, mod, grp_mm, g_pre, w_router, b_router)
    idx = route[:, 0:TOP_K]
    rank = route[:, TOP_K:2 * TOP_K]
    counts = cnt[0, :n_exp].astype(I32)
    padded = (counts + MOE_BM - 1) // MOE_BM * MOE_BM
    pad_end = jnp.cumsum(padded)
    pad_start = pad_end - padded
    experts = jnp.arange(n_exp, dtype=I32)
    dest = (rank + jnp.sum(jnp.where(idx[..., None] == experts, pad_start, 0), axis=-1)).reshape(-1)
    n_blocks = t * TOP_K // MOE_BM + n_exp
    xs = _dispatch(h3, dest, pad_start + counts, pad_end, n_blocks * MOE_BM)
    y = _experts(xs, pad_start // MOE_BM, padded // MOE_BM, layer,
                 w_gate, b_gate, w_up, b_up, w_down, b_down)
    return _combine(x, y, dest, gates, mod, grp, g_post, split)


def kernel(x_prompt, x_sample, state_hgrn, cache_k, cache_v, c, c_ctx, w_mod, b_mod, g_pre_mix, g_post_mix, g_pre_ffn, g_post_ffn, hgrn_w_in, hgrn_gamma, hgrn_g_norm, hgrn_w_out, attn_w_qkv, attn_q_gain, attn_k_gain, attn_w_o, moe_w_router, moe_b_router, moe_w_gate, moe_b_gate, moe_w_up, moe_b_up, moe_w_down, moe_b_down):
    bp, tp, d = x_prompt.shape
    bs, ts, _ = x_sample.shape
    depth = w_mod.shape[0]
    n_heads = d // HGRN_DK
    nk = N_KV_HEADS * HEAD_DIM
    assert tp % TM == 0 and ts % MM_TILE == 0 and ts % GRID_W == 0 and d % LANES == 0
    n_ctx = bp * tp
    assert n_ctx % MM_TILE == 0
    ctx_tiles, lat_tiles = tp // TM, ts // TM
    grp = _tile_groups(bp * ctx_tiles, bs, lat_tiles)
    grp_mm = _tile_groups(n_ctx // MM_TILE, bs, ts // MM_TILE)

    rows = -(-(1 + bs) // 8) * 8
    cvec = jnp.zeros((rows, d), F32).at[0].set(c_ctx).at[1:1 + bs].set(c)
    mod_all = _mod_table(cvec, w_mod, b_mod).reshape(depth, rows, 6, d)

    lb_all = jnp.cumsum(jax.nn.softmax(hgrn_gamma.astype(F32), axis=0), axis=0)

    new_states, new_k, new_v = [], [], []
    x = None
    x_parts = (x_prompt, x_sample)
    for l in range(depth):
        mod = mod_all[l]
        if l % 2 == 0:
            a = l // 2
            if x is not None:
                x_parts = (x[:n_ctx].reshape(bp, tp, d), x[n_ctx:].reshape(bs, ts, d))
            hgrn = functools.partial(_hgrn_layer, g_pre=g_pre_mix[l], g_post=g_post_mix[l], w_in=hgrn_w_in[a],
                                     lb=lb_all[l], g_norm=hgrn_g_norm[a], w_out=hgrn_w_out[a])
            assert (bp * ctx_tiles) % (lat_tiles * SEQ_PAR) == 0
            ctx_groups = bp * ctx_tiles // lat_tiles
            stream = jax.ShapeDtypeStruct((ctx_groups + bs, lat_tiles, TM, d), F32)
            stream, s_f, s_b = hgrn(x_parts[0], jnp.broadcast_to(mod[0], (bp, 6, d)),
                                    s0=jnp.zeros((bp, 2, n_heads, HGRN_DK, HGRN_DK), F32),
                                    stream=stream, group0=0)
            stream, _, _ = hgrn(x_parts[1], mod[1:1 + bs], s0=state_hgrn[:, a].astype(F32),
                                stream=stream, group0=ctx_groups)
            x = stream.reshape(n_ctx + bs * ts, d)
            new_states.append(jnp.stack([s_f, s_b], axis=1))
        else:
            m = l // 2
            w_qkv = attn_w_qkv[m].astype(BF16)
            q_c, k_c, v_c = _qkv(x, mod, grp, g_pre_mix[l], w_qkv, attn_q_gain[m], attn_k_gain[m],
                                 0, n_ctx // TM, 0, F32)
            q_l, k_l, v_l = _qkv(x, mod, grp, g_pre_mix[l], w_qkv, attn_q_gain[m], attn_k_gain[m],
                                 n_ctx // TM, bs * ts // TM, ts, BF16)
            o_c = _attention(q_c, k_c, v_c, bp, tp)
            past = cache_k.shape[2]
            o_l = _attention(q_l, k_l, v_l, bs, ts,
                             cache=(cache_k[:, m].reshape(bs, past, nk), cache_v[:, m].reshape(bs, past, nk)))
            new_k.append(k_c.reshape(bp, tp, N_KV_HEADS, HEAD_DIM))
            new_v.append(v_c.reshape(bp, tp, N_KV_HEADS, HEAD_DIM))
            x = _attn_out(x, o_c, o_l, mod, grp_mm, attn_w_o[m].astype(BF16), g_post_mix[l])
        x = _moe_layer(x, mod, grp, grp_mm, l, g_pre_ffn[l], g_post_ffn[l], moe_w_router[l], moe_b_router[l],
                       moe_w_gate, moe_b_gate, moe_w_up, moe_b_up, moe_w_down, moe_b_down,
                       split=bp * ctx_tiles if l == depth - 1 else None)

    y_prompt, y_sample = x
    return (y_prompt.reshape(bp, tp, d), y_sample.reshape(bs, ts, d), jnp.stack(new_states, axis=1),
            jnp.stack(new_k, axis=1), jnp.stack(new_v, axis=1))
```

```python
import functools

import numpy as np
import jax
import jax.numpy as jnp
from jax import lax
from jax.experimental import pallas as pl
from jax.experimental.pallas import tpu as pltpu

F32 = jnp.float32
BF16 = jnp.bfloat16
I32 = jnp.int32

EPS = 1e-6
GRID_W = 64
ROPE_THETA = 10000.0
HGRN_DK = 128
HEAD_DIM = 128
N_KV_HEADS = 2
N_EXPERTS = 32
TOP_K = 4
SWIGLU_LIMIT = 7.0
SWIGLU_ALPHA = 1.702

LANES = 128
TM = 256
MM_TILE = 1024
ATT_TQ = 1024
GLA_CHUNK = 64
GLA_GROUP = 4
SEQ_PAR = 2
MOE_BM = 512
ROW_TILE = 8
DMA_UNROLL = 8
ZERO_GROUP = 16
MOD_COLS = 512
VMEM_LIMIT = 56 * 1024 * 1024
NEG_BIG = -1e30

NT_DIMS = (((1,), (1,)), ((), ()))
TN_DIMS = (((0,), (0,)), ((), ()))


def _params(*sem):
    return pltpu.CompilerParams(dimension_semantics=sem, vmem_limit_bytes=VMEM_LIMIT)


def _sigmoid(x):
    return 1.0 / (1.0 + jnp.exp(-x))


def _silu(x):
    return x * _sigmoid(x)


def _rms(x, gain):
    ms = jnp.mean(x * x, axis=-1, keepdims=True)
    return x * lax.rsqrt(ms + EPS) * gain


def _dot(a, b):
    return jnp.dot(a, b, preferred_element_type=F32)


def _mod_kernel(c_ref, w_ref, b_ref, o_ref):
    s = _silu(c_ref[...]).astype(BF16)
    o_ref[...] = _dot(s, w_ref[...].astype(BF16)) + b_ref[...]


def _mod_table(cvec, w_mod, b_mod):
    depth, d, n = w_mod.shape
    rows = cvec.shape[0]
    return pl.pallas_call(
        _mod_kernel,
        grid=(depth, n // MOD_COLS),
        in_specs=[
            pl.BlockSpec((rows, d), lambda l, j: (0, 0)),
            pl.BlockSpec((None, d, MOD_COLS), lambda l, j: (l, 0, j)),
            pl.BlockSpec((None, 1, MOD_COLS), lambda l, j: (l, 0, j)),
        ],
        out_specs=pl.BlockSpec((None, rows, MOD_COLS), lambda l, j: (l, 0, j)),
        out_shape=jax.ShapeDtypeStruct((depth, rows, n), F32),
        compiler_params=_params("arbitrary", "arbitrary"),
        name="mod_table",
    )(cvec, w_mod, b_mod.reshape(depth, 1, n))


def _gla_tiles(q_s, k_s, lf_s, v_s, st_ref, o_s, direction):
    c = GLA_CHUNK
    pair = GLA_GROUP * c
    d_model = q_s.shape[1]
    n_heads = d_model // HGRN_DK
    ri = lax.broadcasted_iota(I32, (pair, pair), 0)
    ci = lax.broadcasted_iota(I32, (pair, pair), 1)
    causal = (ci <= ri) if direction == 0 else (ci >= ri)
    shift = c.bit_length() - 1
    assert c == 1 << shift
    keep = jnp.logical_and(causal, (ri >> shift) == (ci >> shift))
    ones_tri = jnp.where(keep, 1.0, 0.0).astype(BF16)
    pairs = list(range(TM // pair))
    halves = list(range(GLA_GROUP))
    if direction == 1:
        pairs.reverse()
        halves.reverse()
    for pr, seq in [(pr, seq) for pr in pairs for seq in range(SEQ_PAR)]:
        row0 = seq * TM + pr * pair
        rows = pl.ds(row0, pair)
        lf = lf_s[rows, :]
        lf_hi = lf.astype(BF16)
        lf_lo = (lf - lf_hi.astype(F32)).astype(BF16)
        cum2 = _dot(ones_tri, lf_hi) + _dot(ones_tri, lf_lo)
        q_dec, k_inv, q_in, k_end, e_tot = [], [], [], [], []
        for hf in range(GLA_GROUP):
            cum = cum2[hf * c:(hf + 1) * c]
            part = pl.ds(row0 + hf * c, c)
            if direction == 0:
                btot, bmid = cum[c - 1:c], cum[c // 2 - 1:c // 2]
            else:
                btot, bmid = cum[0:1], cum[c // 2:c // 2 + 1]
            qd = q_s[part, :] * jnp.exp(cum - bmid)
            ki = k_s[part, :] * jnp.exp(bmid - cum)
            q_in.append((qd * jnp.exp(bmid)).astype(BF16))
            k_end.append((ki * jnp.exp(btot - bmid)).astype(BF16))
            q_dec.append(qd.astype(BF16))
            k_inv.append(ki.astype(BF16))
            e_tot.append(jnp.exp(btot))
        q_dec = jnp.concatenate(q_dec, axis=0)
        k_inv = jnp.concatenate(k_inv, axis=0)
        vc = v_s[rows, :].astype(BF16)
        for h in range(n_heads):
            sl = slice(h * HGRN_DK, (h + 1) * HGRN_DK)
            a = lax.dot_general(q_dec[:, sl], k_inv[:, sl], NT_DIMS, preferred_element_type=F32)
            a = jnp.where(keep, a, 0.0).astype(BF16)
            o_intra = _dot(a, vc[:, sl])
            for hf in halves:
                part = slice(hf * c, (hf + 1) * c)
                st = st_ref[seq, h]
                o_s[pl.ds(row0 + hf * c, c), sl] = o_intra[part] + lax.dot_general(
                    q_in[hf][:, sl], st.astype(BF16), NT_DIMS, preferred_element_type=F32)
                st_ref[seq, h] = st * e_tot[hf][:, sl] + lax.dot_general(
                    vc[part, sl], k_end[hf][:, sl], TN_DIMS, preferred_element_type=F32)


def _hgrn_gates(h, w_ref, lb, q_s, k_s, lf_s, v_s):
    d = h.shape[1]
    q_s[...] = _silu(_dot(h, w_ref[:, 0:d]))
    f = lb + (1.0 - lb) * _sigmoid(_dot(h, w_ref[:, d:2 * d]))
    lf_s[...] = jnp.log(f)
    k_s[...] = 1.0 - f
    v_s[...] = _dot(h, w_ref[:, 2 * d:3 * d])


def _load_state(s0_ref, st_s):
    @pl.when(pl.program_id(1) == 0)
    def _():
        for s in range(st_s.shape[0]):
            for h in range(st_s.shape[1]):
                st_s[s, h] = s0_ref[s, h].T


def _store_state(sfin_ref, st_s):
    @pl.when(pl.program_id(1) == pl.num_programs(1) - 1)
    def _():
        for s in range(st_s.shape[0]):
            for h in range(st_s.shape[1]):
                sfin_ref[s, h] = st_s[s, h].T


def _hgrn_norm(x_ref, mod_ref, gpre_ref):
    hs = [(_rms(x_ref[s], gpre_ref[...]) * (1.0 + mod_ref[s, 1:2]) + mod_ref[s, 0:1]).astype(BF16)
          for s in range(SEQ_PAR)]
    return jnp.concatenate(hs, axis=0)


def _hgrn_bwd_kernel(x_ref, mod_ref, gpre_ref, w_ref, lb_ref, s0_ref,
                     ob_ref, qb_ref, vb_ref, sfin_ref, st_s, o_s, q_s, k_s, lf_s, v_s):
    _load_state(s0_ref, st_s)
    h = _hgrn_norm(x_ref, mod_ref, gpre_ref)
    _hgrn_gates(h, w_ref, lb_ref[...], q_s, k_s, lf_s, v_s)
    for s in range(SEQ_PAR):
        rows = slice(s * TM, (s + 1) * TM)
        qb_ref[s] = q_s[rows, :].astype(BF16)
        vb_ref[s] = v_s[rows, :].astype(BF16)
    _gla_tiles(q_s, k_s, lf_s, v_s, st_s, o_s, 1)
    for s in range(SEQ_PAR):
        ob_ref[s] = o_s[s * TM:(s + 1) * TM, :].astype(BF16)
    _store_state(sfin_ref, st_s)


def _hgrn_fwd_kernel(x_ref, mod_ref, gpre_ref, w_ref, lb_ref, s0_ref, ob_ref, qb_ref, vb_ref,
                     gnorm_ref, wout_ref, gpost_ref, *rest):
    xo_ref, sfin_ref, st_s, o_s, q_s, k_s, lf_s, v_s = rest[-8:]
    _load_state(s0_ref, st_s)
    d = x_ref.shape[2]
    h = _hgrn_norm(x_ref, mod_ref, gpre_ref)
    lb = lb_ref[...]
    f = lb + (1.0 - lb) * _sigmoid(_dot(h, w_ref[:, 0:d]))
    lf_s[...] = jnp.log(f)
    k_s[...] = 1.0 - f
    for s in range(SEQ_PAR):
        rows = slice(s * TM, (s + 1) * TM)
        q_s[rows, :] = qb_ref[s].astype(F32)
        v_s[rows, :] = vb_ref[s].astype(F32)
    _gla_tiles(q_s, k_s, lf_s, v_s, st_s, o_s, 0)
    _store_state(sfin_ref, st_s)
    gate = _silu(_dot(h, w_ref[:, d:2 * d]))
    gnorm = gnorm_ref[...]
    o_b = jnp.concatenate([ob_ref[s] for s in range(SEQ_PAR)], axis=0).astype(F32)
    for hh in range(d // HGRN_DK):
        sl = slice(hh * HGRN_DK, (hh + 1) * HGRN_DK)
        o_s[:, sl] = _rms(o_s[:, sl] + o_b[:, sl], gnorm) * gate[:, sl]
    out = _dot(o_s[...].astype(BF16), wout_ref[...])
    for s in range(SEQ_PAR):
        xo_ref[s] = x_ref[s] + mod_ref[s, 2:3] * _rms(out[s * TM:(s + 1) * TM], gpost_ref[...])


def _tile_groups(n_ctx_tiles, n_lat_seq, lat_tiles):
    g = [0] * n_ctx_tiles
    for b in range(n_lat_seq):
        g += [1 + b] * lat_tiles
    return jnp.asarray(np.array(g, dtype=np.int32))


def _hgrn_layer(x, mod_seq, g_pre, g_post, w_in, lb, g_norm, w_out, s0, stream, group0):
    n_seq, seq_len, d = x.shape
    n_heads = d // HGRN_DK
    tiles = seq_len // TM
    per_group = stream.shape[1]
    assert n_seq % SEQ_PAR == 0 and per_group % SEQ_PAR == 0 and tiles in (1, per_group)
    x4 = x.reshape(n_seq, tiles, TM, d)
    w_bwd = jnp.concatenate([w_in[:, 0:d], w_in[:, 2 * d:3 * d], w_in[:, 3 * d:4 * d]], axis=1).astype(BF16)
    w_fwd = jnp.concatenate([w_in[:, d:2 * d], w_in[:, 4 * d:5 * d]], axis=1).astype(BF16)

    def specs(direction):
        def tile_spec():
            if direction == 0:
                return pl.BlockSpec((SEQ_PAR, None, TM, d), lambda p, j: (p, j, 0, 0))
            return pl.BlockSpec((SEQ_PAR, None, TM, d), lambda p, j: (p, tiles - 1 - j, 0, 0))
        state = pl.BlockSpec((SEQ_PAR, n_heads, HGRN_DK, HGRN_DK), lambda p, j: (p, 0, 0, 0))
        mods = pl.BlockSpec((SEQ_PAR, 6, d), lambda p, j: (p, 0, 0))
        vec = pl.BlockSpec((1, d), lambda p, j: (0, 0))
        return tile_spec, state, mods, vec

    def const(shape):
        return pl.BlockSpec(shape, lambda p, j: (0, 0))

    rows = SEQ_PAR * TM
    scratch = ([pltpu.VMEM((SEQ_PAR, n_heads, HGRN_DK, HGRN_DK), F32)]
               + [pltpu.VMEM((rows, d), F32)] * 5)
    grid = (n_seq // SEQ_PAR, tiles)
    state_shape = jax.ShapeDtypeStruct((n_seq, n_heads, HGRN_DK, HGRN_DK), F32)

    tile_spec, state, mods, vec = specs(1)
    o_bwd, q_act, v_act, s_bwd = pl.pallas_call(
        _hgrn_bwd_kernel,
        grid=grid,
        in_specs=[tile_spec(), mods, vec, const((d, 3 * d)), vec, state],
        out_specs=[tile_spec(), tile_spec(), tile_spec(), state],
        scratch_shapes=scratch,
        out_shape=[jax.ShapeDtypeStruct(x4.shape, BF16)] * 3 + [state_shape],
        compiler_params=_params("arbitrary", "arbitrary"),
        name="hgrn_bwd",
    )(x4, mod_seq, g_pre.reshape(1, d), w_bwd, lb[1].reshape(1, d), s0[:, 1])

    tile_spec, state, mods, vec = specs(0)
    if tiles == per_group:
        stream_spec = pl.BlockSpec((SEQ_PAR, None, TM, d), lambda p, j: (group0 // SEQ_PAR + p, j, 0, 0))
    else:
        per = per_group // SEQ_PAR
        stream_spec = pl.BlockSpec((None, SEQ_PAR, TM, d), lambda p, j: (group0 + p // per, p % per, 0, 0))
    in_specs = [tile_spec(), mods, vec, const((d, 2 * d)), vec, state, tile_spec(), tile_spec(), tile_spec(),
                const((1, HGRN_DK)), const((d, d)), vec]
    args = [x4, mod_seq, g_pre.reshape(1, d), w_fwd, lb[0].reshape(1, d), s0[:, 0],
            o_bwd, q_act, v_act, g_norm.reshape(1, HGRN_DK), w_out.astype(BF16), g_post.reshape(1, d)]
    aliases = {}
    if not isinstance(stream, jax.ShapeDtypeStruct):
        aliases = {len(args): 0}
        in_specs.append(pl.BlockSpec(memory_space=pl.ANY))
        args.append(stream)
    stream, s_fwd = pl.pallas_call(
        _hgrn_fwd_kernel,
        grid=grid,
        in_specs=in_specs,
        out_specs=[stream_spec, state],
        scratch_shapes=scratch,
        out_shape=[jax.ShapeDtypeStruct(stream.shape, F32), state_shape],
        input_output_aliases=aliases,
        compiler_params=_params("arbitrary", "arbitrary"),
        name="hgrn_fwd",
    )(*args)
    return stream, s_fwd, s_bwd


def _rope_tables(n_tok):
    axis_dim = HEAD_DIM // 2
    half = axis_dim // 2
    pos = np.arange(n_tok)
    row, col = pos // GRID_W, pos % GRID_W
    inv_freq = ROPE_THETA ** (-np.arange(0, axis_dim, 2, dtype=np.float64) / axis_dim)
    lane = np.arange(HEAD_DIM)
    p = np.where(lane[None, :] < axis_dim, row[:, None], col[:, None]).astype(np.float64)
    ang = p * inv_freq[(lane % axis_dim) % half][None, :]
    sign = np.where((lane % axis_dim) < half, -1.0, 1.0)[None, :]
    return jnp.asarray(np.cos(ang), F32), jnp.asarray(np.sin(ang) * sign, F32)


def _rope(xh, cos, sin_signed):
    axis_dim = HEAD_DIM // 2
    half = axis_dim // 2
    lane = lax.broadcasted_iota(I32, xh.shape, 1)
    upper = pltpu.roll(xh, HEAD_DIM - half, 1)
    lower = pltpu.roll(xh, half, 1)
    partner = jnp.where((lane % axis_dim) < half, upper, lower)
    return xh * cos + partner * sin_signed


def _qkv_kernel(grp, x_ref, mod_ref, gpre_ref, w_ref, qg_ref, kg_ref, *rest, rope):
    del grp
    if rope:
        cos_ref, sin_ref, q_out, k_out, v_out = rest
    else:
        q_out, k_out, v_out = rest
    d = x_ref.shape[1]
    nk = N_KV_HEADS * HEAD_DIM
    h = (_rms(x_ref[...], gpre_ref[...]) * (1.0 + mod_ref[1:2]) + mod_ref[0:1]).astype(BF16)
    q = _dot(h, w_ref[:, 0:d])
    k = _dot(h, w_ref[:, d:d + nk])
    v_out[...] = _dot(h, w_ref[:, d + nk:d + 2 * nk]).astype(v_out.dtype)
    scale = HEAD_DIM ** -0.5
    for hh in range(d // HEAD_DIM):
        sl = slice(hh * HEAD_DIM, (hh + 1) * HEAD_DIM)
        qh = _rms(q[:, sl], qg_ref[...])
        if rope:
            qh = _rope(qh, cos_ref[...], sin_ref[...])
        q_out[:, sl] = (qh * scale).astype(q_out.dtype)
    for hh in range(N_KV_HEADS):
        sl = slice(hh * HEAD_DIM, (hh + 1) * HEAD_DIM)
        kh = _rms(k[:, sl], kg_ref[...])
        if rope:
            kh = _rope(kh, cos_ref[...], sin_ref[...])
        k_out[:, sl] = kh.astype(k_out.dtype)


def _qkv(x, mod, grp, g_pre, w_qkv, q_gain, k_gain, tile0, n_tiles, rope_len, kv_dtype):
    t, d = x.shape
    nk = N_KV_HEADS * HEAD_DIM
    n_out = n_tiles * TM
    tile_in = pl.BlockSpec((TM, d), lambda i, g: (tile0 + i, 0))
    in_specs = [tile_in,
                pl.BlockSpec((None, 6, d), lambda i, g: (g[tile0 + i], 0, 0)),
                pl.BlockSpec((1, d), lambda i, g: (0, 0)),
                pl.BlockSpec((d, d + 2 * nk), lambda i, g: (0, 0)),
                pl.BlockSpec((1, HEAD_DIM), lambda i, g: (0, 0)),
                pl.BlockSpec((1, HEAD_DIM), lambda i, g: (0, 0))]
    args = [grp, x, mod, g_pre.reshape(1, d), w_qkv, q_gain.reshape(1, HEAD_DIM), k_gain.reshape(1, HEAD_DIM)]
    if rope_len:
        per_seq = rope_len // TM
        cos, sin = _rope_tables(rope_len)
        in_specs += [pl.BlockSpec((TM, HEAD_DIM), lambda i, g: (i % per_seq, 0))] * 2
        args += [cos, sin]
    return pl.pallas_call(
        functools.partial(_qkv_kernel, rope=bool(rope_len)),
        grid_spec=pltpu.PrefetchScalarGridSpec(
            num_scalar_prefetch=1, grid=(n_tiles,), in_specs=in_specs,
            out_specs=[pl.BlockSpec((TM, d), lambda i, g: (i, 0)),
                       pl.BlockSpec((TM, nk), lambda i, g: (i, 0)),
                       pl.BlockSpec((TM, nk), lambda i, g: (i, 0))]),
        out_shape=[jax.ShapeDtypeStruct((n_out, d), BF16),
                   jax.ShapeDtypeStruct((n_out, nk), kv_dtype),
                   jax.ShapeDtypeStruct((n_out, nk), kv_dtype)],
        compiler_params=_params("arbitrary"),
        name="qkv_rope" if rope_len else "qkv",
    )(*args)


def _attn_kernel(q_ref, *rest, n_pieces, n_sub):
    kv, o_ref = rest[:-1], rest[-1]
    q_rows = q_ref.shape[0] // n_sub
    kv_rows = kv[-1].shape[0] // n_sub
    group = q_ref.shape[1] // HEAD_DIM
    for sub in range(n_sub):
        qr = slice(sub * q_rows, (sub + 1) * q_rows)
        if n_sub == 1:
            ks = [kv[2 * p][...].astype(BF16) for p in range(n_pieces)]
            vs = [kv[2 * p + 1][...].astype(BF16) for p in range(n_pieces)]
        else:
            kr = slice(sub * kv_rows, (sub + 1) * kv_rows)
            ks, vs = [kv[0][kr, :].astype(BF16)], [kv[1][kr, :].astype(BF16)]
        for g in range(group):
            sl = slice(g * HEAD_DIM, (g + 1) * HEAD_DIM)
            qh = q_ref[qr, sl]
            ss = [lax.dot_general(qh, kp, NT_DIMS, preferred_element_type=F32) for kp in ks]
            m = ss[0].max(axis=-1, keepdims=True)
            for s in ss[1:]:
                m = jnp.maximum(m, s.max(axis=-1, keepdims=True))
            ps = [jnp.exp(s - m) for s in ss]
            denom = ps[0].sum(axis=-1, keepdims=True)
            for p in ps[1:]:
                denom = denom + p.sum(axis=-1, keepdims=True)
            o = _dot(ps[0].astype(BF16), vs[0])
            for p, vp in zip(ps[1:], vs[1:]):
                o = o + _dot(p.astype(BF16), vp)
            o_ref[qr, sl] = (o / denom).astype(o_ref.dtype)


def _attention(q, k_new, v_new, n_seq, seq_len, cache=None):
    t, d = q.shape
    group_w = d // N_KV_HEADS
    n_sub = 1
    if cache is None and seq_len < ATT_TQ:
        n_sub = max(s for s in range(1, ATT_TQ // seq_len + 1) if n_seq % s == 0)
    if n_sub > 1:
        n_seq, seq_len = n_seq // n_sub, seq_len * n_sub
    tq = min(ATT_TQ, seq_len)
    nq = seq_len // tq
    q_spec = pl.BlockSpec((tq, group_w), lambda b, i, kv: (b * nq + i, kv))
    new_spec = pl.BlockSpec((seq_len, HEAD_DIM), lambda b, i, kv: (b, kv))
    in_specs, args = [q_spec], [q]
    if cache is not None:
        past = cache[0].shape[1]
        c_spec = pl.BlockSpec((None, past, HEAD_DIM), lambda b, i, kv: (b, 0, kv))
        in_specs += [c_spec, c_spec]
        args += list(cache)
    in_specs += [new_spec, new_spec]
    args += [k_new, v_new]
    return pl.pallas_call(
        functools.partial(_attn_kernel, n_pieces=len(args) // 2, n_sub=n_sub),
        grid=(n_seq, nq, N_KV_HEADS),
        in_specs=in_specs,
        out_specs=q_spec,
        out_shape=jax.ShapeDtypeStruct((t, d), BF16),
        compiler_params=_params("arbitrary", "arbitrary", "arbitrary"),
        name="attention",
    )(*args)


def _attn_out_kernel(grp, x_ref, oc_ref, ol_ref, mod_ref, w_ref, gpost_ref, xo_ref, *, n_ctx_tiles):
    del grp
    o = jnp.where(pl.program_id(0) < n_ctx_tiles, oc_ref[...], ol_ref[...])
    out = _dot(o, w_ref[...])
    xo_ref[...] = x_ref[...] + mod_ref[2:3] * _rms(out, gpost_ref[...])


def _attn_out(x, o_ctx, o_lat, mod, grp, w_o, g_post):
    t, d = x.shape
    nct = o_ctx.shape[0] // MM_TILE
    tile = pl.BlockSpec((MM_TILE, d), lambda i, g: (i, 0))
    return pl.pallas_call(
        functools.partial(_attn_out_kernel, n_ctx_tiles=nct),
        grid_spec=pltpu.PrefetchScalarGridSpec(
            num_scalar_prefetch=1, grid=(t // MM_TILE,),
            in_specs=[tile,
                      pl.BlockSpec((MM_TILE, d), lambda i, g: (jnp.minimum(i, nct - 1), 0)),
                      pl.BlockSpec((MM_TILE, d), lambda i, g: (jnp.maximum(i - nct, 0), 0)),
                      pl.BlockSpec((None, 6, d), lambda i, g: (g[i], 0, 0)),
                      pl.BlockSpec((d, d), lambda i, g: (0, 0)),
                      pl.BlockSpec((1, d), lambda i, g: (0, 0))],
            out_specs=tile),
        out_shape=jax.ShapeDtypeStruct((t, d), F32),
        input_output_aliases={1: 0},
        compiler_params=_params("arbitrary"),
        name="attn_out",
    )(grp, x, o_ctx, o_lat, mod, w_o, g_post.reshape(1, d))


def _to_token_tiles(ref, x):
    rows = x.shape[0]
    for s in range(x.shape[1] // LANES):
        ref[pl.ds(s, rows, stride=ROW_TILE), :] = x[:, s * LANES:(s + 1) * LANES]


def _from_token_tiles(ref, rows):
    n = ROW_TILE
    return jnp.concatenate([ref[pl.ds(s, rows, stride=n), :] for s in range(n)], axis=1)


def _router_kernel(grp, x_ref, mod_ref, g_ref, wr_ref, br_ref,
                   h_ref, route_ref, gate_ref, cnt_ref, carry):
    del grp
    @pl.when(pl.program_id(0) == 0)
    def _():
        carry[...] = jnp.zeros_like(carry)
    h = _rms(x_ref[...], g_ref[...]) * (1.0 + mod_ref[4:5]) + mod_ref[3:4]
    _to_token_tiles(h_ref, h)
    logits = _dot(h.astype(BF16), wr_ref[...]) + br_ref[...]
    lane = lax.broadcasted_iota(I32, logits.shape, 1)
    lane_f = lane.astype(F32)
    vals, idxs = [], []
    work = logits
    for _ in range(TOP_K):
        m = work.max(axis=-1, keepdims=True)
        idx = jnp.where(work == m, lane_f, float(LANES)).min(axis=-1, keepdims=True).astype(I32)
        vals.append(m)
        idxs.append(idx)
        work = jnp.where(lane == idx, -jnp.inf, work)
    es = [jnp.exp(v - vals[0]) for v in vals]
    denom = es[0]
    for e in es[1:]:
        denom = denom + e
    onehot = jnp.zeros(logits.shape, F32)
    for idx in idxs:
        onehot = onehot + jnp.where(lane == idx, 1.0, 0.0)
    ri = lax.broadcasted_iota(I32, (MM_TILE, MM_TILE), 0)
    ci = lax.broadcasted_iota(I32, (MM_TILE, MM_TILE), 1)
    before = jnp.where(ci < ri, 1.0, 0.0).astype(BF16)
    rank_all = _dot(before, onehot.astype(BF16)) + carry[0:1]
    route = jnp.zeros(logits.shape, I32)
    gate = jnp.zeros(logits.shape, F32)
    for kk in range(TOP_K):
        rank = jnp.where(lane == idxs[kk], rank_all, 0.0).sum(axis=-1, keepdims=True)
        route = jnp.where(lane == kk, idxs[kk], route)
        route = jnp.where(lane == TOP_K + kk, rank.astype(I32), route)
        gate = jnp.where(lane == kk, es[kk] / denom, gate)
    route_ref[...] = route
    gate_ref[...] = gate
    carry[...] = carry[...] + onehot.sum(axis=0, keepdims=True)
    cnt_ref[...] = carry[...]


def _router(x, mod, grp, g_pre, w_router, b_router):
    t, d = x.shape
    n_exp = w_router.shape[1]
    wr = jnp.pad(w_router.astype(BF16), ((0, 0), (0, LANES - n_exp)))
    br = jnp.pad(b_router.reshape(1, n_exp), ((0, 0), (0, LANES - n_exp)), constant_values=NEG_BIG)
    tile = pl.BlockSpec((MM_TILE, d), lambda i, g: (i, 0))
    lane_tile = pl.BlockSpec((MM_TILE, LANES), lambda i, g: (i, 0))
    return pl.pallas_call(
        _router_kernel,
        grid_spec=pltpu.PrefetchScalarGridSpec(
            num_scalar_prefetch=1, grid=(t // MM_TILE,),
            in_specs=[tile,
                      pl.BlockSpec((None, 6, d), lambda i, g: (g[i], 0, 0)),
                      pl.BlockSpec((1, d), lambda i, g: (0, 0)),
                      pl.BlockSpec((d, LANES), lambda i, g: (0, 0)),
                      pl.BlockSpec((1, LANES), lambda i, g: (0, 0))],
            out_specs=[pl.BlockSpec((MM_TILE * ROW_TILE, LANES), lambda i, g: (i, 0)), lane_tile, lane_tile,
                       pl.BlockSpec((8, LANES), lambda i, g: (0, 0))],
            scratch_shapes=[pltpu.VMEM((8, LANES), F32)]),
        out_shape=[jax.ShapeDtypeStruct((t * ROW_TILE, LANES), F32),
                   jax.ShapeDtypeStruct((t, LANES), I32),
                   jax.ShapeDtypeStruct((t, LANES), F32),
                   jax.ShapeDtypeStruct((8, LANES), F32)],
        compiler_params=_params("arbitrary"),
        name="router",
    )(grp, x, mod, g_pre.reshape(1, d), wr, br)


def _tile_copy(src, src_row, dst, dst_row, sem):
    n = ROW_TILE
    return pltpu.make_async_copy(src.at[pl.ds(pl.multiple_of(src_row * n, n), n)],
                                 dst.at[pl.ds(pl.multiple_of(dst_row * n, n), n)], sem)


def _dispatch_kernel(pad_lo, pad_hi, dest_ref, h_ref, xs_hbm, zero_s, sem, zsem):
    def issue(g, c):
        for j in range(DMA_UNROLL):
            t = g * DMA_UNROLL + j
            for kk in range(TOP_K):
                _tile_copy(h_ref, t, xs_hbm, dest_ref[t * TOP_K + kk], sem).start(priority=kk % 2)
        return c
    lax.fori_loop(0, TM // DMA_UNROLL, issue, 0)

    @pl.when(pl.program_id(0) == 0)
    def _():
        zero_s[...] = jnp.zeros_like(zero_s)
        group_rows = ZERO_GROUP * ROW_TILE

        def single(s):
            return _tile_copy(zero_s, 0, xs_hbm, s, zsem)

        def group(g):
            return pltpu.make_async_copy(
                zero_s, xs_hbm.at[pl.ds(pl.multiple_of(g * group_rows, group_rows), group_rows)], zsem)

        def per_expert(e, c):
            lo, hi = pad_lo[e], pad_hi[e]
            mid = jnp.minimum(hi, (lo + ZERO_GROUP - 1) // ZERO_GROUP * ZERO_GROUP)
            g_lo, g_hi = mid // ZERO_GROUP, hi // ZERO_GROUP
            lax.fori_loop(lo, mid, lambda s, c2: (single(s).start(), c2)[1], 0)
            lax.fori_loop(g_lo, g_hi, lambda g, c2: (group(g).start(), c2)[1], 0)
            lax.fori_loop(lo, mid, lambda s, c2: (single(s).wait(), c2)[1], 0)
            lax.fori_loop(g_lo, g_hi, lambda g, c2: (group(g).wait(), c2)[1], 0)
            return c
        lax.fori_loop(0, pad_lo.shape[0], per_expert, 0)

    for _ in range(TOP_K):
        pltpu.make_async_copy(h_ref, xs_hbm.at[pl.ds(0, TM * ROW_TILE)], sem).wait()


def _dispatch(h3, dest, pad_lo, pad_hi, n_slots):
    t = h3.shape[0] // ROW_TILE
    return pl.pallas_call(
        _dispatch_kernel,
        grid_spec=pltpu.PrefetchScalarGridSpec(
            num_scalar_prefetch=2, grid=(t // TM,),
            in_specs=[pl.BlockSpec((TM * TOP_K,), lambda i, lo, hi: (i,), memory_space=pltpu.SMEM),
                      pl.BlockSpec((TM * ROW_TILE, LANES), lambda i, lo, hi: (i, 0))],
            out_specs=pl.BlockSpec(memory_space=pl.ANY),
            scratch_shapes=[pltpu.VMEM((ZERO_GROUP * ROW_TILE, LANES), F32),
                            pltpu.SemaphoreType.DMA, pltpu.SemaphoreType.DMA]),
        out_shape=jax.ShapeDtypeStruct((n_slots * ROW_TILE, LANES), F32),
        compiler_params=_params("arbitrary"),
        name="moe_dispatch",
    )(pad_lo, pad_hi, dest, h3)


def _expert_kernel(blk0, nblk, xs_hbm, wg_ref, bg_ref, wu_ref, bu_ref, wd_ref, bd_ref,
                   y_hbm, wg_s, wu_s, wd_s, xbuf, ybuf, in_sem, out_sem):
    e = pl.program_id(0)
    first = blk0[e]
    n = nblk[e]
    rows = MOE_BM * ROW_TILE

    def in_copy(b, slot):
        return pltpu.make_async_copy(
            xs_hbm.at[pl.ds(pl.multiple_of((first + b) * rows, rows), rows)], xbuf.at[slot], in_sem.at[slot])

    def out_copy(b, slot):
        return pltpu.make_async_copy(
            ybuf.at[slot], y_hbm.at[pl.ds(pl.multiple_of((first + b) * rows, rows), rows)], out_sem.at[slot])

    @pl.when(n > 0)
    def _():
        in_copy(0, 0).start()
        wg_s[...] = wg_ref[...].astype(BF16)
        wu_s[...] = wu_ref[...].astype(BF16)
        wd_s[...] = wd_ref[...].astype(BF16)

    def block(b, c):
        slot = b % 2
        in_copy(b, slot).wait()

        @pl.when(b + 1 < n)
        def _():
            in_copy(b + 1, 1 - slot).start()

        @pl.when(b >= 2)
        def _():
            out_copy(b - 2, slot).wait()

        x = _from_token_tiles(xbuf.at[slot], MOE_BM).astype(BF16)
        hg = jnp.minimum(_dot(x, wg_s[...]) + bg_ref[...], SWIGLU_LIMIT)
        hu = jnp.clip(_dot(x, wu_s[...]) + bu_ref[...], -SWIGLU_LIMIT, SWIGLU_LIMIT)
        hh = (hu + 1.0) * hg * _sigmoid(SWIGLU_ALPHA * hg)
        _to_token_tiles(ybuf.at[slot], _dot(hh.astype(BF16), wd_s[...]) + bd_ref[...])
        out_copy(b, slot).start()
        return c
    lax.fori_loop(0, n, block, 0)

    @pl.when(n >= 2)
    def _():
        out_copy(n - 2, n % 2).wait()

    @pl.when(n >= 1)
    def _():
        out_copy(n - 1, (n - 1) % 2).wait()


def _experts(xs, blk0, nblk, layer, w_gate, b_gate, w_up, b_up, w_down, b_down):
    depth, n_exp, d, f = w_gate.shape

    def w_spec(k, n):
        return pl.BlockSpec((None, None, k, n), lambda e, b0, nb: (layer, e, 0, 0))

    rows = MOE_BM * ROW_TILE
    return pl.pallas_call(
        _expert_kernel,
        grid_spec=pltpu.PrefetchScalarGridSpec(
            num_scalar_prefetch=2, grid=(n_exp,),
            in_specs=[pl.BlockSpec(memory_space=pl.ANY),
                      w_spec(d, f), w_spec(1, f), w_spec(d, f), w_spec(1, f), w_spec(f, d), w_spec(1, d)],
            out_specs=pl.BlockSpec(memory_space=pl.ANY),
            scratch_shapes=[pltpu.VMEM((d, f), BF16), pltpu.VMEM((d, f), BF16), pltpu.VMEM((f, d), BF16),
                            pltpu.VMEM((2, rows, LANES), F32), pltpu.VMEM((2, rows, LANES), F32),
                            pltpu.SemaphoreType.DMA((2,)), pltpu.SemaphoreType.DMA((2,))]),
        out_shape=jax.ShapeDtypeStruct(xs.shape, F32),
        compiler_params=_params("arbitrary"),
        name="moe_experts",
    )(blk0, nblk, xs, w_gate, b_gate.reshape(depth, n_exp, 1, f), w_up, b_up.reshape(depth, n_exp, 1, f),
      w_down, b_down.reshape(depth, n_exp, 1, d))


def _combine_kernel(grp, dest_ref, dest_next_ref, x_ref, gate_ref, mod_ref, gpost_ref, y_hbm,
                    *rest, split):
    del grp
    outs, (ybuf, sem, acc_s) = rest[:-3], rest[-3:]
    i = pl.program_id(0)
    n = pl.num_programs(0)
    rows = TM * ROW_TILE

    def issue(dref, slot):
        def body(g, c):
            for j in range(DMA_UNROLL):
                t = g * DMA_UNROLL + j
                for kk in range(TOP_K):
                    _tile_copy(y_hbm, dref[t * TOP_K + kk], ybuf.at[slot, kk], t, sem.at[slot]).start(
                        priority=kk % 2)
            return c
        lax.fori_loop(0, TM // DMA_UNROLL, body, 0)

    @pl.when(i == 0)
    def _():
        issue(dest_ref, 0)

    @pl.when(i + 1 < n)
    def _():
        issue(dest_next_ref, (i + 1) % 2)

    slot = i % 2

    for kk in range(TOP_K):
        pltpu.make_async_copy(y_hbm.at[pl.ds(0, rows)], ybuf.at[slot, kk], sem.at[slot]).wait()

    gates = gate_ref[...]
    for s in range(ROW_TILE):
        sl = slice(s * LANES, (s + 1) * LANES)
        acc = gates[:, 0:1] * ybuf[slot, 0, pl.ds(s, TM, stride=ROW_TILE), :]
        for kk in range(1, TOP_K):
            acc = acc + gates[:, kk:kk + 1] * ybuf[slot, kk, pl.ds(s, TM, stride=ROW_TILE), :]
        acc_s[:, sl] = acc
    result = x_ref[...] + mod_ref[5:6] * _rms(acc_s[...], gpost_ref[...])
    if split is None:
        outs[0][...] = result
    else:
        @pl.when(i < split)
        def _():
            outs[0][...] = result

        @pl.when(i >= split)
        def _():
            outs[1][...] = result


def _combine(x, y, dest, gates, mod, grp, g_post, split):
    t, d = x.shape
    n_tiles = t // TM
    tile = pl.BlockSpec((TM, d), lambda i, g: (i, 0))
    if split is None:
        out_specs, out_shape, aliases = tile, jax.ShapeDtypeStruct((t, d), F32), {3: 0}
    else:
        out_specs = [pl.BlockSpec((TM, d), lambda i, g: (jnp.minimum(i, split - 1), 0)),
                     pl.BlockSpec((TM, d), lambda i, g: (jnp.maximum(i - split, 0), 0))]
        out_shape = [jax.ShapeDtypeStruct((split * TM, d), F32),
                     jax.ShapeDtypeStruct((t - split * TM, d), F32)]
        aliases = {}
    return pl.pallas_call(
        functools.partial(_combine_kernel, split=split),
        grid_spec=pltpu.PrefetchScalarGridSpec(
            num_scalar_prefetch=1, grid=(n_tiles,),
            in_specs=[pl.BlockSpec((TM * TOP_K,), lambda i, g: (i,), memory_space=pltpu.SMEM),
                      pl.BlockSpec((TM * TOP_K,), lambda i, g: (jnp.minimum(i + 1, n_tiles - 1),),
                                   memory_space=pltpu.SMEM),
                      tile,
                      pl.BlockSpec((TM, LANES), lambda i, g: (i, 0)),
                      pl.BlockSpec((None, 6, d), lambda i, g: (g[i], 0, 0)),
                      pl.BlockSpec((1, d), lambda i, g: (0, 0)),
                      pl.BlockSpec(memory_space=pl.ANY)],
            out_specs=out_specs,
            scratch_shapes=[pltpu.VMEM((2, TOP_K, TM * ROW_TILE, LANES), F32),
                            pltpu.SemaphoreType.DMA((2,)), pltpu.VMEM((TM, d), F32)]),
        out_shape=out_shape,
        input_output_aliases=aliases,
        compiler_params=_params("arbitrary"),
        name="moe_combine",
    )(grp, dest, dest, x, gates, mod, g_post.reshape(1, d), y)


def _moe_layer(x, mod, grp, grp_mm, layer, g_pre, g_post, w_router, b_router,
               w_gate, b_gate, w_up, b_up, w_down, b_down, split=None):
    t, d = x.shape
    n_exp = w_router.shape[1]
    assert d == ROW_TILE * LANES
    h3, route, gates, cnt = _router(x, mod, grp_mm, g_pre, w_router, b_router)
    idx = route[:, 0:TOP_K]
    rank = route[:, TOP_K:2 * TOP_K]
    counts = cnt[0, :n_exp].astype(I32)
    padded = (counts + MOE_BM - 1) // MOE_BM * MOE_BM
    pad_end = jnp.cumsum(padded)
    pad_start = pad_end - padded
    experts = jnp.arange(n_exp, dtype=I32)
    dest = (rank + jnp.sum(jnp.where(idx[..., None] == experts, pad_start, 0), axis=-1)).reshape(-1)
    n_blocks = t * TOP_K // MOE_BM + n_exp
    xs = _dispatch(h3, dest, pad_start + counts, pad_end, n_blocks * MOE_BM)
    y = _experts(xs, pad_start // MOE_BM, padded // MOE_BM, layer,
                 w_gate, b_gate, w_up, b_up, w_down, b_down)
    return _combine(x, y, dest, gates, mod, grp, g_post, split)


def kernel(x_prompt, x_sample, state_hgrn, cache_k, cache_v, c, c_ctx, w_mod, b_mod, g_pre_mix, g_post_mix, g_pre_ffn, g_post_ffn, hgrn_w_in, hgrn_gamma, hgrn_g_norm, hgrn_w_out, attn_w_qkv, attn_q_gain, attn_k_gain, attn_w_o, moe_w_router, moe_b_router, moe_w_gate, moe_b_gate, moe_w_up, moe_b_up, moe_w_down, moe_b_down):
    bp, tp, d = x_prompt.shape
    bs, ts, _ = x_sample.shape
    depth = w_mod.shape[0]
    n_heads = d // HGRN_DK
    nk = N_KV_HEADS * HEAD_DIM
    assert tp % TM == 0 and ts % MM_TILE == 0 and ts % GRID_W == 0 and d % LANES == 0
    n_ctx = bp * tp
    assert n_ctx % MM_TILE == 0
    ctx_tiles, lat_tiles = tp // TM, ts // TM
    grp = _tile_groups(bp * ctx_tiles, bs, lat_tiles)
    grp_mm = _tile_groups(n_ctx // MM_TILE, bs, ts // MM_TILE)

    rows = -(-(1 + bs) // 8) * 8
    cvec = jnp.zeros((rows, d), F32).at[0].set(c_ctx).at[1:1 + bs].set(c)
    mod_all = _mod_table(cvec, w_mod, b_mod).reshape(depth, rows, 6, d)

    lb_all = jnp.cumsum(jax.nn.softmax(hgrn_gamma.astype(F32), axis=0), axis=0)

    new_states, new_k, new_v = [], [], []
    x = None
    x_parts = (x_prompt, x_sample)
    for l in range(depth):
        mod = mod_all[l]
        if l % 2 == 0:
            a = l // 2
            if x is not None:
                x_parts = (x[:n_ctx].reshape(bp, tp, d), x[n_ctx:].reshape(bs, ts, d))
            hgrn = functools.partial(_hgrn_layer, g_pre=g_pre_mix[l], g_post=g_post_mix[l], w_in=hgrn_w_in[a],
                                     lb=lb_all[l], g_norm=hgrn_g_norm[a], w_out=hgrn_w_out[a])
            assert (bp * ctx_tiles) % (lat_tiles * SEQ_PAR) == 0
            ctx_groups = bp * ctx_tiles // lat_tiles
            stream = jax.ShapeDtypeStruct((ctx_groups + bs, lat_tiles, TM, d), F32)
            stream, s_f, s_b = hgrn(x_parts[0], jnp.broadcast_to(mod[0], (bp, 6, d)),
                                    s0=jnp.zeros((bp, 2, n_heads, HGRN_DK, HGRN_DK), F32),
                                    stream=stream, group0=0)
            stream, _, _ = hgrn(x_parts[1], mod[1:1 + bs], s0=state_hgrn[:, a].astype(F32),
                                stream=stream, group0=ctx_groups)
            x = stream.reshape(n_ctx + bs * ts, d)
            new_states.append(jnp.stack([s_f, s_b], axis=1))
        else:
            m = l // 2
            w_qkv = attn_w_qkv[m].astype(BF16)
            q_c, k_c, v_c = _qkv(x, mod, grp, g_pre_mix[l], w_qkv, attn_q_gain[m], attn_k_gain[m],
                                 0, n_ctx // TM, 0, F32)
            q_l, k_l, v_l = _qkv(x, mod, grp, g_pre_mix[l], w_qkv, attn_q_gain[m], attn_k_gain[m],
                                 n_ctx // TM, bs * ts // TM, ts, BF16)
            o_c = _attention(q_c, k_c, v_c, bp, tp)
            past = cache_k.shape[2]
            o_l = _attention(q_l, k_l, v_l, bs, ts,
                             cache=(cache_k[:, m].reshape(bs, past, nk), cache_v[:, m].reshape(bs, past, nk)))
            new_k.append(k_c.reshape(bp, tp, N_KV_HEADS, HEAD_DIM))
            new_v.append(v_c.reshape(bp, tp, N_KV_HEADS, HEAD_DIM))
            x = _attn_out(x, o_c, o_l, mod, grp_mm, attn_w_o[m].astype(BF16), g_post_mix[l])
        x = _moe_layer(x, mod, grp, grp_mm, l, g_pre_ffn[l], g_post_ffn[l], moe_w_router[l], moe_b_router[l],
                       moe_w_gate, moe_b_gate, moe_w_up, moe_b_up, moe_w_down, moe_b_down,
                       split=bp * ctx_tiles if l == depth - 1 else None)

    y_prompt, y_sample = x
    return (y_prompt.reshape(bp, tp, d), y_sample.reshape(bs, ts, d), jnp.stack(new_states, axis=1),
            jnp.stack(new_k, axis=1), jnp.stack(new_v, axis=1))
```

```python
import functools

import numpy as np
import jax
import jax.numpy as jnp
from jax import lax
from jax.experimental import pallas as pl
from jax.experimental.pallas import tpu as pltpu

F32 = jnp.float32
BF16 = jnp.bfloat16
I32 = jnp.int32

EPS = 1e-6
GRID_W = 64
ROPE_THETA = 10000.0
HGRN_DK = 128
HEAD_DIM = 128
N_KV_HEADS = 2
TOP_K = 4
SWIGLU_LIMIT = 7.0
SWIGLU_ALPHA = 1.702

LANES = 128
TM = 256
MM_TILE = 1024
ATT_TQ = 1024
RANK_BLOCK = 256
GLA_CHUNK = 64
GLA_GROUP = 4
SEQ_PAR = 2
MOE_BM = 512
FF_CHUNK = 512
ROW_TILE = 8
DMA_UNROLL = 8
ZERO_GROUP = 16
MOD_COLS = 512
VMEM_LIMIT = 56 * 1024 * 1024
NEG_BIG = -1e30

NT_DIMS = (((1,), (1,)), ((), ()))
TN_DIMS = (((0,), (0,)), ((), ()))


def _params(*sem):
    return pltpu.CompilerParams(dimension_semantics=sem, vmem_limit_bytes=VMEM_LIMIT)


def _sigmoid(x):
    return 1.0 / (1.0 + jnp.exp(-x))


def _silu(x):
    return x * _sigmoid(x)


def _rms(x, gain):
    ms = jnp.mean(x * x, axis=-1, keepdims=True)
    return x * lax.rsqrt(ms + EPS) * gain


def _dot(a, b):
    return jnp.dot(a, b, preferred_element_type=F32)


def _mod_kernel(c_ref, w_ref, b_ref, o_ref):
    s = _silu(c_ref[...]).astype(BF16)
    o_ref[...] = _dot(s, w_ref[...].astype(BF16)) + b_ref[...]


def _mod_table(cvec, w_mod, b_mod):
    depth, d, n = w_mod.shape
    rows = cvec.shape[0]
    return pl.pallas_call(
        _mod_kernel,
        grid=(depth, n // MOD_COLS),
        in_specs=[
            pl.BlockSpec((rows, d), lambda l, j: (0, 0)),
            pl.BlockSpec((None, d, MOD_COLS), lambda l, j: (l, 0, j)),
            pl.BlockSpec((None, 1, MOD_COLS), lambda l, j: (l, 0, j)),
        ],
        out_specs=pl.BlockSpec((None, rows, MOD_COLS), lambda l, j: (l, 0, j)),
        out_shape=jax.ShapeDtypeStruct((depth, rows, n), F32),
        compiler_params=_params("arbitrary", "arbitrary"),
        name="mod_table",
    )(cvec, w_mod, b_mod.reshape(depth, 1, n))


def _gla_tiles(q_s, k_s, lf_s, v_s, st_ref, o_s, direction):
    c = GLA_CHUNK
    pair = GLA_GROUP * c
    d_model = q_s.shape[1]
    n_heads = d_model // HGRN_DK
    ri = lax.broadcasted_iota(I32, (pair, pair), 0)
    ci = lax.broadcasted_iota(I32, (pair, pair), 1)
    causal = (ci <= ri) if direction == 0 else (ci >= ri)
    shift = c.bit_length() - 1
    assert c == 1 << shift
    keep = jnp.logical_and(causal, (ri >> shift) == (ci >> shift))
    ones_tri = jnp.where(keep, 1.0, 0.0).astype(BF16)
    pairs = list(range(TM // pair))
    halves = list(range(GLA_GROUP))
    if direction == 1:
        pairs.reverse()
        halves.reverse()
    for pr, seq in [(pr, seq) for pr in pairs for seq in range(SEQ_PAR)]:
        row0 = seq * TM + pr * pair
        rows = pl.ds(row0, pair)
        lf = lf_s[rows, :]
        lf_hi = lf.astype(BF16)
        lf_lo = (lf - lf_hi.astype(F32)).astype(BF16)
        cum2 = _dot(ones_tri, lf_hi) + _dot(ones_tri, lf_lo)
        q_dec, k_inv, q_in, k_end, e_tot = [], [], [], [], []
        for hf in range(GLA_GROUP):
            cum = cum2[hf * c:(hf + 1) * c]
            part = pl.ds(row0 + hf * c, c)
            if direction == 0:
                btot, bmid = cum[c - 1:c], cum[c // 2 - 1:c // 2]
            else:
                btot, bmid = cum[0:1], cum[c // 2:c // 2 + 1]
            qd = q_s[part, :] * jnp.exp(cum - bmid)
            ki = k_s[part, :] * jnp.exp(bmid - cum)
            q_in.append((qd * jnp.exp(bmid)).astype(BF16))
            k_end.append((ki * jnp.exp(btot - bmid)).astype(BF16))
            q_dec.append(qd.astype(BF16))
            k_inv.append(ki.astype(BF16))
            e_tot.append(jnp.exp(btot))
        q_dec = jnp.concatenate(q_dec, axis=0)
        k_inv = jnp.concatenate(k_inv, axis=0)
        vc = v_s[rows, :].astype(BF16)
        for h in range(n_heads):
            sl = slice(h * HGRN_DK, (h + 1) * HGRN_DK)
            a = lax.dot_general(q_dec[:, sl], k_inv[:, sl], NT_DIMS, preferred_element_type=F32)
            a = jnp.where(keep, a, 0.0).astype(BF16)
            o_intra = _dot(a, vc[:, sl])
            for hf in halves:
                part = slice(hf * c, (hf + 1) * c)
                st = st_ref[seq, h]
                o_s[pl.ds(row0 + hf * c, c), sl] = o_intra[part] + lax.dot_general(
                    q_in[hf][:, sl], st.astype(BF16), NT_DIMS, preferred_element_type=F32)
                st_ref[seq, h] = st * e_tot[hf][:, sl] + lax.dot_general(
                    vc[part, sl], k_end[hf][:, sl], TN_DIMS, preferred_element_type=F32)


def _hgrn_gates(h, w_ref, lb, q_s, k_s, lf_s, v_s):
    d = h.shape[1]
    q_s[...] = _silu(_dot(h, w_ref[:, 0:d]))
    f = lb + (1.0 - lb) * _sigmoid(_dot(h, w_ref[:, d:2 * d]))
    lf_s[...] = jnp.log(f)
    k_s[...] = 1.0 - f
    v_s[...] = _dot(h, w_ref[:, 2 * d:3 * d])


def _load_state(s0_ref, st_s):
    @pl.when(pl.program_id(1) == 0)
    def _():
        for s in range(st_s.shape[0]):
            for h in range(st_s.shape[1]):
                st_s[s, h] = s0_ref[s, h].T


def _store_state(sfin_ref, st_s):
    @pl.when(pl.program_id(1) == pl.num_programs(1) - 1)
    def _():
        for s in range(st_s.shape[0]):
            for h in range(st_s.shape[1]):
                sfin_ref[s, h] = st_s[s, h].T


def _hgrn_norm(x_ref, mod_ref, gpre_ref):
    hs = [(_rms(x_ref[s], gpre_ref[...]) * (1.0 + mod_ref[s, 1:2]) + mod_ref[s, 0:1]).astype(BF16)
          for s in range(SEQ_PAR)]
    return jnp.concatenate(hs, axis=0)


def _hgrn_bwd_kernel(x_ref, mod_ref, gpre_ref, w_ref, lb_ref, s0_ref,
                     ob_ref, qb_ref, vb_ref, sfin_ref, st_s, o_s, q_s, k_s, lf_s, v_s):
    _load_state(s0_ref, st_s)
    h = _hgrn_norm(x_ref, mod_ref, gpre_ref)
    _hgrn_gates(h, w_ref, lb_ref[...], q_s, k_s, lf_s, v_s)
    for s in range(SEQ_PAR):
        rows = slice(s * TM, (s + 1) * TM)
        qb_ref[s] = q_s[rows, :].astype(BF16)
        vb_ref[s] = v_s[rows, :].astype(BF16)
    _gla_tiles(q_s, k_s, lf_s, v_s, st_s, o_s, 1)
    for s in range(SEQ_PAR):
        ob_ref[s] = o_s[s * TM:(s + 1) * TM, :].astype(BF16)
    _store_state(sfin_ref, st_s)


def _hgrn_fwd_kernel(x_ref, mod_ref, gpre_ref, w_ref, lb_ref, s0_ref, ob_ref, qb_ref, vb_ref,
                     gnorm_ref, wout_ref, gpost_ref, *rest):
    xo_ref, sfin_ref, st_s, o_s, q_s, k_s, lf_s, v_s = rest[-8:]
    _load_state(s0_ref, st_s)
    d = x_ref.shape[2]
    h = _hgrn_norm(x_ref, mod_ref, gpre_ref)
    lb = lb_ref[...]
    f = lb + (1.0 - lb) * _sigmoid(_dot(h, w_ref[:, 0:d]))
    lf_s[...] = jnp.log(f)
    k_s[...] = 1.0 - f
    for s in range(SEQ_PAR):
        rows = slice(s * TM, (s + 1) * TM)
        q_s[rows, :] = qb_ref[s].astype(F32)
        v_s[rows, :] = vb_ref[s].astype(F32)
    _gla_tiles(q_s, k_s, lf_s, v_s, st_s, o_s, 0)
    _store_state(sfin_ref, st_s)
    gate = _silu(_dot(h, w_ref[:, d:2 * d]))
    gnorm = gnorm_ref[...]
    o_b = jnp.concatenate([ob_ref[s] for s in range(SEQ_PAR)], axis=0).astype(F32)
    for hh in range(d // HGRN_DK):
        sl = slice(hh * HGRN_DK, (hh + 1) * HGRN_DK)
        o_s[:, sl] = _rms(o_s[:, sl] + o_b[:, sl], gnorm) * gate[:, sl]
    out = _dot(o_s[...].astype(BF16), wout_ref[...])
    for s in range(SEQ_PAR):
        xo_ref[s] = x_ref[s] + mod_ref[s, 2:3] * _rms(out[s * TM:(s + 1) * TM], gpost_ref[...])


def _tile_groups(n_ctx_tiles, n_lat_seq, lat_tiles):
    g = [0] * n_ctx_tiles
    for b in range(n_lat_seq):
        g += [1 + b] * lat_tiles
    return jnp.asarray(np.array(g, dtype=np.int32))


def _hgrn_layer(x, mod_seq, g_pre, g_post, w_in, lb, g_norm, w_out, s0, stream, group0):
    n_seq, seq_len, d = x.shape
    n_heads = d // HGRN_DK
    tiles = seq_len // TM
    per_group = stream.shape[1]
    assert n_seq % SEQ_PAR == 0 and per_group % SEQ_PAR == 0 and tiles in (1, per_group)
    x4 = x.reshape(n_seq, tiles, TM, d)
    w_bwd = jnp.concatenate([w_in[:, 0:d], w_in[:, 2 * d:3 * d], w_in[:, 3 * d:4 * d]], axis=1).astype(BF16)
    w_fwd = jnp.concatenate([w_in[:, d:2 * d], w_in[:, 4 * d:5 * d]], axis=1).astype(BF16)

    def specs(direction):
        def tile_spec():
            if direction == 0:
                return pl.BlockSpec((SEQ_PAR, None, TM, d), lambda p, j: (p, j, 0, 0))
            return pl.BlockSpec((SEQ_PAR, None, TM, d), lambda p, j: (p, tiles - 1 - j, 0, 0))
        state = pl.BlockSpec((SEQ_PAR, n_heads, HGRN_DK, HGRN_DK), lambda p, j: (p, 0, 0, 0))
        mods = pl.BlockSpec((SEQ_PAR, 6, d), lambda p, j: (p, 0, 0))
        vec = pl.BlockSpec((1, d), lambda p, j: (0, 0))
        return tile_spec, state, mods, vec

    def const(shape):
        return pl.BlockSpec(shape, lambda p, j: (0, 0))

    rows = SEQ_PAR * TM
    scratch = ([pltpu.VMEM((SEQ_PAR, n_heads, HGRN_DK, HGRN_DK), F32)]
               + [pltpu.VMEM((rows, d), F32)] * 5)
    grid = (n_seq // SEQ_PAR, tiles)
    state_shape = jax.ShapeDtypeStruct((n_seq, n_heads, HGRN_DK, HGRN_DK), F32)

    tile_spec, state, mods, vec = specs(1)
    o_bwd, q_act, v_act, s_bwd = pl.pallas_call(
        _hgrn_bwd_kernel,
        grid=grid,
        in_specs=[tile_spec(), mods, vec, const((d, 3 * d)), vec, state],
        out_specs=[tile_spec(), tile_spec(), tile_spec(), state],
        scratch_shapes=scratch,
        out_shape=[jax.ShapeDtypeStruct(x4.shape, BF16)] * 3 + [state_shape],
        compiler_params=_params("arbitrary", "arbitrary"),
        name="hgrn_bwd",
    )(x4, mod_seq, g_pre.reshape(1, d), w_bwd, lb[1].reshape(1, d), s0[:, 1])

    tile_spec, state, mods, vec = specs(0)
    if tiles == per_group:
        stream_spec = pl.BlockSpec((SEQ_PAR, None, TM, d), lambda p, j: (group0 // SEQ_PAR + p, j, 0, 0))
    else:
        per = per_group // SEQ_PAR
        stream_spec = pl.BlockSpec((None, SEQ_PAR, TM, d), lambda p, j: (group0 + p // per, p % per, 0, 0))
    in_specs = [tile_spec(), mods, vec, const((d, 2 * d)), vec, state, tile_spec(), tile_spec(), tile_spec(),
                const((1, HGRN_DK)), const((d, d)), vec]
    args = [x4, mod_seq, g_pre.reshape(1, d), w_fwd, lb[0].reshape(1, d), s0[:, 0],
            o_bwd, q_act, v_act, g_norm.reshape(1, HGRN_DK), w_out.astype(BF16), g_post.reshape(1, d)]
    aliases = {}
    if not isinstance(stream, jax.ShapeDtypeStruct):
        aliases = {len(args): 0}
        in_specs.append(pl.BlockSpec(memory_space=pl.ANY))
        args.append(stream)
    stream, s_fwd = pl.pallas_call(
        _hgrn_fwd_kernel,
        grid=grid,
        in_specs=in_specs,
        out_specs=[stream_spec, state],
        scratch_shapes=scratch,
        out_shape=[jax.ShapeDtypeStruct(stream.shape, F32), state_shape],
        input_output_aliases=aliases,
        compiler_params=_params("arbitrary", "arbitrary"),
        name="hgrn_fwd",
    )(*args)
    return stream, s_fwd, s_bwd


def _rope_tables(n_tok):
    axis_dim = HEAD_DIM // 2
    half = axis_dim // 2
    pos = np.arange(n_tok)
    row, col = pos // GRID_W, pos % GRID_W
    inv_freq = ROPE_THETA ** (-np.arange(0, axis_dim, 2, dtype=np.float64) / axis_dim)
    lane = np.arange(HEAD_DIM)
    p = np.where(lane[None, :] < axis_dim, row[:, None], col[:, None]).astype(np.float64)
    ang = p * inv_freq[(lane % axis_dim) % half][None, :]
    sign = np.where((lane % axis_dim) < half, -1.0, 1.0)[None, :]
    return jnp.asarray(np.cos(ang), F32), jnp.asarray(np.sin(ang) * sign, F32)


def _rope(xh, cos, sin_signed):
    axis_dim = HEAD_DIM // 2
    half = axis_dim // 2
    lane = lax.broadcasted_iota(I32, xh.shape, 1)
    upper = pltpu.roll(xh, HEAD_DIM - half, 1)
    lower = pltpu.roll(xh, half, 1)
    partner = jnp.where((lane % axis_dim) < half, upper, lower)
    return xh * cos + partner * sin_signed


def _qkv_kernel(grp, x_ref, mod_ref, gpre_ref, w_ref, qg_ref, kg_ref, *rest, rope):
    del grp
    if rope:
        cos_ref, sin_ref, q_out, k_out, v_out = rest
    else:
        q_out, k_out, v_out = rest
    d = x_ref.shape[1]
    nk = N_KV_HEADS * HEAD_DIM
    h = (_rms(x_ref[...], gpre_ref[...]) * (1.0 + mod_ref[1:2]) + mod_ref[0:1]).astype(BF16)
    q = _dot(h, w_ref[:, 0:d])
    k = _dot(h, w_ref[:, d:d + nk])
    v_out[...] = _dot(h, w_ref[:, d + nk:d + 2 * nk]).astype(v_out.dtype)
    scale = HEAD_DIM ** -0.5
    for hh in range(d // HEAD_DIM):
        sl = slice(hh * HEAD_DIM, (hh + 1) * HEAD_DIM)
        qh = _rms(q[:, sl], qg_ref[...])
        if rope:
            qh = _rope(qh, cos_ref[...], sin_ref[...])
        q_out[:, sl] = (qh * scale).astype(q_out.dtype)
    for hh in range(N_KV_HEADS):
        sl = slice(hh * HEAD_DIM, (hh + 1) * HEAD_DIM)
        kh = _rms(k[:, sl], kg_ref[...])
        if rope:
            kh = _rope(kh, cos_ref[...], sin_ref[...])
        k_out[:, sl] = kh.astype(k_out.dtype)


def _qkv(x, mod, grp, g_pre, w_qkv, q_gain, k_gain, tile0, n_tiles, rope_len, kv_dtype):
    t, d = x.shape
    nk = N_KV_HEADS * HEAD_DIM
    n_out = n_tiles * TM
    tile_in = pl.BlockSpec((TM, d), lambda i, g: (tile0 + i, 0))
    in_specs = [tile_in,
                pl.BlockSpec((None, 6, d), lambda i, g: (g[tile0 + i], 0, 0)),
                pl.BlockSpec((1, d), lambda i, g: (0, 0)),
                pl.BlockSpec((d, d + 2 * nk), lambda i, g: (0, 0)),
                pl.BlockSpec((1, HEAD_DIM), lambda i, g: (0, 0)),
                pl.BlockSpec((1, HEAD_DIM), lambda i, g: (0, 0))]
    args = [grp, x, mod, g_pre.reshape(1, d), w_qkv, q_gain.reshape(1, HEAD_DIM), k_gain.reshape(1, HEAD_DIM)]
    if rope_len:
        per_seq = rope_len // TM
        cos, sin = _rope_tables(rope_len)
        in_specs += [pl.BlockSpec((TM, HEAD_DIM), lambda i, g: (i % per_seq, 0))] * 2
        args += [cos, sin]
    return pl.pallas_call(
        functools.partial(_qkv_kernel, rope=bool(rope_len)),
        grid_spec=pltpu.PrefetchScalarGridSpec(
            num_scalar_prefetch=1, grid=(n_tiles,), in_specs=in_specs,
            out_specs=[pl.BlockSpec((TM, d), lambda i, g: (i, 0)),
                       pl.BlockSpec((TM, nk), lambda i, g: (i, 0)),
                       pl.BlockSpec((TM, nk), lambda i, g: (i, 0))]),
        out_shape=[jax.ShapeDtypeStruct((n_out, d), BF16),
                   jax.ShapeDtypeStruct((n_out, nk), kv_dtype),
                   jax.ShapeDtypeStruct((n_out, nk), kv_dtype)],
        compiler_params=_params("arbitrary"),
        name="qkv_rope" if rope_len else "qkv",
    )(*args)


def _attn_kernel(q_ref, *rest, n_pieces, n_sub):
    kv, o_ref = rest[:-1], rest[-1]
    q_rows = q_ref.shape[0] // n_sub
    kv_rows = kv[-1].shape[0] // n_sub
    group = q_ref.shape[1] // HEAD_DIM
    for sub in range(n_sub):
        qr = slice(sub * q_rows, (sub + 1) * q_rows)
        if n_sub == 1:
            ks = [kv[2 * p][...].astype(BF16) for p in range(n_pieces)]
            vs = [kv[2 * p + 1][...].astype(BF16) for p in range(n_pieces)]
        else:
            kr = slice(sub * kv_rows, (sub + 1) * kv_rows)
            ks, vs = [kv[0][kr, :].astype(BF16)], [kv[1][kr, :].astype(BF16)]
        for g in range(group):
            sl = slice(g * HEAD_DIM, (g + 1) * HEAD_DIM)
            qh = q_ref[qr, sl]
            ss = [lax.dot_general(qh, kp, NT_DIMS, preferred_element_type=F32) for kp in ks]
            m = ss[0].max(axis=-1, keepdims=True)
            for s in ss[1:]:
                m = jnp.maximum(m, s.max(axis=-1, keepdims=True))
            ps = [jnp.exp(s - m) for s in ss]
            denom = ps[0].sum(axis=-1, keepdims=True)
            for p in ps[1:]:
                denom = denom + p.sum(axis=-1, keepdims=True)
            o = _dot(ps[0].astype(BF16), vs[0])
            for p, vp in zip(ps[1:], vs[1:]):
                o = o + _dot(p.astype(BF16), vp)
            o_ref[qr, sl] = (o / denom).astype(o_ref.dtype)


def _attention(q, k_new, v_new, n_seq, seq_len, cache=None):
    t, d = q.shape
    group_w = d // N_KV_HEADS
    n_sub = 1
    if cache is None and seq_len < ATT_TQ:
        n_sub = max(s for s in range(1, ATT_TQ // seq_len + 1) if n_seq % s == 0)
    if n_sub > 1:
        n_seq, seq_len = n_seq // n_sub, seq_len * n_sub
    tq = min(ATT_TQ, seq_len)
    nq = seq_len // tq
    q_spec = pl.BlockSpec((tq, group_w), lambda b, i, kv: (b * nq + i, kv))
    new_spec = pl.BlockSpec((seq_len, HEAD_DIM), lambda b, i, kv: (b, kv))
    in_specs, args = [q_spec], [q]
    if cache is not None:
        past = cache[0].shape[1]
        c_spec = pl.BlockSpec((None, past, HEAD_DIM), lambda b, i, kv: (b, 0, kv))
        in_specs += [c_spec, c_spec]
        args += list(cache)
    in_specs += [new_spec, new_spec]
    args += [k_new, v_new]
    return pl.pallas_call(
        functools.partial(_attn_kernel, n_pieces=len(args) // 2, n_sub=n_sub),
        grid=(n_seq, nq, N_KV_HEADS),
        in_specs=in_specs,
        out_specs=q_spec,
        out_shape=jax.ShapeDtypeStruct((t, d), BF16),
        compiler_params=_params("arbitrary", "arbitrary", "arbitrary"),
        name="attention",
    )(*args)


def _attn_out_kernel(grp, x_ref, oc_ref, ol_ref, mod_ref, w_ref, gpost_ref, xo_ref, *, n_ctx_tiles):
    del grp
    o = jnp.where(pl.program_id(0) < n_ctx_tiles, oc_ref[...], ol_ref[...])
    out = _dot(o, w_ref[...])
    xo_ref[...] = x_ref[...] + mod_ref[2:3] * _rms(out, gpost_ref[...])


def _attn_out(x, o_ctx, o_lat, mod, grp, w_o, g_post):
    t, d = x.shape
    nct = o_ctx.shape[0] // MM_TILE
    tile = pl.BlockSpec((MM_TILE, d), lambda i, g: (i, 0))
    return pl.pallas_call(
        functools.partial(_attn_out_kernel, n_ctx_tiles=nct),
        grid_spec=pltpu.PrefetchScalarGridSpec(
            num_scalar_prefetch=1, grid=(t // MM_TILE,),
            in_specs=[tile,
                      pl.BlockSpec((MM_TILE, d), lambda i, g: (jnp.minimum(i, nct - 1), 0)),
                      pl.BlockSpec((MM_TILE, d), lambda i, g: (jnp.maximum(i - nct, 0), 0)),
                      pl.BlockSpec((None, 6, d), lambda i, g: (g[i], 0, 0)),
                      pl.BlockSpec((d, d), lambda i, g: (0, 0)),
                      pl.BlockSpec((1, d), lambda i, g: (0, 0))],
            out_specs=tile),
        out_shape=jax.ShapeDtypeStruct((t, d), F32),
        input_output_aliases={1: 0},
        compiler_params=_params("arbitrary"),
        name="attn_out",
    )(grp, x, o_ctx, o_lat, mod, w_o, g_post.reshape(1, d))


def _to_token_tiles(ref, x):
    rows = x.shape[0]
    for s in range(x.shape[1] // LANES):
        ref[pl.ds(s, rows, stride=ROW_TILE), :] = x[:, s * LANES:(s + 1) * LANES]


def _from_token_tiles(ref, rows):
    n = ROW_TILE
    return jnp.concatenate([ref[pl.ds(s, rows, stride=n), :] for s in range(n)], axis=1)


def _router_kernel(grp, x_ref, mod_ref, g_ref, wr_ref, br_ref,
                   h_ref, route_ref, gate_ref, cnt_ref, carry):
    del grp
    @pl.when(pl.program_id(0) == 0)
    def _():
        carry[...] = jnp.zeros_like(carry)
    h = _rms(x_ref[...], g_ref[...]) * (1.0 + mod_ref[4:5]) + mod_ref[3:4]
    _to_token_tiles(h_ref, h)
    logits = _dot(h.astype(BF16), wr_ref[...]) + br_ref[...]
    lane = lax.broadcasted_iota(I32, logits.shape, 1)
    lane_f = lane.astype(F32)
    vals, idxs = [], []
    work = logits
    for _ in range(TOP_K):
        m = work.max(axis=-1, keepdims=True)
        idx = jnp.where(work == m, lane_f, float(LANES)).min(axis=-1, keepdims=True).astype(I32)
        vals.append(m)
        idxs.append(idx)
        work = jnp.where(lane == idx, -jnp.inf, work)
    es = [jnp.exp(v - vals[0]) for v in vals]
    denom = es[0]
    for e in es[1:]:
        denom = denom + e
    onehot = jnp.zeros(logits.shape, F32)
    for idx in idxs:
        onehot = onehot + jnp.where(lane == idx, 1.0, 0.0)
    ri = lax.broadcasted_iota(I32, (RANK_BLOCK, RANK_BLOCK), 0)
    ci = lax.broadcasted_iota(I32, (RANK_BLOCK, RANK_BLOCK), 1)
    before = jnp.where(ci < ri, 1.0, 0.0).astype(BF16)
    counts = carry[0:1]
    ranks = []
    for r0 in range(0, MM_TILE, RANK_BLOCK):
        part = onehot[r0:r0 + RANK_BLOCK]
        ranks.append(_dot(before, part.astype(BF16)) + counts)
        counts = counts + part.sum(axis=0, keepdims=True)
    rank_all = jnp.concatenate(ranks, axis=0)
    route = jnp.zeros(logits.shape, I32)
    gate = jnp.zeros(logits.shape, F32)
    for kk in range(TOP_K):
        rank = jnp.where(lane == idxs[kk], rank_all, 0.0).sum(axis=-1, keepdims=True)
        route = jnp.where(lane == kk, idxs[kk], route)
        route = jnp.where(lane == TOP_K + kk, rank.astype(I32), route)
        gate = jnp.where(lane == kk, es[kk] / denom, gate)
    route_ref[...] = route
    gate_ref[...] = gate
    carry[...] = jnp.broadcast_to(counts, carry.shape)
    cnt_ref[...] = carry[...]


def _router(x, mod, grp, g_pre, w_router, b_router):
    t, d = x.shape
    n_exp = w_router.shape[1]
    wr = jnp.pad(w_router.astype(BF16), ((0, 0), (0, LANES - n_exp)))
    br = jnp.pad(b_router.reshape(1, n_exp), ((0, 0), (0, LANES - n_exp)), constant_values=NEG_BIG)
    tile = pl.BlockSpec((MM_TILE, d), lambda i, g: (i, 0))
    lane_tile = pl.BlockSpec((MM_TILE, LANES), lambda i, g: (i, 0))
    return pl.pallas_call(
        _router_kernel,
        grid_spec=pltpu.PrefetchScalarGridSpec(
            num_scalar_prefetch=1, grid=(t // MM_TILE,),
            in_specs=[tile,
                      pl.BlockSpec((None, 6, d), lambda i, g: (g[i], 0, 0)),
                      pl.BlockSpec((1, d), lambda i, g: (0, 0)),
                      pl.BlockSpec((d, LANES), lambda i, g: (0, 0)),
                      pl.BlockSpec((1, LANES), lambda i, g: (0, 0))],
            out_specs=[pl.BlockSpec((MM_TILE * ROW_TILE, LANES), lambda i, g: (i, 0)), lane_tile, lane_tile,
                       pl.BlockSpec((8, LANES), lambda i, g: (0, 0))],
            scratch_shapes=[pltpu.VMEM((8, LANES), F32)]),
        out_shape=[jax.ShapeDtypeStruct((t * ROW_TILE, LANES), F32),
                   jax.ShapeDtypeStruct((t, LANES), I32),
                   jax.ShapeDtypeStruct((t, LANES), F32),
                   jax.ShapeDtypeStruct((8, LANES), F32)],
        compiler_params=_params("arbitrary"),
        name="router",
    )(grp, x, mod, g_pre.reshape(1, d), wr, br)


def _tile_copy(src, src_row, dst, dst_row, sem):
    n = ROW_TILE
    return pltpu.make_async_copy(src.at[pl.ds(pl.multiple_of(src_row * n, n), n)],
                                 dst.at[pl.ds(pl.multiple_of(dst_row * n, n), n)], sem)


def _dispatch_kernel(pad_lo, pad_hi, dest_ref, h_ref, xs_hbm, zero_s, sem, zsem):
    def issue(g, c):
        for j in range(DMA_UNROLL):
            t = g * DMA_UNROLL + j
            for kk in range(TOP_K):
                _tile_copy(h_ref, t, xs_hbm, dest_ref[t * TOP_K + kk], sem).start(priority=kk % 2)
        return c
    lax.fori_loop(0, TM // DMA_UNROLL, issue, 0)

    @pl.when(pl.program_id(0) == 0)
    def _():
        zero_s[...] = jnp.zeros_like(zero_s)
        group_rows = ZERO_GROUP * ROW_TILE

        def single(s):
            return _tile_copy(zero_s, 0, xs_hbm, s, zsem)

        def group(g):
            return pltpu.make_async_copy(
                zero_s, xs_hbm.at[pl.ds(pl.multiple_of(g * group_rows, group_rows), group_rows)], zsem)

        def per_expert(e, c):
            lo, hi = pad_lo[e], pad_hi[e]
            mid = jnp.minimum(hi, (lo + ZERO_GROUP - 1) // ZERO_GROUP * ZERO_GROUP)
            g_lo, g_hi = mid // ZERO_GROUP, hi // ZERO_GROUP
            lax.fori_loop(lo, mid, lambda s, c2: (single(s).start(), c2)[1], 0)
            lax.fori_loop(g_lo, g_hi, lambda g, c2: (group(g).start(), c2)[1], 0)
            lax.fori_loop(lo, mid, lambda s, c2: (single(s).wait(), c2)[1], 0)
            lax.fori_loop(g_lo, g_hi, lambda g, c2: (group(g).wait(), c2)[1], 0)
            return c
        lax.fori_loop(0, pad_lo.shape[0], per_expert, 0)

    for _ in range(TOP_K):
        pltpu.make_async_copy(h_ref, xs_hbm.at[pl.ds(0, TM * ROW_TILE)], sem).wait()


def _dispatch(h3, dest, pad_lo, pad_hi, n_slots):
    t = h3.shape[0] // ROW_TILE
    return pl.pallas_call(
        _dispatch_kernel,
        grid_spec=pltpu.PrefetchScalarGridSpec(
            num_scalar_prefetch=2, grid=(t // TM,),
            in_specs=[pl.BlockSpec((TM * TOP_K,), lambda i, lo, hi: (i,), memory_space=pltpu.SMEM),
                      pl.BlockSpec((TM * ROW_TILE, LANES), lambda i, lo, hi: (i, 0))],
            out_specs=pl.BlockSpec(memory_space=pl.ANY),
            scratch_shapes=[pltpu.VMEM((ZERO_GROUP * ROW_TILE, LANES), F32),
                            pltpu.SemaphoreType.DMA, pltpu.SemaphoreType.DMA]),
        out_shape=jax.ShapeDtypeStruct((n_slots * ROW_TILE, LANES), F32),
        compiler_params=_params("arbitrary"),
        name="moe_dispatch",
    )(pad_lo, pad_hi, dest, h3)


def _expert_kernel(blk0, nblk, xs_hbm, wg_ref, bg_ref, wu_ref, bu_ref, wd_ref, bd_ref,
                   y_hbm, wg_s, wu_s, wd_s, xbuf, ybuf, in_sem, out_sem):
    e = pl.program_id(0)
    first = blk0[e]
    n = nblk[e]
    rows = MOE_BM * ROW_TILE

    def in_copy(b, slot):
        return pltpu.make_async_copy(
            xs_hbm.at[pl.ds(pl.multiple_of((first + b) * rows, rows), rows)], xbuf.at[slot], in_sem.at[slot])

    def out_copy(b, slot):
        return pltpu.make_async_copy(
            ybuf.at[slot], y_hbm.at[pl.ds(pl.multiple_of((first + b) * rows, rows), rows)], out_sem.at[slot])

    @pl.when(n > 0)
    def _():
        in_copy(0, 0).start()
        wg_s[...] = wg_ref[...].astype(BF16)
        wu_s[...] = wu_ref[...].astype(BF16)
        wd_s[...] = wd_ref[...].astype(BF16)

    def block(b, c):
        slot = b % 2
        in_copy(b, slot).wait()

        @pl.when(b + 1 < n)
        def _():
            in_copy(b + 1, 1 - slot).start()

        @pl.when(b >= 2)
        def _():
            out_copy(b - 2, slot).wait()

        x = _from_token_tiles(xbuf.at[slot], MOE_BM).astype(BF16)
        y = bd_ref[...]
        for c0 in range(0, wg_s.shape[1], FF_CHUNK):
            cols = slice(c0, c0 + FF_CHUNK)
            hg = jnp.minimum(_dot(x, wg_s[:, cols]) + bg_ref[:, cols], SWIGLU_LIMIT)
            hu = jnp.clip(_dot(x, wu_s[:, cols]) + bu_ref[:, cols], -SWIGLU_LIMIT, SWIGLU_LIMIT)
            hh = (hu + 1.0) * hg * _sigmoid(SWIGLU_ALPHA * hg)
            y = y + _dot(hh.astype(BF16), wd_s[cols, :])
        _to_token_tiles(ybuf.at[slot], y)
        out_copy(b, slot).start()
        return c
    lax.fori_loop(0, n, block, 0)

    @pl.when(n >= 2)
    def _():
        out_copy(n - 2, n % 2).wait()

    @pl.when(n >= 1)
    def _():
        out_copy(n - 1, (n - 1) % 2).wait()


def _experts(xs, blk0, nblk, layer, w_gate, b_gate, w_up, b_up, w_down, b_down):
    depth, n_exp, d, f = w_gate.shape

    def w_spec(k, n):
        return pl.BlockSpec((None, None, k, n), lambda e, b0, nb: (layer, e, 0, 0))

    rows = MOE_BM * ROW_TILE
    return pl.pallas_call(
        _expert_kernel,
        grid_spec=pltpu.PrefetchScalarGridSpec(
            num_scalar_prefetch=2, grid=(n_exp,),
            in_specs=[pl.BlockSpec(memory_space=pl.ANY),
                      w_spec(d, f), w_spec(1, f), w_spec(d, f), w_spec(1, f), w_spec(f, d), w_spec(1, d)],
            out_specs=pl.BlockSpec(memory_space=pl.ANY),
            scratch_shapes=[pltpu.VMEM((d, f), BF16), pltpu.VMEM((d, f), BF16), pltpu.VMEM((f, d), BF16),
                            pltpu.VMEM((2, rows, LANES), F32), pltpu.VMEM((2, rows, LANES), F32),
                            pltpu.SemaphoreType.DMA((2,)), pltpu.SemaphoreType.DMA((2,))]),
        out_shape=jax.ShapeDtypeStruct(xs.shape, F32),
        compiler_params=_params("arbitrary"),
        name="moe_experts",
    )(blk0, nblk, xs, w_gate, b_gate.reshape(depth, n_exp, 1, f), w_up, b_up.reshape(depth, n_exp, 1, f),
      w_down, b_down.reshape(depth, n_exp, 1, d))


def _combine_kernel(grp, dest_ref, dest_next_ref, x_ref, gate_ref, mod_ref, gpost_ref, y_hbm,
                    *rest, split):
    del grp
    outs, (ybuf, sem, acc_s) = rest[:-3], rest[-3:]
    i = pl.program_id(0)
    n = pl.num_programs(0)
    rows = TM * ROW_TILE

    def issue(dref, slot):
        def body(g, c):
            for j in range(DMA_UNROLL):
                t = g * DMA_UNROLL + j
                for kk in range(TOP_K):
                    _tile_copy(y_hbm, dref[t * TOP_K + kk], ybuf.at[slot, kk], t, sem.at[slot]).start(
                        priority=kk % 2)
            return c
        lax.fori_loop(0, TM // DMA_UNROLL, body, 0)

    @pl.when(i == 0)
    def _():
        issue(dest_ref, 0)

    @pl.when(i + 1 < n)
    def _():
        issue(dest_next_ref, (i + 1) % 2)

    slot = i % 2

    for kk in range(TOP_K):
        pltpu.make_async_copy(y_hbm.at[pl.ds(0, rows)], ybuf.at[slot, kk], sem.at[slot]).wait()

    gates = gate_ref[...]
    for s in range(ROW_TILE):
        sl = slice(s * LANES, (s + 1) * LANES)
        acc = gates[:, 0:1] * ybuf[slot, 0, pl.ds(s, TM, stride=ROW_TILE), :]
        for kk in range(1, TOP_K):
            acc = acc + gates[:, kk:kk + 1] * ybuf[slot, kk, pl.ds(s, TM, stride=ROW_TILE), :]
        acc_s[:, sl] = acc
    result = x_ref[...] + mod_ref[5:6] * _rms(acc_s[...], gpost_ref[...])
    if split is None:
        outs[0][...] = result
    else:
        @pl.when(i < split)
        def _():
            outs[0][...] = result

        @pl.when(i >= split)
        def _():
            outs[1][...] = result


def _combine(x, y, dest, gates, mod, grp, g_post, split):
    t, d = x.shape
    n_tiles = t // TM
    tile = pl.BlockSpec((TM, d), lambda i, g: (i, 0))
    if split is None:
        out_specs, out_shape, aliases = tile, jax.ShapeDtypeStruct((t, d), F32), {3: 0}
    else:
        out_specs = [pl.BlockSpec((TM, d), lambda i, g: (jnp.minimum(i, split - 1), 0)),
                     pl.BlockSpec((TM, d), lambda i, g: (jnp.maximum(i - split, 0), 0))]
        out_shape = [jax.ShapeDtypeStruct((split * TM, d), F32),
                     jax.ShapeDtypeStruct((t - split * TM, d), F32)]
        aliases = {}
    return pl.pallas_call(
        functools.partial(_combine_kernel, split=split),
        grid_spec=pltpu.PrefetchScalarGridSpec(
            num_scalar_prefetch=1, grid=(n_tiles,),
            in_specs=[pl.BlockSpec((TM * TOP_K,), lambda i, g: (i,), memory_space=pltpu.SMEM),
                      pl.BlockSpec((TM * TOP_K,), lambda i, g: (jnp.minimum(i + 1, n_tiles - 1),),
                                   memory_space=pltpu.SMEM),
                      tile,
                      pl.BlockSpec((TM, LANES), lambda i, g: (i, 0)),
                      pl.BlockSpec((None, 6, d), lambda i, g: (g[i], 0, 0)),
                      pl.BlockSpec((1, d), lambda i, g: (0, 0)),
                      pl.BlockSpec(memory_space=pl.ANY)],
            out_specs=out_specs,
            scratch_shapes=[pltpu.VMEM((2, TOP_K, TM * ROW_TILE, LANES), F32),
                            pltpu.SemaphoreType.DMA((2,)), pltpu.VMEM((TM, d), F32)]),
        out_shape=out_shape,
        input_output_aliases=aliases,
        compiler_params=_params("arbitrary"),
        name="moe_combine",
    )(grp, dest, dest, x, gates, mod, g_post.reshape(1, d), y)


def _moe_layer(x, mod, grp, grp_mm, layer, g_pre, g_post, w_router, b_router,
               w_gate, b_gate, w_up, b_up, w_down, b_down, split=None):
    t, d = x.shape
    n_exp = w_router.shape[1]
    assert d == ROW_TILE * LANES
    h3, route, gates, cnt = _router(x, mod, grp_mm, g_pre, w_router, b_router)
    idx = route[:, 0:TOP_K]
    rank = route[:, TOP_K:2 * TOP_K]
    counts = cnt[0, :n_exp].astype(I32)
    padded = (counts + MOE_BM - 1) // MOE_BM * MOE_BM
    pad_end = jnp.cumsum(padded)
    pad_start = pad_end - padded
    experts = jnp.arange(n_exp, dtype=I32)
    dest = (rank + jnp.sum(jnp.where(idx[..., None] == experts, pad_start, 0), axis=-1)).reshape(-1)
    n_blocks = t * TOP_K // MOE_BM + n_exp
    xs = _dispatch(h3, dest, pad_start + counts, pad_end, n_blocks * MOE_BM)
    y = _experts(xs, pad_start // MOE_BM, padded // MOE_BM, layer,
                 w_gate, b_gate, w_up, b_up, w_down, b_down)
    return _combine(x, y, dest, gates, mod, grp, g_post, split)


def kernel(x_prompt, x_sample, state_hgrn, cache_k, cache_v, c, c_ctx, w_mod, b_mod, g_pre_mix, g_post_mix, g_pre_ffn, g_post_ffn, hgrn_w_in, hgrn_gamma, hgrn_g_norm, hgrn_w_out, attn_w_qkv, attn_q_gain, attn_k_gain, attn_w_o, moe_w_router, moe_b_router, moe_w_gate, moe_b_gate, moe_w_up, moe_b_up, moe_w_down, moe_b_down):
    bp, tp, d = x_prompt.shape
    bs, ts, _ = x_sample.shape
    depth = w_mod.shape[0]
    n_heads = d // HGRN_DK
    nk = N_KV_HEADS * HEAD_DIM
    assert tp % TM == 0 and ts % MM_TILE == 0 and ts % GRID_W == 0 and d % LANES == 0
    n_ctx = bp * tp
    assert n_ctx % MM_TILE == 0
    ctx_tiles, lat_tiles = tp // TM, ts // TM
    grp = _tile_groups(bp * ctx_tiles, bs, lat_tiles)
    grp_mm = _tile_groups(n_ctx // MM_TILE, bs, ts // MM_TILE)

    rows = -(-(1 + bs) // 8) * 8
    cvec = jnp.zeros((rows, d), F32).at[0].set(c_ctx).at[1:1 + bs].set(c)
    mod_all = _mod_table(cvec, w_mod, b_mod).reshape(depth, rows, 6, d)

    lb_all = jnp.cumsum(jax.nn.softmax(hgrn_gamma.astype(F32), axis=0), axis=0)

    new_states, new_k, new_v = [], [], []
    x = None
    x_parts = (x_prompt, x_sample)
    for l in range(depth):
        mod = mod_all[l]
        if l % 2 == 0:
            a = l // 2
            if x is not None:
                x_parts = (x[:n_ctx].reshape(bp, tp, d), x[n_ctx:].reshape(bs, ts, d))
            hgrn = functools.partial(_hgrn_layer, g_pre=g_pre_mix[l], g_post=g_post_mix[l], w_in=hgrn_w_in[a],
                                     lb=lb_all[l], g_norm=hgrn_g_norm[a], w_out=hgrn_w_out[a])
            assert (bp * ctx_tiles) % (lat_tiles * SEQ_PAR) == 0
            ctx_groups = bp * ctx_tiles // lat_tiles
            stream = jax.ShapeDtypeStruct((ctx_groups + bs, lat_tiles, TM, d), F32)
            stream, s_f, s_b = hgrn(x_parts[0], jnp.broadcast_to(mod[0], (bp, 6, d)),
                                    s0=jnp.zeros((bp, 2, n_heads, HGRN_DK, HGRN_DK), F32),
                                    stream=stream, group0=0)
            stream, _, _ = hgrn(x_parts[1], mod[1:1 + bs], s0=state_hgrn[:, a].astype(F32),
                                stream=stream, group0=ctx_groups)
            x = stream.reshape(n_ctx + bs * ts, d)
            new_states.append(jnp.stack([s_f, s_b], axis=1))
        else:
            m = l // 2
            w_qkv = attn_w_qkv[m].astype(BF16)
            q_c, k_c, v_c = _qkv(x, mod, grp, g_pre_mix[l], w_qkv, attn_q_gain[m], attn_k_gain[m],
                                 0, n_ctx // TM, 0, F32)
            q_l, k_l, v_l = _qkv(x, mod, grp, g_pre_mix[l], w_qkv, attn_q_gain[m], attn_k_gain[m],
                                 n_ctx // TM, bs * ts // TM, ts, BF16)
            o_c = _attention(q_c, k_c, v_c, bp, tp)
            past = cache_k.shape[2]
            o_l = _attention(q_l, k_l, v_l, bs, ts,
                             cache=(cache_k[:, m].reshape(bs, past, nk), cache_v[:, m].reshape(bs, past, nk)))
            new_k.append(k_c.reshape(bp, tp, N_KV_HEADS, HEAD_DIM))
            new_v.append(v_c.reshape(bp, tp, N_KV_HEADS, HEAD_DIM))
            x = _attn_out(x, o_c, o_l, mod, grp_mm, attn_w_o[m].astype(BF16), g_post_mix[l])
        x = _moe_layer(x, mod, grp, grp_mm, l, g_pre_ffn[l], g_post_ffn[l], moe_w_router[l], moe_b_router[l],
                       moe_w_gate, moe_b_gate, moe_w_up, moe_b_up, moe_w_down, moe_b_down,
                       split=bp * ctx_tiles if l == depth - 1 else None)

    y_prompt, y_sample = x
    return (y_prompt.reshape(bp, tp, d), y_sample.reshape(bs, ts, d), jnp.stack(new_states, axis=1),
            jnp.stack(new_k, axis=1), jnp.stack(new_v, axis=1))
```

```python
import functools

import numpy as np
import jax
import jax.numpy as jnp
from jax import lax
from jax.experimental import pallas as pl
from jax.experimental.pallas import tpu as pltpu

F32 = jnp.float32
BF16 = jnp.bfloat16
I32 = jnp.int32

EPS = 1e-6
GRID_W = 64
ROPE_THETA = 10000.0
HGRN_DK = 128
HEAD_DIM = 128
N_KV_HEADS = 2
TOP_K = 4
SWIGLU_LIMIT = 7.0
SWIGLU_ALPHA = 1.702

LANES = 128
TM = 256
MM_TILE = 1024
ATT_TQ = 1024
RANK_BLOCK = 256
GLA_CHUNK = 64
GLA_GROUP = 4
SEQ_PAR = 2
MOE_BM = 512
FF_CHUNK = 512
ROW_TILE = 8
DMA_UNROLL = 8
ZERO_GROUP = 16
MOD_COLS = 512
VMEM_LIMIT = 56 * 1024 * 1024
NEG_BIG = -1e30

NT_DIMS = (((1,), (1,)), ((), ()))
TN_DIMS = (((0,), (0,)), ((), ()))


def _params(*sem):
    return pltpu.CompilerParams(dimension_semantics=sem, vmem_limit_bytes=VMEM_LIMIT)


def _sigmoid(x):
    return 1.0 / (1.0 + jnp.exp(-x))


def _silu(x):
    return x * _sigmoid(x)


def _rms(x, gain):
    ms = jnp.mean(x * x, axis=-1, keepdims=True)
    return x * lax.rsqrt(ms + EPS) * gain


def _dot(a, b):
    return jnp.dot(a, b, preferred_element_type=F32)


def _mod_kernel(c_ref, w_ref, b_ref, o_ref):
    s = _silu(c_ref[...]).astype(BF16)
    o_ref[...] = _dot(s, w_ref[...].astype(BF16)) + b_ref[...]


def _mod_table(cvec, w_mod, b_mod):
    depth, d, n = w_mod.shape
    rows = cvec.shape[0]
    return pl.pallas_call(
        _mod_kernel,
        grid=(depth, n // MOD_COLS),
        in_specs=[
            pl.BlockSpec((rows, d), lambda l, j: (0, 0)),
            pl.BlockSpec((None, d, MOD_COLS), lambda l, j: (l, 0, j)),
            pl.BlockSpec((None, 1, MOD_COLS), lambda l, j: (l, 0, j)),
        ],
        out_specs=pl.BlockSpec((None, rows, MOD_COLS), lambda l, j: (l, 0, j)),
        out_shape=jax.ShapeDtypeStruct((depth, rows, n), F32),
        compiler_params=_params("arbitrary", "arbitrary"),
        name="mod_table",
    )(cvec, w_mod, b_mod.reshape(depth, 1, n))


def _gla_tiles(q_s, k_s, lf_s, v_s, st_ref, o_s, direction):
    c = GLA_CHUNK
    pair = GLA_GROUP * c
    d_model = q_s.shape[1]
    n_heads = d_model // HGRN_DK
    ri = lax.broadcasted_iota(I32, (pair, pair), 0)
    ci = lax.broadcasted_iota(I32, (pair, pair), 1)
    causal = (ci <= ri) if direction == 0 else (ci >= ri)
    shift = c.bit_length() - 1
    assert c == 1 << shift
    keep = jnp.logical_and(causal, (ri >> shift) == (ci >> shift))
    ones_tri = jnp.where(keep, 1.0, 0.0).astype(BF16)
    pairs = list(range(TM // pair))
    halves = list(range(GLA_GROUP))
    if direction == 1:
        pairs.reverse()
        halves.reverse()
    for pr, seq in [(pr, seq) for pr in pairs for seq in range(SEQ_PAR)]:
        row0 = seq * TM + pr * pair
        rows = pl.ds(row0, pair)
        lf = lf_s[rows, :]
        lf_hi = lf.astype(BF16)
        lf_lo = (lf - lf_hi.astype(F32)).astype(BF16)
        cum2 = _dot(ones_tri, lf_hi) + _dot(ones_tri, lf_lo)
        q_dec, k_inv, q_in, k_end, e_tot = [], [], [], [], []
        for hf in range(GLA_GROUP):
            cum = cum2[hf * c:(hf + 1) * c]
            part = pl.ds(row0 + hf * c, c)
            if direction == 0:
                btot, bmid = cum[c - 1:c], cum[c // 2 - 1:c // 2]
            else:
                btot, bmid = cum[0:1], cum[c // 2:c // 2 + 1]
            qd = q_s[part, :] * jnp.exp(cum - bmid)
            ki = k_s[part, :] * jnp.exp(bmid - cum)
            q_in.append((qd * jnp.exp(bmid)).astype(BF16))
            k_end.append((ki * jnp.exp(btot - bmid)).astype(BF16))
            q_dec.append(qd.astype(BF16))
            k_inv.append(ki.astype(BF16))
            e_tot.append(jnp.exp(btot))
        q_dec = jnp.concatenate(q_dec, axis=0)
        k_inv = jnp.concatenate(k_inv, axis=0)
        vc = v_s[rows, :].astype(BF16)
        for h in range(n_heads):
            sl = slice(h * HGRN_DK, (h + 1) * HGRN_DK)
            a = lax.dot_general(q_dec[:, sl], k_inv[:, sl], NT_DIMS, preferred_element_type=F32)
            a = jnp.where(keep, a, 0.0).astype(BF16)
            o_intra = _dot(a, vc[:, sl])
            for hf in halves:
                part = slice(hf * c, (hf + 1) * c)
                st = st_ref[seq, h]
                o_s[pl.ds(row0 + hf * c, c), sl] = o_intra[part] + lax.dot_general(
                    q_in[hf][:, sl], st.astype(BF16), NT_DIMS, preferred_element_type=F32)
                st_ref[seq, h] = st * e_tot[hf][:, sl] + lax.dot_general(
                    vc[part, sl], k_end[hf][:, sl], TN_DIMS, preferred_element_type=F32)


def _hgrn_gates(h, w_ref, lb, q_s, k_s, lf_s, v_s):
    d = h.shape[1]
    q_s[...] = _silu(_dot(h, w_ref[:, 0:d]))
    f = lb + (1.0 - lb) * _sigmoid(_dot(h, w_ref[:, d:2 * d]))
    lf_s[...] = jnp.log(f)
    k_s[...] = 1.0 - f
    v_s[...] = _dot(h, w_ref[:, 2 * d:3 * d])


def _load_state(s0_ref, st_s):
    @pl.when(pl.program_id(1) == 0)
    def _():
        for s in range(st_s.shape[0]):
            for h in range(st_s.shape[1]):
                st_s[s, h] = s0_ref[s, h].T


def _store_state(sfin_ref, st_s):
    @pl.when(pl.program_id(1) == pl.num_programs(1) - 1)
    def _():
        for s in range(st_s.shape[0]):
            for h in range(st_s.shape[1]):
                sfin_ref[s, h] = st_s[s, h].T


def _hgrn_norm(x_ref, mod_ref, gpre_ref):
    hs = [(_rms(x_ref[s], gpre_ref[...]) * (1.0 + mod_ref[s, 1:2]) + mod_ref[s, 0:1]).astype(BF16)
          for s in range(SEQ_PAR)]
    return jnp.concatenate(hs, axis=0)


def _hgrn_bwd_kernel(x_ref, mod_ref, gpre_ref, w_ref, lb_ref, s0_ref,
                     ob_ref, qb_ref, vb_ref, sfin_ref, st_s, o_s, q_s, k_s, lf_s, v_s):
    _load_state(s0_ref, st_s)
    h = _hgrn_norm(x_ref, mod_ref, gpre_ref)
    _hgrn_gates(h, w_ref, lb_ref[...], q_s, k_s, lf_s, v_s)
    for s in range(SEQ_PAR):
        rows = slice(s * TM, (s + 1) * TM)
        qb_ref[s] = q_s[rows, :].astype(BF16)
        vb_ref[s] = v_s[rows, :].astype(BF16)
    _gla_tiles(q_s, k_s, lf_s, v_s, st_s, o_s, 1)
    for s in range(SEQ_PAR):
        ob_ref[s] = o_s[s * TM:(s + 1) * TM, :].astype(BF16)
    _store_state(sfin_ref, st_s)


def _hgrn_fwd_kernel(x_ref, mod_ref, gpre_ref, w_ref, lb_ref, s0_ref, ob_ref, qb_ref, vb_ref,
                     gnorm_ref, wout_ref, gpost_ref, *rest):
    xo_ref, sfin_ref, st_s, o_s, q_s, k_s, lf_s, v_s = rest[-8:]
    _load_state(s0_ref, st_s)
    d = x_ref.shape[2]
    h = _hgrn_norm(x_ref, mod_ref, gpre_ref)
    lb = lb_ref[...]
    f = lb + (1.0 - lb) * _sigmoid(_dot(h, w_ref[:, 0:d]))
    lf_s[...] = jnp.log(f)
    k_s[...] = 1.0 - f
    for s in range(SEQ_PAR):
        rows = slice(s * TM, (s + 1) * TM)
        q_s[rows, :] = qb_ref[s].astype(F32)
        v_s[rows, :] = vb_ref[s].astype(F32)
    _gla_tiles(q_s, k_s, lf_s, v_s, st_s, o_s, 0)
    _store_state(sfin_ref, st_s)
    gate = _silu(_dot(h, w_ref[:, d:2 * d]))
    gnorm = gnorm_ref[...]
    o_b = jnp.concatenate([ob_ref[s] for s in range(SEQ_PAR)], axis=0).astype(F32)
    for hh in range(d // HGRN_DK):
        sl = slice(hh * HGRN_DK, (hh + 1) * HGRN_DK)
        o_s[:, sl] = _rms(o_s[:, sl] + o_b[:, sl], gnorm) * gate[:, sl]
    out = _dot(o_s[...].astype(BF16), wout_ref[...])
    for s in range(SEQ_PAR):
        xo_ref[s] = x_ref[s] + mod_ref[s, 2:3] * _rms(out[s * TM:(s + 1) * TM], gpost_ref[...])


def _tile_groups(n_ctx_tiles, n_lat_seq, lat_tiles):
    g = [0] * n_ctx_tiles
    for b in range(n_lat_seq):
        g += [1 + b] * lat_tiles
    return jnp.asarray(np.array(g, dtype=np.int32))


def _hgrn_layer(x, mod_seq, g_pre, g_post, w_in, lb, g_norm, w_out, s0, stream, group0):
    n_seq, seq_len, d = x.shape
    n_heads = d // HGRN_DK
    tiles = seq_len // TM
    per_group = stream.shape[1]
    assert n_seq % SEQ_PAR == 0 and per_group % SEQ_PAR == 0 and tiles in (1, per_group)
    x4 = x.reshape(n_seq, tiles, TM, d)
    w_bwd = jnp.concatenate([w_in[:, 0:d], w_in[:, 2 * d:3 * d], w_in[:, 3 * d:4 * d]], axis=1).astype(BF16)
    w_fwd = jnp.concatenate([w_in[:, d:2 * d], w_in[:, 4 * d:5 * d]], axis=1).astype(BF16)

    def specs(direction):
        def tile_spec():
            if direction == 0:
                return pl.BlockSpec((SEQ_PAR, None, TM, d), lambda p, j: (p, j, 0, 0))
            return pl.BlockSpec((SEQ_PAR, None, TM, d), lambda p, j: (p, tiles - 1 - j, 0, 0))
        state = pl.BlockSpec((SEQ_PAR, n_heads, HGRN_DK, HGRN_DK), lambda p, j: (p, 0, 0, 0))
        mods = pl.BlockSpec((SEQ_PAR, 6, d), lambda p, j: (p, 0, 0))
        vec = pl.BlockSpec((1, d), lambda p, j: (0, 0))
        return tile_spec, state, mods, vec

    def const(shape):
        return pl.BlockSpec(shape, lambda p, j: (0, 0))

    rows = SEQ_PAR * TM
    scratch = ([pltpu.VMEM((SEQ_PAR, n_heads, HGRN_DK, HGRN_DK), F32)]
               + [pltpu.VMEM((rows, d), F32)] * 5)
    grid = (n_seq // SEQ_PAR, tiles)
    state_shape = jax.ShapeDtypeStruct((n_seq, n_heads, HGRN_DK, HGRN_DK), F32)

    tile_spec, state, mods, vec = specs(1)
    o_bwd, q_act, v_act, s_bwd = pl.pallas_call(
        _hgrn_bwd_kernel,
        grid=grid,
        in_specs=[tile_spec(), mods, vec, const((d, 3 * d)), vec, state],
        out_specs=[tile_spec(), tile_spec(), tile_spec(), state],
        scratch_shapes=scratch,
        out_shape=[jax.ShapeDtypeStruct(x4.shape, BF16)] * 3 + [state_shape],
        compiler_params=_params("arbitrary", "arbitrary"),
        name="hgrn_bwd",
    )(x4, mod_seq, g_pre.reshape(1, d), w_bwd, lb[1].reshape(1, d), s0[:, 1])

    tile_spec, state, mods, vec = specs(0)
    if tiles == per_group:
        stream_spec = pl.BlockSpec((SEQ_PAR, None, TM, d), lambda p, j: (group0 // SEQ_PAR + p, j, 0, 0))
    else:
        per = per_group // SEQ_PAR
        stream_spec = pl.BlockSpec((None, SEQ_PAR, TM, d), lambda p, j: (group0 + p // per, p % per, 0, 0))
    in_specs = [tile_spec(), mods, vec, const((d, 2 * d)), vec, state, tile_spec(), tile_spec(), tile_spec(),
                const((1, HGRN_DK)), const((d, d)), vec]
    args = [x4, mod_seq, g_pre.reshape(1, d), w_fwd, lb[0].reshape(1, d), s0[:, 0],
            o_bwd, q_act, v_act, g_norm.reshape(1, HGRN_DK), w_out.astype(BF16), g_post.reshape(1, d)]
    aliases = {}
    if not isinstance(stream, jax.ShapeDtypeStruct):
        aliases = {len(args): 0}
        in_specs.append(pl.BlockSpec(memory_space=pl.ANY))
        args.append(stream)
    stream, s_fwd = pl.pallas_call(
        _hgrn_fwd_kernel,
        grid=grid,
        in_specs=in_specs,
        out_specs=[stream_spec, state],
        scratch_shapes=scratch,
        out_shape=[jax.ShapeDtypeStruct(stream.shape, F32), state_shape],
        input_output_aliases=aliases,
        compiler_params=_params("arbitrary", "arbitrary"),
        name="hgrn_fwd",
    )(*args)
    return stream, s_fwd, s_bwd


def _rope_tables(n_tok):
    axis_dim = HEAD_DIM // 2
    half = axis_dim // 2
    pos = np.arange(n_tok)
    row, col = pos // GRID_W, pos % GRID_W
    inv_freq = ROPE_THETA ** (-np.arange(0, axis_dim, 2, dtype=np.float64) / axis_dim)
    lane = np.arange(HEAD_DIM)
    p = np.where(lane[None, :] < axis_dim, row[:, None], col[:, None]).astype(np.float64)
    ang = p * inv_freq[(lane % axis_dim) % half][None, :]
    sign = np.where((lane % axis_dim) < half, -1.0, 1.0)[None, :]
    return jnp.asarray(np.cos(ang), F32), jnp.asarray(np.sin(ang) * sign, F32)


def _rope(xh, cos, sin_signed):
    axis_dim = HEAD_DIM // 2
    half = axis_dim // 2
    lane = lax.broadcasted_iota(I32, xh.shape, 1)
    upper = pltpu.roll(xh, HEAD_DIM - half, 1)
    lower = pltpu.roll(xh, half, 1)
    partner = jnp.where((lane % axis_dim) < half, upper, lower)
    return xh * cos + partner * sin_signed


def _qkv_kernel(grp, x_ref, mod_ref, gpre_ref, w_ref, qg_ref, kg_ref, *rest, rope):
    del grp
    if rope:
        cos_ref, sin_ref, q_out, k_out, v_out = rest
    else:
        q_out, k_out, v_out = rest
    d = x_ref.shape[1]
    nk = N_KV_HEADS * HEAD_DIM
    h = (_rms(x_ref[...], gpre_ref[...]) * (1.0 + mod_ref[1:2]) + mod_ref[0:1]).astype(BF16)
    q = _dot(h, w_ref[:, 0:d])
    k = _dot(h, w_ref[:, d:d + nk])
    v_out[...] = _dot(h, w_ref[:, d + nk:d + 2 * nk]).astype(v_out.dtype)
    scale = HEAD_DIM ** -0.5
    for hh in range(d // HEAD_DIM):
        sl = slice(hh * HEAD_DIM, (hh + 1) * HEAD_DIM)
        qh = _rms(q[:, sl], qg_ref[...])
        if rope:
            qh = _rope(qh, cos_ref[...], sin_ref[...])
        q_out[:, sl] = (qh * scale).astype(q_out.dtype)
    for hh in range(N_KV_HEADS):
        sl = slice(hh * HEAD_DIM, (hh + 1) * HEAD_DIM)
        kh = _rms(k[:, sl], kg_ref[...])
        if rope:
            kh = _rope(kh, cos_ref[...], sin_ref[...])
        k_out[:, sl] = kh.astype(k_out.dtype)


def _qkv(x, mod, grp, g_pre, w_qkv, q_gain, k_gain, tile0, n_tiles, rope_len, kv_dtype):
    t, d = x.shape
    nk = N_KV_HEADS * HEAD_DIM
    n_out = n_tiles * TM
    tile_in = pl.BlockSpec((TM, d), lambda i, g: (tile0 + i, 0))
    in_specs = [tile_in,
                pl.BlockSpec((None, 6, d), lambda i, g: (g[tile0 + i], 0, 0)),
                pl.BlockSpec((1, d), lambda i, g: (0, 0)),
                pl.BlockSpec((d, d + 2 * nk), lambda i, g: (0, 0)),
                pl.BlockSpec((1, HEAD_DIM), lambda i, g: (0, 0)),
                pl.BlockSpec((1, HEAD_DIM), lambda i, g: (0, 0))]
    args = [grp, x, mod, g_pre.reshape(1, d), w_qkv, q_gain.reshape(1, HEAD_DIM), k_gain.reshape(1, HEAD_DIM)]
    if rope_len:
        per_seq = rope_len // TM
        cos, sin = _rope_tables(rope_len)
        in_specs += [pl.BlockSpec((TM, HEAD_DIM), lambda i, g: (i % per_seq, 0))] * 2
        args += [cos, sin]
    return pl.pallas_call(
        functools.partial(_qkv_kernel, rope=bool(rope_len)),
        grid_spec=pltpu.PrefetchScalarGridSpec(
            num_scalar_prefetch=1, grid=(n_tiles,), in_specs=in_specs,
            out_specs=[pl.BlockSpec((TM, d), lambda i, g: (i, 0)),
                       pl.BlockSpec((TM, nk), lambda i, g: (i, 0)),
                       pl.BlockSpec((TM, nk), lambda i, g: (i, 0))]),
        out_shape=[jax.ShapeDtypeStruct((n_out, d), BF16),
                   jax.ShapeDtypeStruct((n_out, nk), kv_dtype),
                   jax.ShapeDtypeStruct((n_out, nk), kv_dtype)],
        compiler_params=_params("arbitrary"),
        name="qkv_rope" if rope_len else "qkv",
    )(*args)


def _attn_kernel(q_ref, *rest, n_pieces, n_sub):
    kv, o_ref = rest[:-1], rest[-1]
    q_rows = q_ref.shape[0] // n_sub
    kv_rows = kv[-1].shape[0] // n_sub
    group = q_ref.shape[1] // HEAD_DIM
    for sub in range(n_sub):
        qr = slice(sub * q_rows, (sub + 1) * q_rows)
        if n_sub == 1:
            ks = [kv[2 * p][...].astype(BF16) for p in range(n_pieces)]
            vs = [kv[2 * p + 1][...].astype(BF16) for p in range(n_pieces)]
        else:
            kr = slice(sub * kv_rows, (sub + 1) * kv_rows)
            ks, vs = [kv[0][kr, :].astype(BF16)], [kv[1][kr, :].astype(BF16)]
        for g in range(group):
            sl = slice(g * HEAD_DIM, (g + 1) * HEAD_DIM)
            qh = q_ref[qr, sl]
            ss = [lax.dot_general(qh, kp, NT_DIMS, preferred_element_type=F32) for kp in ks]
            m = ss[0].max(axis=-1, keepdims=True)
            for s in ss[1:]:
                m = jnp.maximum(m, s.max(axis=-1, keepdims=True))
            ps = [jnp.exp(s - m) for s in ss]
            denom = ps[0].sum(axis=-1, keepdims=True)
            for p in ps[1:]:
                denom = denom + p.sum(axis=-1, keepdims=True)
            o = _dot(ps[0].astype(BF16), vs[0])
            for p, vp in zip(ps[1:], vs[1:]):
                o = o + _dot(p.astype(BF16), vp)
            o_ref[qr, sl] = (o / denom).astype(o_ref.dtype)


def _attention(q, k_new, v_new, n_seq, seq_len, cache=None):
    t, d = q.shape
    group_w = d // N_KV_HEADS
    n_sub = 1
    if cache is None and seq_len < ATT_TQ:
        n_sub = max(s for s in range(1, ATT_TQ // seq_len + 1) if n_seq % s == 0)
    if n_sub > 1:
        n_seq, seq_len = n_seq // n_sub, seq_len * n_sub
    tq = min(ATT_TQ, seq_len)
    nq = seq_len // tq
    q_spec = pl.BlockSpec((tq, group_w), lambda b, i, kv: (b * nq + i, kv))
    new_spec = pl.BlockSpec((seq_len, HEAD_DIM), lambda b, i, kv: (b, kv))
    in_specs, args = [q_spec], [q]
    if cache is not None:
        past = cache[0].shape[1]
        c_spec = pl.BlockSpec((None, past, HEAD_DIM), lambda b, i, kv: (b, 0, kv))
        in_specs += [c_spec, c_spec]
        args += list(cache)
    in_specs += [new_spec, new_spec]
    args += [k_new, v_new]
    return pl.pallas_call(
        functools.partial(_attn_kernel, n_pieces=len(args) // 2, n_sub=n_sub),
        grid=(n_seq, nq, N_KV_HEADS),
        in_specs=in_specs,
        out_specs=q_spec,
        out_shape=jax.ShapeDtypeStruct((t, d), BF16),
        compiler_params=_params("arbitrary", "arbitrary", "arbitrary"),
        name="attention",
    )(*args)


def _attn_out_kernel(grp, x_ref, oc_ref, ol_ref, mod_ref, w_ref, gpost_ref, xo_ref, *, n_ctx_tiles):
    del grp
    o = jnp.where(pl.program_id(0) < n_ctx_tiles, oc_ref[...], ol_ref[...])
    out = _dot(o, w_ref[...])
    xo_ref[...] = x_ref[...] + mod_ref[2:3] * _rms(out, gpost_ref[...])


def _attn_out(x, o_ctx, o_lat, mod, grp, w_o, g_post):
    t, d = x.shape
    nct = o_ctx.shape[0] // MM_TILE
    tile = pl.BlockSpec((MM_TILE, d), lambda i, g: (i, 0))
    return pl.pallas_call(
        functools.partial(_attn_out_kernel, n_ctx_tiles=nct),
        grid_spec=pltpu.PrefetchScalarGridSpec(
            num_scalar_prefetch=1, grid=(t // MM_TILE,),
            in_specs=[tile,
                      pl.BlockSpec((MM_TILE, d), lambda i, g: (jnp.minimum(i, nct - 1), 0)),
                      pl.BlockSpec((MM_TILE, d), lambda i, g: (jnp.maximum(i - nct, 0), 0)),
                      pl.BlockSpec((None, 6, d), lambda i, g: (g[i], 0, 0)),
                      pl.BlockSpec((d, d), lambda i, g: (0, 0)),
                      pl.BlockSpec((1, d), lambda i, g: (0, 0))],
            out_specs=tile),
        out_shape=jax.ShapeDtypeStruct((t, d), F32),
        input_output_aliases={1: 0},
        compiler_params=_params("arbitrary"),
        name="attn_out",
    )(grp, x, o_ctx, o_lat, mod, w_o, g_post.reshape(1, d))


def _to_token_tiles(ref, x):
    rows = x.shape[0]
    for s in range(x.shape[1] // LANES):
        ref[pl.ds(s, rows, stride=ROW_TILE), :] = x[:, s * LANES:(s + 1) * LANES]


def _from_token_tiles(ref, rows):
    n = ROW_TILE
    return jnp.concatenate([ref[pl.ds(s, rows, stride=n), :] for s in range(n)], axis=1)


def _router_kernel(grp, x_ref, mod_ref, g_ref, wr_ref, br_ref,
                   h_ref, route_ref, gate_ref, cnt_ref, carry):
    del grp
    @pl.when(pl.program_id(0) == 0)
    def _():
        carry[...] = jnp.zeros_like(carry)
    h = _rms(x_ref[...], g_ref[...]) * (1.0 + mod_ref[4:5]) + mod_ref[3:4]
    _to_token_tiles(h_ref, h)
    logits = _dot(h.astype(BF16), wr_ref[...]) + br_ref[...]
    lane = lax.broadcasted_iota(I32, logits.shape, 1)
    lane_f = lane.astype(F32)
    vals, idxs = [], []
    work = logits
    for _ in range(TOP_K):
        m = work.max(axis=-1, keepdims=True)
        idx = jnp.where(work == m, lane_f, float(LANES)).min(axis=-1, keepdims=True).astype(I32)
        vals.append(m)
        idxs.append(idx)
        work = jnp.where(lane == idx, -jnp.inf, work)
    es = [jnp.exp(v - vals[0]) for v in vals]
    denom = es[0]
    for e in es[1:]:
        denom = denom + e
    onehot = jnp.zeros(logits.shape, F32)
    for idx in idxs:
        onehot = onehot + jnp.where(lane == idx, 1.0, 0.0)
    ri = lax.broadcasted_iota(I32, (RANK_BLOCK, RANK_BLOCK), 0)
    ci = lax.broadcasted_iota(I32, (RANK_BLOCK, RANK_BLOCK), 1)
    before = jnp.where(ci < ri, 1.0, 0.0).astype(BF16)
    counts = carry[0:1]
    ranks = []
    for r0 in range(0, MM_TILE, RANK_BLOCK):
        part = onehot[r0:r0 + RANK_BLOCK]
        ranks.append(_dot(before, part.astype(BF16)) + counts)
        counts = counts + part.sum(axis=0, keepdims=True)
    rank_all = jnp.concatenate(ranks, axis=0)
    route = jnp.zeros(logits.shape, I32)
    gate = jnp.zeros(logits.shape, F32)
    for kk in range(TOP_K):
        rank = jnp.where(lane == idxs[kk], rank_all, 0.0).sum(axis=-1, keepdims=True)
        route = jnp.where(lane == kk, idxs[kk], route)
        route = jnp.where(lane == TOP_K + kk, rank.astype(I32), route)
        gate = jnp.where(lane == kk, es[kk] / denom, gate)
    route_ref[...] = route
    gate_ref[...] = gate
    carry[...] = jnp.broadcast_to(counts, carry.shape)
    cnt_ref[...] = carry[...]


def _router(x, mod, grp, g_pre, w_router, b_router):
    t, d = x.shape
    n_exp = w_router.shape[1]
    wr = jnp.pad(w_router.astype(BF16), ((0, 0), (0, LANES - n_exp)))
    br = jnp.pad(b_router.reshape(1, n_exp), ((0, 0), (0, LANES - n_exp)), constant_values=NEG_BIG)
    tile = pl.BlockSpec((MM_TILE, d), lambda i, g: (i, 0))
    lane_tile = pl.BlockSpec((MM_TILE, LANES), lambda i, g: (i, 0))
    return pl.pallas_call(
        _router_kernel,
        grid_spec=pltpu.PrefetchScalarGridSpec(
            num_scalar_prefetch=1, grid=(t // MM_TILE,),
            in_specs=[tile,
                      pl.BlockSpec((None, 6, d), lambda i, g: (g[i], 0, 0)),
                      pl.BlockSpec((1, d), lambda i, g: (0, 0)),
                      pl.BlockSpec((d, LANES), lambda i, g: (0, 0)),
                      pl.BlockSpec((1, LANES), lambda i, g: (0, 0))],
            out_specs=[pl.BlockSpec((MM_TILE * ROW_TILE, LANES), lambda i, g: (i, 0)), lane_tile, lane_tile,
                       pl.BlockSpec((8, LANES), lambda i, g: (0, 0))],
            scratch_shapes=[pltpu.VMEM((8, LANES), F32)]),
        out_shape=[jax.ShapeDtypeStruct((t * ROW_TILE, LANES), F32),
                   jax.ShapeDtypeStruct((t, LANES), I32),
                   jax.ShapeDtypeStruct((t, LANES), F32),
                   jax.ShapeDtypeStruct((8, LANES), F32)],
        compiler_params=_params("arbitrary"),
        name="router",
    )(grp, x, mod, g_pre.reshape(1, d), wr, br)


def _tile_copy(src, src_row, dst, dst_row, sem):
    n = ROW_TILE
    return pltpu.make_async_copy(src.at[pl.ds(pl.multiple_of(src_row * n, n), n)],
                                 dst.at[pl.ds(pl.multiple_of(dst_row * n, n), n)], sem)


def _dispatch_kernel(pad_lo, pad_hi, dest_ref, h_ref, xs_hbm, zero_s, sem, zsem):
    def issue(g, c):
        for j in range(DMA_UNROLL):
            t = g * DMA_UNROLL + j
            for kk in range(TOP_K):
                _tile_copy(h_ref, t, xs_hbm, dest_ref[t * TOP_K + kk], sem).start(priority=kk % 2)
        return c
    lax.fori_loop(0, TM // DMA_UNROLL, issue, 0)

    @pl.when(pl.program_id(0) == 0)
    def _():
        zero_s[...] = jnp.zeros_like(zero_s)
        group_rows = ZERO_GROUP * ROW_TILE

        def single(s):
            return _tile_copy(zero_s, 0, xs_hbm, s, zsem)

        def group(g):
            return pltpu.make_async_copy(
                zero_s, xs_hbm.at[pl.ds(pl.multiple_of(g * group_rows, group_rows), group_rows)], zsem)

        def per_expert(e, c):
            lo, hi = pad_lo[e], pad_hi[e]
            mid = jnp.minimum(hi, (lo + ZERO_GROUP - 1) // ZERO_GROUP * ZERO_GROUP)
            g_lo, g_hi = mid // ZERO_GROUP, hi // ZERO_GROUP
            lax.fori_loop(lo, mid, lambda s, c2: (single(s).start(), c2)[1], 0)
            lax.fori_loop(g_lo, g_hi, lambda g, c2: (group(g).start(), c2)[1], 0)
            lax.fori_loop(lo, mid, lambda s, c2: (single(s).wait(), c2)[1], 0)
            lax.fori_loop(g_lo, g_hi, lambda g, c2: (group(g).wait(), c2)[1], 0)
            return c
        lax.fori_loop(0, pad_lo.shape[0], per_expert, 0)

    for _ in range(TOP_K):
        pltpu.make_async_copy(h_ref, xs_hbm.at[pl.ds(0, TM * ROW_TILE)], sem).wait()


def _dispatch(h3, dest, pad_lo, pad_hi, n_slots):
    t = h3.shape[0] // ROW_TILE
    return pl.pallas_call(
        _dispatch_kernel,
        grid_spec=pltpu.PrefetchScalarGridSpec(
            num_scalar_prefetch=2, grid=(t // TM,),
            in_specs=[pl.BlockSpec((TM * TOP_K,), lambda i, lo, hi: (i,), memory_space=pltpu.SMEM),
                      pl.BlockSpec((TM * ROW_TILE, LANES), lambda i, lo, hi: (i, 0))],
            out_specs=pl.BlockSpec(memory_space=pl.ANY),
            scratch_shapes=[pltpu.VMEM((ZERO_GROUP * ROW_TILE, LANES), F32),
                            pltpu.SemaphoreType.DMA, pltpu.SemaphoreType.DMA]),
        out_shape=jax.ShapeDtypeStruct((n_slots * ROW_TILE, LANES), F32),
        compiler_params=_params("arbitrary"),
        name="moe_dispatch",
    )(pad_lo, pad_hi, dest, h3)


def _expert_kernel(blk0, nblk, cnt, xs_hbm, wg_ref, bg_ref, wu_ref, bu_ref, wd_ref, bd_ref,
                   y_hbm, wg_s, wu_s, wd_s, xbuf, ybuf, in_sem, out_sem):
    e = pl.program_id(0)
    first = blk0[e]
    n = nblk[e]
    rows = MOE_BM * ROW_TILE

    def in_copy(b, slot):
        return pltpu.make_async_copy(
            xs_hbm.at[pl.ds(pl.multiple_of((first + b) * rows, rows), rows)], xbuf.at[slot], in_sem.at[slot])

    def out_copy(b, slot):
        return pltpu.make_async_copy(
            ybuf.at[slot], y_hbm.at[pl.ds(pl.multiple_of((first + b) * rows, rows), rows)], out_sem.at[slot])

    @pl.when(e == 0)
    def _():
        ybuf[...] = jnp.zeros_like(ybuf)

    @pl.when(n > 0)
    def _():
        in_copy(0, 0).start()
        wg_s[...] = wg_ref[...].astype(BF16)
        wu_s[...] = wu_ref[...].astype(BF16)
        wd_s[...] = wd_ref[...].astype(BF16)

    def block(b, c):
        slot = b % 2
        in_copy(b, slot).wait()

        @pl.when(b + 1 < n)
        def _():
            in_copy(b + 1, 1 - slot).start()

        @pl.when(b >= 2)
        def _():
            out_copy(b - 2, slot).wait()

        def mlp(n_rows):
            x = _from_token_tiles(xbuf.at[slot], n_rows).astype(BF16)
            y = bd_ref[...]
            for c0 in range(0, wg_s.shape[1], FF_CHUNK):
                cols = slice(c0, c0 + FF_CHUNK)
                hg = jnp.minimum(_dot(x, wg_s[:, cols]) + bg_ref[:, cols], SWIGLU_LIMIT)
                hu = jnp.clip(_dot(x, wu_s[:, cols]) + bu_ref[:, cols], -SWIGLU_LIMIT, SWIGLU_LIMIT)
                hh = (hu + 1.0) * hg * _sigmoid(SWIGLU_ALPHA * hg)
                y = y + _dot(hh.astype(BF16), wd_s[cols, :])
            _to_token_tiles(ybuf.at[slot], y)

        real_rows = cnt[e] - b * MOE_BM

        @pl.when(real_rows > MOE_BM // 2)
        def _():
            mlp(MOE_BM)

        @pl.when(real_rows <= MOE_BM // 2)
        def _():
            mlp(MOE_BM // 2)

        out_copy(b, slot).start()
        return c
    lax.fori_loop(0, n, block, 0)

    @pl.when(n >= 2)
    def _():
        out_copy(n - 2, n % 2).wait()

    @pl.when(n >= 1)
    def _():
        out_copy(n - 1, (n - 1) % 2).wait()


def _experts(xs, blk0, nblk, counts, layer, w_gate, b_gate, w_up, b_up, w_down, b_down):
    depth, n_exp, d, f = w_gate.shape

    def w_spec(k, n):
        return pl.BlockSpec((None, None, k, n), lambda e, b0, nb, cnt: (layer, e, 0, 0))

    rows = MOE_BM * ROW_TILE
    return pl.pallas_call(
        _expert_kernel,
        grid_spec=pltpu.PrefetchScalarGridSpec(
            num_scalar_prefetch=3, grid=(n_exp,),
            in_specs=[pl.BlockSpec(memory_space=pl.ANY),
                      w_spec(d, f), w_spec(1, f), w_spec(d, f), w_spec(1, f), w_spec(f, d), w_spec(1, d)],
            out_specs=pl.BlockSpec(memory_space=pl.ANY),
            scratch_shapes=[pltpu.VMEM((d, f), BF16), pltpu.VMEM((d, f), BF16), pltpu.VMEM((f, d), BF16),
                            pltpu.VMEM((2, rows, LANES), F32), pltpu.VMEM((2, rows, LANES), F32),
                            pltpu.SemaphoreType.DMA((2,)), pltpu.SemaphoreType.DMA((2,))]),
        out_shape=jax.ShapeDtypeStruct(xs.shape, F32),
        compiler_params=_params("arbitrary"),
        name="moe_experts",
    )(blk0, nblk, counts, xs, w_gate, b_gate.reshape(depth, n_exp, 1, f), w_up, b_up.reshape(depth, n_exp, 1, f),
      w_down, b_down.reshape(depth, n_exp, 1, d))


def _combine_kernel(grp, dest_ref, dest_next_ref, x_ref, gate_ref, mod_ref, gpost_ref, y_hbm,
                    *rest, split):
    del grp
    outs, (ybuf, sem, acc_s) = rest[:-3], rest[-3:]
    i = pl.program_id(0)
    n = pl.num_programs(0)
    rows = TM * ROW_TILE

    def issue(dref, slot):
        def body(g, c):
            for j in range(DMA_UNROLL):
                t = g * DMA_UNROLL + j
                for kk in range(TOP_K):
                    _tile_copy(y_hbm, dref[t * TOP_K + kk], ybuf.at[slot, kk], t, sem.at[slot]).start(
                        priority=kk % 2)
            return c
        lax.fori_loop(0, TM // DMA_UNROLL, body, 0)

    @pl.when(i == 0)
    def _():
        issue(dest_ref, 0)

    @pl.when(i + 1 < n)
    def _():
        issue(dest_next_ref, (i + 1) % 2)

    slot = i % 2

    for kk in range(TOP_K):
        pltpu.make_async_copy(y_hbm.at[pl.ds(0, rows)], ybuf.at[slot, kk], sem.at[slot]).wait()

    gates = gate_ref[...]
    for s in range(ROW_TILE):
        sl = slice(s * LANES, (s + 1) * LANES)
        acc = gates[:, 0:1] * ybuf[slot, 0, pl.ds(s, TM, stride=ROW_TILE), :]
        for kk in range(1, TOP_K):
            acc = acc + gates[:, kk:kk + 1] * ybuf[slot, kk, pl.ds(s, TM, stride=ROW_TILE), :]
        acc_s[:, sl] = acc
    result = x_ref[...] + mod_ref[5:6] * _rms(acc_s[...], gpost_ref[...])
    if split is None:
        outs[0][...] = result
    else:
        @pl.when(i < split)
        def _():
            outs[0][...] = result

        @pl.when(i >= split)
        def _():
            outs[1][...] = result


def _combine(x, y, dest, gates, mod, grp, g_post, split):
    t, d = x.shape
    n_tiles = t // TM
    tile = pl.BlockSpec((TM, d), lambda i, g: (i, 0))
    if split is None:
        out_specs, out_shape, aliases = tile, jax.ShapeDtypeStruct((t, d), F32), {3: 0}
    else:
        out_specs = [pl.BlockSpec((TM, d), lambda i, g: (jnp.minimum(i, split - 1), 0)),
                     pl.BlockSpec((TM, d), lambda i, g: (jnp.maximum(i - split, 0), 0))]
        out_shape = [jax.ShapeDtypeStruct((split * TM, d), F32),
                     jax.ShapeDtypeStruct((t - split * TM, d), F32)]
        aliases = {}
    return pl.pallas_call(
        functools.partial(_combine_kernel, split=split),
        grid_spec=pltpu.PrefetchScalarGridSpec(
            num_scalar_prefetch=1, grid=(n_tiles,),
            in_specs=[pl.BlockSpec((TM * TOP_K,), lambda i, g: (i,), memory_space=pltpu.SMEM),
                      pl.BlockSpec((TM * TOP_K,), lambda i, g: (jnp.minimum(i + 1, n_tiles - 1),),
                                   memory_space=pltpu.SMEM),
                      tile,
                      pl.BlockSpec((TM, LANES), lambda i, g: (i, 0)),
                      pl.BlockSpec((None, 6, d), lambda i, g: (g[i], 0, 0)),
                      pl.BlockSpec((1, d), lambda i, g: (0, 0)),
                      pl.BlockSpec(memory_space=pl.ANY)],
            out_specs=out_specs,
            scratch_shapes=[pltpu.VMEM((2, TOP_K, TM * ROW_TILE, LANES), F32),
                            pltpu.SemaphoreType.DMA((2,)), pltpu.VMEM((TM, d), F32)]),
        out_shape=out_shape,
        input_output_aliases=aliases,
        compiler_params=_params("arbitrary"),
        name="moe_combine",
    )(grp, dest, dest, x, gates, mod, g_post.reshape(1, d), y)


def _moe_layer(x, mod, grp, grp_mm, layer, g_pre, g_post, w_router, b_router,
               w_gate, b_gate, w_up, b_up, w_down, b_down, split=None):
    t, d = x.shape
    n_exp = w_router.shape[1]
    assert d == ROW_TILE * LANES
    h3, route, gates, cnt = _router(x, mod, grp_mm, g_pre, w_router, b_router)
    idx = route[:, 0:TOP_K]
    rank = route[:, TOP_K:2 * TOP_K]
    counts = cnt[0, :n_exp].astype(I32)
    padded = (counts + MOE_BM - 1) // MOE_BM * MOE_BM
    pad_end = jnp.cumsum(padded)
    pad_start = pad_end - padded
    experts = jnp.arange(n_exp, dtype=I32)
    dest = (rank + jnp.sum(jnp.where(idx[..., None] == experts, pad_start, 0), axis=-1)).reshape(-1)
    n_blocks = t * TOP_K // MOE_BM + n_exp
    xs = _dispatch(h3, dest, pad_start + counts, pad_end, n_blocks * MOE_BM)
    y = _experts(xs, pad_start // MOE_BM, padded // MOE_BM, counts, layer,
                 w_gate, b_gate, w_up, b_up, w_down, b_down)
    return _combine(x, y, dest, gates, mod, grp, g_post, split)


def kernel(x_prompt, x_sample, state_hgrn, cache_k, cache_v, c, c_ctx, w_mod, b_mod, g_pre_mix, g_post_mix, g_pre_ffn, g_post_ffn, hgrn_w_in, hgrn_gamma, hgrn_g_norm, hgrn_w_out, attn_w_qkv, attn_q_gain, attn_k_gain, attn_w_o, moe_w_router, moe_b_router, moe_w_gate, moe_b_gate, moe_w_up, moe_b_up, moe_w_down, moe_b_down):
    bp, tp, d = x_prompt.shape
    bs, ts, _ = x_sample.shape
    depth = w_mod.shape[0]
    n_heads = d // HGRN_DK
    nk = N_KV_HEADS * HEAD_DIM
    assert tp % TM == 0 and ts % MM_TILE == 0 and ts % GRID_W == 0 and d % LANES == 0
    n_ctx = bp * tp
    assert n_ctx % MM_TILE == 0
    ctx_tiles, lat_tiles = tp // TM, ts // TM
    grp = _tile_groups(bp * ctx_tiles, bs, lat_tiles)
    grp_mm = _tile_groups(n_ctx // MM_TILE, bs, ts // MM_TILE)

    rows = -(-(1 + bs) // 8) * 8
    cvec = jnp.zeros((rows, d), F32).at[0].set(c_ctx).at[1:1 + bs].set(c)
    mod_all = _mod_table(cvec, w_mod, b_mod).reshape(depth, rows, 6, d)

    lb_all = jnp.cumsum(jax.nn.softmax(hgrn_gamma.astype(F32), axis=0), axis=0)

    new_states, new_k, new_v = [], [], []
    x = None
    x_parts = (x_prompt, x_sample)
    for l in range(depth):
        mod = mod_all[l]
        if l % 2 == 0:
            a = l // 2
            if x is not None:
                x_parts = (x[:n_ctx].reshape(bp, tp, d), x[n_ctx:].reshape(bs, ts, d))
            hgrn = functools.partial(_hgrn_layer, g_pre=g_pre_mix[l], g_post=g_post_mix[l], w_in=hgrn_w_in[a],
                                     lb=lb_all[l], g_norm=hgrn_g_norm[a], w_out=hgrn_w_out[a])
            assert (bp * ctx_tiles) % (lat_tiles * SEQ_PAR) == 0
            ctx_groups = bp * ctx_tiles // lat_tiles
            stream = jax.ShapeDtypeStruct((ctx_groups + bs, lat_tiles, TM, d), F32)
            stream, s_f, s_b = hgrn(x_parts[0], jnp.broadcast_to(mod[0], (bp, 6, d)),
                                    s0=jnp.zeros((bp, 2, n_heads, HGRN_DK, HGRN_DK), F32),
                                    stream=stream, group0=0)
            stream, _, _ = hgrn(x_parts[1], mod[1:1 + bs], s0=state_hgrn[:, a].astype(F32),
                                stream=stream, group0=ctx_groups)
            x = stream.reshape(n_ctx + bs * ts, d)
            new_states.append(jnp.stack([s_f, s_b], axis=1))
        else:
            m = l // 2
            w_qkv = attn_w_qkv[m].astype(BF16)
            q_c, k_c, v_c = _qkv(x, mod, grp, g_pre_mix[l], w_qkv, attn_q_gain[m], attn_k_gain[m],
                                 0, n_ctx // TM, 0, F32)
            q_l, k_l, v_l = _qkv(x, mod, grp, g_pre_mix[l], w_qkv, attn_q_gain[m], attn_k_gain[m],
                                 n_ctx // TM, bs * ts // TM, ts, BF16)
            o_c = _attention(q_c, k_c, v_c, bp, tp)
            past = cache_k.shape[2]
            o_l = _attention(q_l, k_l, v_l, bs, ts,
                             cache=(cache_k[:, m].reshape(bs, past, nk), cache_v[:, m].reshape(bs, past, nk)))
            new_k.append(k_c.reshape(bp, tp, N_KV_HEADS, HEAD_DIM))
            new_v.append(v_c.reshape(bp, tp, N_KV_HEADS, HEAD_DIM))
            x = _attn_out(x, o_c, o_l, mod, grp_mm, attn_w_o[m].astype(BF16), g_post_mix[l])
        x = _moe_layer(x, mod, grp, grp_mm, l, g_pre_ffn[l], g_post_ffn[l], moe_w_router[l], moe_b_router[l],
                       moe_w_gate, moe_b_gate, moe_w_up, moe_b_up, moe_w_down, moe_b_down,
                       split=bp * ctx_tiles if l == depth - 1 else None)

    y_prompt, y_sample = x
    return (y_prompt.reshape(bp, tp, d), y_sample.reshape(bs, ts, d), jnp.stack(new_states, axis=1),
            jnp.stack(new_k, axis=1), jnp.stack(new_v, axis=1))
```

```python
import functools

import numpy as np
import jax
import jax.numpy as jnp
from jax import lax
from jax.experimental import pallas as pl
from jax.experimental.pallas import tpu as pltpu

F32 = jnp.float32
BF16 = jnp.bfloat16
I32 = jnp.int32

EPS = 1e-6
GRID_W = 64
ROPE_THETA = 10000.0
HGRN_DK = 128
HEAD_DIM = 128
N_KV_HEADS = 2
TOP_K = 4
SWIGLU_LIMIT = 7.0
SWIGLU_ALPHA = 1.702

LANES = 128
TM = 256
MM_TILE = 1024
ATT_TQ = 1024
RANK_BLOCK = 256
GLA_CHUNK = 64
GLA_GROUP = 4
SEQ_PAR = 2
MOE_BM = 512
FF_CHUNK = 512
ROW_TILE = 8
DMA_UNROLL = 16
ZERO_GROUP = 16
MOD_COLS = 512
VMEM_LIMIT = 56 * 1024 * 1024
NEG_BIG = -1e30

NT_DIMS = (((1,), (1,)), ((), ()))
TN_DIMS = (((0,), (0,)), ((), ()))


def _params(*sem):
    return pltpu.CompilerParams(dimension_semantics=sem, vmem_limit_bytes=VMEM_LIMIT)


def _sigmoid(x):
    return 1.0 / (1.0 + jnp.exp(-x))


def _silu(x):
    return x * _sigmoid(x)


def _rms(x, gain):
    ms = jnp.mean(x * x, axis=-1, keepdims=True)
    return x * lax.rsqrt(ms + EPS) * gain


def _dot(a, b):
    return jnp.dot(a, b, preferred_element_type=F32)


def _mod_kernel(c_ref, w_ref, b_ref, o_ref):
    s = _silu(c_ref[...]).astype(BF16)
    o_ref[...] = _dot(s, w_ref[...].astype(BF16)) + b_ref[...]


def _mod_table(cvec, w_mod, b_mod):
    depth, d, n = w_mod.shape
    rows = cvec.shape[0]
    return pl.pallas_call(
        _mod_kernel,
        grid=(depth, n // MOD_COLS),
        in_specs=[
            pl.BlockSpec((rows, d), lambda l, j: (0, 0)),
            pl.BlockSpec((None, d, MOD_COLS), lambda l, j: (l, 0, j)),
            pl.BlockSpec((None, 1, MOD_COLS), lambda l, j: (l, 0, j)),
        ],
        out_specs=pl.BlockSpec((None, rows, MOD_COLS), lambda l, j: (l, 0, j)),
        out_shape=jax.ShapeDtypeStruct((depth, rows, n), F32),
        compiler_params=_params("arbitrary", "arbitrary"),
        name="mod_table",
    )(cvec, w_mod, b_mod.reshape(depth, 1, n))


def _gla_tiles(q_s, k_s, lf_s, v_s, st_ref, o_s, direction):
    c = GLA_CHUNK
    span = GLA_GROUP * c
    d_model = q_s.shape[1]
    n_heads = d_model // HGRN_DK
    ri = lax.broadcasted_iota(I32, (span, span), 0)
    ci = lax.broadcasted_iota(I32, (span, span), 1)
    causal = (ci <= ri) if direction == 0 else (ci >= ri)
    shift = c.bit_length() - 1
    assert c == 1 << shift
    keep = jnp.logical_and(causal, (ri >> shift) == (ci >> shift))
    ones_tri = jnp.where(keep, 1.0, 0.0).astype(BF16)
    groups = list(range(TM // span))
    members = list(range(GLA_GROUP))
    if direction == 1:
        groups.reverse()
        members.reverse()
    for grp, seq in [(grp, seq) for grp in groups for seq in range(SEQ_PAR)]:
        row0 = seq * TM + grp * span
        rows = pl.ds(row0, span)
        lf = lf_s[rows, :]
        lf_hi = lf.astype(BF16)
        lf_lo = (lf - lf_hi.astype(F32)).astype(BF16)
        cum_all = _dot(ones_tri, lf_hi) + _dot(ones_tri, lf_lo)
        q_dec, k_inv, q_in, k_end, e_tot = [], [], [], [], []
        for hf in range(GLA_GROUP):
            cum = cum_all[hf * c:(hf + 1) * c]
            part = pl.ds(row0 + hf * c, c)
            if direction == 0:
                btot, bmid = cum[c - 1:c], cum[c // 2 - 1:c // 2]
            else:
                btot, bmid = cum[0:1], cum[c // 2:c // 2 + 1]
            qd = q_s[part, :] * jnp.exp(cum - bmid)
            ki = k_s[part, :] * jnp.exp(bmid - cum)
            q_in.append((qd * jnp.exp(bmid)).astype(BF16))
            k_end.append((ki * jnp.exp(btot - bmid)).astype(BF16))
            q_dec.append(qd.astype(BF16))
            k_inv.append(ki.astype(BF16))
            e_tot.append(jnp.exp(btot))
        q_dec = jnp.concatenate(q_dec, axis=0)
        k_inv = jnp.concatenate(k_inv, axis=0)
        vc = v_s[rows, :].astype(BF16)
        for h in range(n_heads):
            sl = slice(h * HGRN_DK, (h + 1) * HGRN_DK)
            a = lax.dot_general(q_dec[:, sl], k_inv[:, sl], NT_DIMS, preferred_element_type=F32)
            a = jnp.where(keep, a, 0.0).astype(BF16)
            o_intra = _dot(a, vc[:, sl])
            for hf in members:
                part = slice(hf * c, (hf + 1) * c)
                st = st_ref[seq, h]
                o_s[pl.ds(row0 + hf * c, c), sl] = o_intra[part] + lax.dot_general(
                    q_in[hf][:, sl], st.astype(BF16), NT_DIMS, preferred_element_type=F32)
                st_ref[seq, h] = st * e_tot[hf][:, sl] + lax.dot_general(
                    vc[part, sl], k_end[hf][:, sl], TN_DIMS, preferred_element_type=F32)


def _hgrn_gates(h, w_ref, lb, q_s, k_s, lf_s, v_s):
    d = h.shape[1]
    q_s[...] = _silu(_dot(h, w_ref[:, 0:d]))
    f = lb + (1.0 - lb) * _sigmoid(_dot(h, w_ref[:, d:2 * d]))
    lf_s[...] = jnp.log(f)
    k_s[...] = 1.0 - f
    v_s[...] = _dot(h, w_ref[:, 2 * d:3 * d])


def _load_state(s0_ref, st_s):
    @pl.when(pl.program_id(1) == 0)
    def _():
        for s in range(st_s.shape[0]):
            for h in range(st_s.shape[1]):
                st_s[s, h] = s0_ref[s, h].T


def _store_state(sfin_ref, st_s):
    @pl.when(pl.program_id(1) == pl.num_programs(1) - 1)
    def _():
        for s in range(st_s.shape[0]):
            for h in range(st_s.shape[1]):
                sfin_ref[s, h] = st_s[s, h].T


def _hgrn_norm(x_ref, mod_ref, gpre_ref):
    hs = [(_rms(x_ref[s], gpre_ref[...]) * (1.0 + mod_ref[s, 1:2]) + mod_ref[s, 0:1]).astype(BF16)
          for s in range(SEQ_PAR)]
    return jnp.concatenate(hs, axis=0)


def _hgrn_bwd_kernel(x_ref, mod_ref, gpre_ref, w_ref, lb_ref, s0_ref,
                     ob_ref, qb_ref, vb_ref, sfin_ref, st_s, o_s, q_s, k_s, lf_s, v_s):
    _load_state(s0_ref, st_s)
    h = _hgrn_norm(x_ref, mod_ref, gpre_ref)
    _hgrn_gates(h, w_ref, lb_ref[...], q_s, k_s, lf_s, v_s)
    for s in range(SEQ_PAR):
        rows = slice(s * TM, (s + 1) * TM)
        qb_ref[s] = q_s[rows, :].astype(BF16)
        vb_ref[s] = v_s[rows, :].astype(BF16)
    _gla_tiles(q_s, k_s, lf_s, v_s, st_s, o_s, 1)
    for s in range(SEQ_PAR):
        ob_ref[s] = o_s[s * TM:(s + 1) * TM, :].astype(BF16)
    _store_state(sfin_ref, st_s)


def _hgrn_fwd_kernel(x_ref, mod_ref, gpre_ref, w_ref, lb_ref, s0_ref, ob_ref, qb_ref, vb_ref,
                     gnorm_ref, wout_ref, gpost_ref, *rest):
    xo_ref, sfin_ref, st_s, o_s, q_s, k_s, lf_s, v_s = rest[-8:]
    _load_state(s0_ref, st_s)
    d = x_ref.shape[2]
    h = _hgrn_norm(x_ref, mod_ref, gpre_ref)
    lb = lb_ref[...]
    f = lb + (1.0 - lb) * _sigmoid(_dot(h, w_ref[:, 0:d]))
    lf_s[...] = jnp.log(f)
    k_s[...] = 1.0 - f
    for s in range(SEQ_PAR):
        rows = slice(s * TM, (s + 1) * TM)
        q_s[rows, :] = qb_ref[s].astype(F32)
        v_s[rows, :] = vb_ref[s].astype(F32)
    _gla_tiles(q_s, k_s, lf_s, v_s, st_s, o_s, 0)
    _store_state(sfin_ref, st_s)
    gate = _silu(_dot(h, w_ref[:, d:2 * d]))
    gnorm = gnorm_ref[...]
    o_b = jnp.concatenate([ob_ref[s] for s in range(SEQ_PAR)], axis=0).astype(F32)
    for hh in range(d // HGRN_DK):
        sl = slice(hh * HGRN_DK, (hh + 1) * HGRN_DK)
        o_s[:, sl] = _rms(o_s[:, sl] + o_b[:, sl], gnorm) * gate[:, sl]
    out = _dot(o_s[...].astype(BF16), wout_ref[...])
    for s in range(SEQ_PAR):
        xo_ref[s] = x_ref[s] + mod_ref[s, 2:3] * _rms(out[s * TM:(s + 1) * TM], gpost_ref[...])


def _tile_groups(n_ctx_tiles, n_lat_seq, lat_tiles):
    g = [0] * n_ctx_tiles
    for b in range(n_lat_seq):
        g += [1 + b] * lat_tiles
    return jnp.asarray(np.array(g, dtype=np.int32))


def _hgrn_layer(x, mod_seq, g_pre, g_post, w_in, lb, g_norm, w_out, s0, stream, group0):
    n_seq, seq_len, d = x.shape
    n_heads = d // HGRN_DK
    tiles = seq_len // TM
    per_group = stream.shape[1]
    assert n_seq % SEQ_PAR == 0 and per_group % SEQ_PAR == 0 and tiles in (1, per_group)
    x4 = x.reshape(n_seq, tiles, TM, d)
    w_bwd = jnp.concatenate([w_in[:, 0:d], w_in[:, 2 * d:3 * d], w_in[:, 3 * d:4 * d]], axis=1).astype(BF16)
    w_fwd = jnp.concatenate([w_in[:, d:2 * d], w_in[:, 4 * d:5 * d]], axis=1).astype(BF16)

    def specs(direction):
        def tile_spec():
            if direction == 0:
                return pl.BlockSpec((SEQ_PAR, None, TM, d), lambda p, j: (p, j, 0, 0))
            return pl.BlockSpec((SEQ_PAR, None, TM, d), lambda p, j: (p, tiles - 1 - j, 0, 0))
        state = pl.BlockSpec((SEQ_PAR, n_heads, HGRN_DK, HGRN_DK), lambda p, j: (p, 0, 0, 0))
        mods = pl.BlockSpec((SEQ_PAR, 6, d), lambda p, j: (p, 0, 0))
        vec = pl.BlockSpec((1, d), lambda p, j: (0, 0))
        return tile_spec, state, mods, vec

    def const(shape):
        return pl.BlockSpec(shape, lambda p, j: (0, 0))

    rows = SEQ_PAR * TM
    scratch = ([pltpu.VMEM((SEQ_PAR, n_heads, HGRN_DK, HGRN_DK), F32)]
               + [pltpu.VMEM((rows, d), F32)] * 5)
    grid = (n_seq // SEQ_PAR, tiles)
    state_shape = jax.ShapeDtypeStruct((n_seq, 2, n_heads, HGRN_DK, HGRN_DK), F32)

    def final_state(direction):
        return pl.BlockSpec((SEQ_PAR, None, n_heads, HGRN_DK, HGRN_DK), lambda p, j: (p, direction, 0, 0, 0))

    tile_spec, state, mods, vec = specs(1)
    o_bwd, q_act, v_act, states = pl.pallas_call(
        _hgrn_bwd_kernel,
        grid=grid,
        in_specs=[tile_spec(), mods, vec, const((d, 3 * d)), vec, state],
        out_specs=[tile_spec(), tile_spec(), tile_spec(), final_state(1)],
        scratch_shapes=scratch,
        out_shape=[jax.ShapeDtypeStruct(x4.shape, BF16)] * 3 + [state_shape],
        compiler_params=_params("arbitrary", "arbitrary"),
        name="hgrn_bwd",
    )(x4, mod_seq, g_pre.reshape(1, d), w_bwd, lb[1].reshape(1, d), s0[:, 1])

    tile_spec, state, mods, vec = specs(0)
    if tiles == per_group:
        stream_spec = pl.BlockSpec((SEQ_PAR, None, TM, d), lambda p, j: (group0 // SEQ_PAR + p, j, 0, 0))
    else:
        per = per_group // SEQ_PAR
        stream_spec = pl.BlockSpec((None, SEQ_PAR, TM, d), lambda p, j: (group0 + p // per, p % per, 0, 0))
    in_specs = [tile_spec(), mods, vec, const((d, 2 * d)), vec, state, tile_spec(), tile_spec(), tile_spec(),
                const((1, HGRN_DK)), const((d, d)), vec]
    args = [x4, mod_seq, g_pre.reshape(1, d), w_fwd, lb[0].reshape(1, d), s0[:, 0],
            o_bwd, q_act, v_act, g_norm.reshape(1, HGRN_DK), w_out.astype(BF16), g_post.reshape(1, d)]
    aliases = {len(args): 1}
    in_specs.append(pl.BlockSpec(memory_space=pl.ANY))
    args.append(states)
    if not isinstance(stream, jax.ShapeDtypeStruct):
        aliases[len(args)] = 0
        in_specs.append(pl.BlockSpec(memory_space=pl.ANY))
        args.append(stream)
    stream, states = pl.pallas_call(
        _hgrn_fwd_kernel,
        grid=grid,
        in_specs=in_specs,
        out_specs=[stream_spec, final_state(0)],
        scratch_shapes=scratch,
        out_shape=[jax.ShapeDtypeStruct(stream.shape, F32), state_shape],
        input_output_aliases=aliases,
        compiler_params=_params("arbitrary", "arbitrary"),
        name="hgrn_fwd",
    )(*args)
    return stream, states


def _rope_tables(n_tok):
    axis_dim = HEAD_DIM // 2
    half = axis_dim // 2
    pos = np.arange(n_tok)
    row, col = pos // GRID_W, pos % GRID_W
    inv_freq = ROPE_THETA ** (-np.arange(0, axis_dim, 2, dtype=np.float64) / axis_dim)
    lane = np.arange(HEAD_DIM)
    p = np.where(lane[None, :] < axis_dim, row[:, None], col[:, None]).astype(np.float64)
    ang = p * inv_freq[(lane % axis_dim) % half][None, :]
    sign = np.where((lane % axis_dim) < half, -1.0, 1.0)[None, :]
    return jnp.asarray(np.cos(ang), F32), jnp.asarray(np.sin(ang) * sign, F32)


def _rope(xh, cos, sin_signed):
    axis_dim = HEAD_DIM // 2
    half = axis_dim // 2
    lane = lax.broadcasted_iota(I32, xh.shape, 1)
    upper = pltpu.roll(xh, HEAD_DIM - half, 1)
    lower = pltpu.roll(xh, half, 1)
    partner = jnp.where((lane % axis_dim) < half, upper, lower)
    return xh * cos + partner * sin_signed


def _qkv_kernel(grp, x_ref, mod_ref, gpre_ref, w_ref, qg_ref, kg_ref, *rest, rope):
    del grp
    if rope:
        cos_ref, sin_ref, q_out, k_out, v_out = rest
    else:
        q_out, k_out, v_out = rest
    d = x_ref.shape[1]
    nk = N_KV_HEADS * HEAD_DIM
    h = (_rms(x_ref[...], gpre_ref[...]) * (1.0 + mod_ref[1:2]) + mod_ref[0:1]).astype(BF16)
    q = _dot(h, w_ref[:, 0:d])
    k = _dot(h, w_ref[:, d:d + nk])
    v_out[...] = _dot(h, w_ref[:, d + nk:d + 2 * nk]).astype(v_out.dtype)
    scale = HEAD_DIM ** -0.5
    for hh in range(d // HEAD_DIM):
        sl = slice(hh * HEAD_DIM, (hh + 1) * HEAD_DIM)
        qh = _rms(q[:, sl], qg_ref[...])
        if rope:
            qh = _rope(qh, cos_ref[...], sin_ref[...])
        q_out[:, sl] = (qh * scale).astype(q_out.dtype)
    for hh in range(N_KV_HEADS):
        sl = slice(hh * HEAD_DIM, (hh + 1) * HEAD_DIM)
        kh = _rms(k[:, sl], kg_ref[...])
        if rope:
            kh = _rope(kh, cos_ref[...], sin_ref[...])
        k_out[:, sl] = kh.astype(k_out.dtype)


def _qkv(x, mod, grp, g_pre, w_qkv, q_gain, k_gain, tile0, n_tiles, rope_len, kv_dtype):
    t, d = x.shape
    nk = N_KV_HEADS * HEAD_DIM
    n_out = n_tiles * TM
    tile_in = pl.BlockSpec((TM, d), lambda i, g: (tile0 + i, 0))
    in_specs = [tile_in,
                pl.BlockSpec((None, 6, d), lambda i, g: (g[tile0 + i], 0, 0)),
                pl.BlockSpec((1, d), lambda i, g: (0, 0)),
                pl.BlockSpec((d, d + 2 * nk), lambda i, g: (0, 0)),
                pl.BlockSpec((1, HEAD_DIM), lambda i, g: (0, 0)),
                pl.BlockSpec((1, HEAD_DIM), lambda i, g: (0, 0))]
    args = [grp, x, mod, g_pre.reshape(1, d), w_qkv, q_gain.reshape(1, HEAD_DIM), k_gain.reshape(1, HEAD_DIM)]
    if rope_len:
        per_seq = rope_len // TM
        cos, sin = _rope_tables(rope_len)
        in_specs += [pl.BlockSpec((TM, HEAD_DIM), lambda i, g: (i % per_seq, 0))] * 2
        args += [cos, sin]
    return pl.pallas_call(
        functools.partial(_qkv_kernel, rope=bool(rope_len)),
        grid_spec=pltpu.PrefetchScalarGridSpec(
            num_scalar_prefetch=1, grid=(n_tiles,), in_specs=in_specs,
            out_specs=[pl.BlockSpec((TM, d), lambda i, g: (i, 0)),
                       pl.BlockSpec((TM, nk), lambda i, g: (i, 0)),
                       pl.BlockSpec((TM, nk), lambda i, g: (i, 0))]),
        out_shape=[jax.ShapeDtypeStruct((n_out, d), BF16),
                   jax.ShapeDtypeStruct((n_out, nk), kv_dtype),
                   jax.ShapeDtypeStruct((n_out, nk), kv_dtype)],
        compiler_params=_params("arbitrary"),
        name="qkv_rope" if rope_len else "qkv",
    )(*args)


def _attn_kernel(q_ref, *rest, n_pieces, n_sub):
    kv, o_ref = rest[:-1], rest[-1]
    q_rows = q_ref.shape[0] // n_sub
    kv_rows = kv[-1].shape[0] // n_sub
    group = q_ref.shape[1] // HEAD_DIM
    for sub in range(n_sub):
        qr = slice(sub * q_rows, (sub + 1) * q_rows)
        if n_sub == 1:
            ks = [kv[2 * p][...].astype(BF16) for p in range(n_pieces)]
            vs = [kv[2 * p + 1][...].astype(BF16) for p in range(n_pieces)]
        else:
            kr = slice(sub * kv_rows, (sub + 1) * kv_rows)
            ks, vs = [kv[0][kr, :].astype(BF16)], [kv[1][kr, :].astype(BF16)]
        for g in range(group):
            sl = slice(g * HEAD_DIM, (g + 1) * HEAD_DIM)
            qh = q_ref[qr, sl]
            ss = [lax.dot_general(qh, kp, NT_DIMS, preferred_element_type=F32) for kp in ks]
            m = ss[0].max(axis=-1, keepdims=True)
            for s in ss[1:]:
                m = jnp.maximum(m, s.max(axis=-1, keepdims=True))
            ps = [jnp.exp(s - m) for s in ss]
            denom = ps[0].sum(axis=-1, keepdims=True)
            for p in ps[1:]:
                denom = denom + p.sum(axis=-1, keepdims=True)
            o = _dot(ps[0].astype(BF16), vs[0])
            for p, vp in zip(ps[1:], vs[1:]):
                o = o + _dot(p.astype(BF16), vp)
            o_ref[qr, sl] = (o / denom).astype(o_ref.dtype)


def _attention(q, k_new, v_new, n_seq, seq_len, cache=None):
    t, d = q.shape
    group_w = d // N_KV_HEADS
    n_sub = 1
    if cache is None and seq_len < ATT_TQ:
        n_sub = max(s for s in range(1, ATT_TQ // seq_len + 1) if n_seq % s == 0)
    if n_sub > 1:
        n_seq, seq_len = n_seq // n_sub, seq_len * n_sub
    tq = min(ATT_TQ, seq_len)
    nq = seq_len // tq
    q_spec = pl.BlockSpec((tq, group_w), lambda b, i, kv: (b * nq + i, kv))
    new_spec = pl.BlockSpec((seq_len, HEAD_DIM), lambda b, i, kv: (b, kv))
    in_specs, args = [q_spec], [q]
    if cache is not None:
        past = cache[0].shape[1]
        c_spec = pl.BlockSpec((None, past, HEAD_DIM), lambda b, i, kv: (b, 0, kv))
        in_specs += [c_spec, c_spec]
        args += list(cache)
    in_specs += [new_spec, new_spec]
    args += [k_new, v_new]
    return pl.pallas_call(
        functools.partial(_attn_kernel, n_pieces=len(args) // 2, n_sub=n_sub),
        grid=(n_seq, nq, N_KV_HEADS),
        in_specs=in_specs,
        out_specs=q_spec,
        out_shape=jax.ShapeDtypeStruct((t, d), BF16),
        compiler_params=_params("arbitrary", "arbitrary", "arbitrary"),
        name="attention",
    )(*args)


def _attn_out_kernel(grp, x_ref, oc_ref, ol_ref, mod_ref, w_ref, gpost_ref, xo_ref, *, n_ctx_tiles):
    del grp
    o = jnp.where(pl.program_id(0) < n_ctx_tiles, oc_ref[...], ol_ref[...])
    out = _dot(o, w_ref[...])
    xo_ref[...] = x_ref[...] + mod_ref[2:3] * _rms(out, gpost_ref[...])


def _attn_out(x, o_ctx, o_lat, mod, grp, w_o, g_post):
    t, d = x.shape
    nct = o_ctx.shape[0] // MM_TILE
    tile = pl.BlockSpec((MM_TILE, d), lambda i, g: (i, 0))
    return pl.pallas_call(
        functools.partial(_attn_out_kernel, n_ctx_tiles=nct),
        grid_spec=pltpu.PrefetchScalarGridSpec(
            num_scalar_prefetch=1, grid=(t // MM_TILE,),
            in_specs=[tile,
                      pl.BlockSpec((MM_TILE, d), lambda i, g: (jnp.minimum(i, nct - 1), 0)),
                      pl.BlockSpec((MM_TILE, d), lambda i, g: (jnp.maximum(i - nct, 0), 0)),
                      pl.BlockSpec((None, 6, d), lambda i, g: (g[i], 0, 0)),
                      pl.BlockSpec((d, d), lambda i, g: (0, 0)),
                      pl.BlockSpec((1, d), lambda i, g: (0, 0))],
            out_specs=tile),
        out_shape=jax.ShapeDtypeStruct((t, d), F32),
        input_output_aliases={1: 0},
        compiler_params=_params("arbitrary"),
        name="attn_out",
    )(grp, x, o_ctx, o_lat, mod, w_o, g_post.reshape(1, d))


def _to_token_tiles(ref, x):
    rows = x.shape[0]
    for s in range(x.shape[1] // LANES):
        ref[pl.ds(s, rows, stride=ROW_TILE), :] = x[:, s * LANES:(s + 1) * LANES]


def _from_token_tiles(ref, rows):
    n = ROW_TILE
    return jnp.concatenate([ref[pl.ds(s, rows, stride=n), :] for s in range(n)], axis=1)


def _router_kernel(grp, x_ref, mod_ref, g_ref, wr_ref, br_ref,
                   h_ref, route_ref, gate_ref, cnt_ref, carry):
    del grp
    @pl.when(pl.program_id(0) == 0)
    def _():
        carry[...] = jnp.zeros_like(carry)
    h = _rms(x_ref[...], g_ref[...]) * (1.0 + mod_ref[4:5]) + mod_ref[3:4]
    _to_token_tiles(h_ref, h)
    logits = _dot(h.astype(BF16), wr_ref[...]) + br_ref[...]
    lane = lax.broadcasted_iota(I32, logits.shape, 1)
    lane_f = lane.astype(F32)
    vals, idxs = [], []
    work = logits
    for _ in range(TOP_K):
        m = work.max(axis=-1, keepdims=True)
        idx = jnp.where(work == m, lane_f, float(LANES)).min(axis=-1, keepdims=True).astype(I32)
        vals.append(m)
        idxs.append(idx)
        work = jnp.where(lane == idx, -jnp.inf, work)
    es = [jnp.exp(v - vals[0]) for v in vals]
    denom = es[0]
    for e in es[1:]:
        denom = denom + e
    onehot = jnp.zeros(logits.shape, F32)
    for idx in idxs:
        onehot = onehot + jnp.where(lane == idx, 1.0, 0.0)
    ri = lax.broadcasted_iota(I32, (RANK_BLOCK, RANK_BLOCK), 0)
    ci = lax.broadcasted_iota(I32, (RANK_BLOCK, RANK_BLOCK), 1)
    before = jnp.where(ci < ri, 1.0, 0.0).astype(BF16)
    counts = carry[0:1]
    ranks = []
    for r0 in range(0, MM_TILE, RANK_BLOCK):
        part = onehot[r0:r0 + RANK_BLOCK]
        ranks.append(_dot(before, part.astype(BF16)) + counts)
        counts = counts + part.sum(axis=0, keepdims=True)
    rank_all = jnp.concatenate(ranks, axis=0)
    route = jnp.zeros(logits.shape, I32)
    gate = jnp.zeros(logits.shape, F32)
    for kk in range(TOP_K):
        rank = jnp.where(lane == idxs[kk], rank_all, 0.0).sum(axis=-1, keepdims=True)
        route = jnp.where(lane == kk, idxs[kk], route)
        route = jnp.where(lane == TOP_K + kk, rank.astype(I32), route)
        gate = jnp.where(lane == kk, es[kk] / denom, gate)
    route_ref[...] = route
    gate_ref[...] = gate
    carry[...] = jnp.broadcast_to(counts, carry.shape)
    cnt_ref[...] = carry[...]


def _router(x, mod, grp, g_pre, w_router, b_router):
    t, d = x.shape
    n_exp = w_router.shape[1]
    wr = jnp.pad(w_router.astype(BF16), ((0, 0), (0, LANES - n_exp)))
    br = jnp.pad(b_router.reshape(1, n_exp), ((0, 0), (0, LANES - n_exp)), constant_values=NEG_BIG)
    tile = pl.BlockSpec((MM_TILE, d), lambda i, g: (i, 0))
    lane_tile = pl.BlockSpec((MM_TILE, LANES), lambda i, g: (i, 0))
    return pl.pallas_call(
        _router_kernel,
        grid_spec=pltpu.PrefetchScalarGridSpec(
            num_scalar_prefetch=1, grid=(t // MM_TILE,),
            in_specs=[tile,
                      pl.BlockSpec((None, 6, d), lambda i, g: (g[i], 0, 0)),
                      pl.BlockSpec((1, d), lambda i, g: (0, 0)),
                      pl.BlockSpec((d, LANES), lambda i, g: (0, 0)),
                      pl.BlockSpec((1, LANES), lambda i, g: (0, 0))],
            out_specs=[pl.BlockSpec((MM_TILE * ROW_TILE, LANES), lambda i, g: (i, 0)), lane_tile, lane_tile,
                       pl.BlockSpec((8, LANES), lambda i, g: (0, 0))],
            scratch_shapes=[pltpu.VMEM((8, LANES), F32)]),
        out_shape=[jax.ShapeDtypeStruct((t * ROW_TILE, LANES), F32),
                   jax.ShapeDtypeStruct((t, LANES), I32),
                   jax.ShapeDtypeStruct((t, LANES), F32),
                   jax.ShapeDtypeStruct((8, LANES), F32)],
        compiler_params=_params("arbitrary"),
        name="router",
    )(grp, x, mod, g_pre.reshape(1, d), wr, br)


def _tile_copy(src, src_row, dst, dst_row, sem):
    n = ROW_TILE
    return pltpu.make_async_copy(src.at[pl.ds(pl.multiple_of(src_row * n, n), n)],
                                 dst.at[pl.ds(pl.multiple_of(dst_row * n, n), n)], sem)


def _dispatch_kernel(pad_lo, pad_hi, dest_ref, h_ref, xs_hbm, zero_s, sem, zsem):
    def issue(g, c):
        for j in range(DMA_UNROLL):
            t = g * DMA_UNROLL + j
            for kk in range(TOP_K):
                _tile_copy(h_ref, t, xs_hbm, dest_ref[t * TOP_K + kk], sem).start(priority=kk % 2)
        return c
    lax.fori_loop(0, TM // DMA_UNROLL, issue, 0)

    @pl.when(pl.program_id(0) == 0)
    def _():
        zero_s[...] = jnp.zeros_like(zero_s)
        group_rows = ZERO_GROUP * ROW_TILE

        def single(s):
            return _tile_copy(zero_s, 0, xs_hbm, s, zsem)

        def group(g):
            return pltpu.make_async_copy(
                zero_s, xs_hbm.at[pl.ds(pl.multiple_of(g * group_rows, group_rows), group_rows)], zsem)

        def per_expert(e, c):
            lo, hi = pad_lo[e], pad_hi[e]
            mid = jnp.minimum(hi, (lo + ZERO_GROUP - 1) // ZERO_GROUP * ZERO_GROUP)
            g_lo, g_hi = mid // ZERO_GROUP, hi // ZERO_GROUP
            lax.fori_loop(lo, mid, lambda s, c2: (single(s).start(), c2)[1], 0)
            lax.fori_loop(g_lo, g_hi, lambda g, c2: (group(g).start(), c2)[1], 0)
            lax.fori_loop(lo, mid, lambda s, c2: (single(s).wait(), c2)[1], 0)
            lax.fori_loop(g_lo, g_hi, lambda g, c2: (group(g).wait(), c2)[1], 0)
            return c
        lax.fori_loop(0, pad_lo.shape[0], per_expert, 0)

    for _ in range(TOP_K):
        pltpu.make_async_copy(h_ref, xs_hbm.at[pl.ds(0, TM * ROW_TILE)], sem).wait()


def _dispatch(h3, dest, pad_lo, pad_hi, n_slots):
    t = h3.shape[0] // ROW_TILE
    return pl.pallas_call(
        _dispatch_kernel,
        grid_spec=pltpu.PrefetchScalarGridSpec(
            num_scalar_prefetch=2, grid=(t // TM,),
            in_specs=[pl.BlockSpec((TM * TOP_K,), lambda i, lo, hi: (i,), memory_space=pltpu.SMEM),
                      pl.BlockSpec((TM * ROW_TILE, LANES), lambda i, lo, hi: (i, 0))],
            out_specs=pl.BlockSpec(memory_space=pl.ANY),
            scratch_shapes=[pltpu.VMEM((ZERO_GROUP * ROW_TILE, LANES), F32),
                            pltpu.SemaphoreType.DMA, pltpu.SemaphoreType.DMA]),
        out_shape=jax.ShapeDtypeStruct((n_slots * ROW_TILE, LANES), F32),
        compiler_params=_params("arbitrary"),
        name="moe_dispatch",
    )(pad_lo, pad_hi, dest, h3)


def _expert_kernel(blk0, nblk, cnt, xs_hbm, wg_ref, bg_ref, wu_ref, bu_ref, wd_ref, bd_ref,
                   y_hbm, wg_s, wu_s, wd_s, xbuf, ybuf, in_sem, out_sem):
    e = pl.program_id(0)
    first = blk0[e]
    n = nblk[e]
    rows = MOE_BM * ROW_TILE

    def in_copy(b, slot):
        return pltpu.make_async_copy(
            xs_hbm.at[pl.ds(pl.multiple_of((first + b) * rows, rows), rows)], xbuf.at[slot], in_sem.at[slot])

    def out_copy(b, slot):
        return pltpu.make_async_copy(
            ybuf.at[slot], y_hbm.at[pl.ds(pl.multiple_of((first + b) * rows, rows), rows)], out_sem.at[slot])

    @pl.when(e == 0)
    def _():
        ybuf[...] = jnp.zeros_like(ybuf)

    @pl.when(n > 0)
    def _():
        in_copy(0, 0).start()
        wg_s[...] = wg_ref[...].astype(BF16)
        wu_s[...] = wu_ref[...].astype(BF16)
        wd_s[...] = wd_ref[...].astype(BF16)

    def block(b, c):
        slot = b % 2
        in_copy(b, slot).wait()

        @pl.when(b + 1 < n)
        def _():
            in_copy(b + 1, 1 - slot).start()

        @pl.when(b >= 2)
        def _():
            out_copy(b - 2, slot).wait()

        def mlp(n_rows):
            x = _from_token_tiles(xbuf.at[slot], n_rows).astype(BF16)
            y = bd_ref[...]
            for c0 in range(0, wg_s.shape[1], FF_CHUNK):
                cols = slice(c0, c0 + FF_CHUNK)
                hg = jnp.minimum(_dot(x, wg_s[:, cols]) + bg_ref[:, cols], SWIGLU_LIMIT)
                hu = jnp.clip(_dot(x, wu_s[:, cols]) + bu_ref[:, cols], -SWIGLU_LIMIT, SWIGLU_LIMIT)
                hh = (hu + 1.0) * hg * _sigmoid(SWIGLU_ALPHA * hg)
                y = y + _dot(hh.astype(BF16), wd_s[cols, :])
            _to_token_tiles(ybuf.at[slot], y)

        real_rows = cnt[e] - b * MOE_BM

        @pl.when(real_rows > MOE_BM // 2)
        def _():
            mlp(MOE_BM)

        @pl.when(real_rows <= MOE_BM // 2)
        def _():
            mlp(MOE_BM // 2)

        out_copy(b, slot).start()
        return c
    lax.fori_loop(0, n, block, 0)

    @pl.when(n >= 2)
    def _():
        out_copy(n - 2, n % 2).wait()

    @pl.when(n >= 1)
    def _():
        out_copy(n - 1, (n - 1) % 2).wait()


def _experts(xs, blk0, nblk, counts, layer, w_gate, b_gate, w_up, b_up, w_down, b_down):
    depth, n_exp, d, f = w_gate.shape

    def w_spec(k, n):
        return pl.BlockSpec((None, None, k, n), lambda e, b0, nb, cnt: (layer, e, 0, 0))

    rows = MOE_BM * ROW_TILE
    return pl.pallas_call(
        _expert_kernel,
        grid_spec=pltpu.PrefetchScalarGridSpec(
            num_scalar_prefetch=3, grid=(n_exp,),
            in_specs=[pl.BlockSpec(memory_space=pl.ANY),
                      w_spec(d, f), w_spec(1, f), w_spec(d, f), w_spec(1, f), w_spec(f, d), w_spec(1, d)],
            out_specs=pl.BlockSpec(memory_space=pl.ANY),
            scratch_shapes=[pltpu.VMEM((d, f), BF16), pltpu.VMEM((d, f), BF16), pltpu.VMEM((f, d), BF16),
                            pltpu.VMEM((2, rows, LANES), F32), pltpu.VMEM((2, rows, LANES), F32),
                            pltpu.SemaphoreType.DMA((2,)), pltpu.SemaphoreType.DMA((2,))]),
        out_shape=jax.ShapeDtypeStruct(xs.shape, F32),
        compiler_params=_params("arbitrary"),
        name="moe_experts",
    )(blk0, nblk, counts, xs, w_gate, b_gate.reshape(depth, n_exp, 1, f), w_up, b_up.reshape(depth, n_exp, 1, f),
      w_down, b_down.reshape(depth, n_exp, 1, d))


def _combine_kernel(grp, dest_ref, dest_next_ref, x_ref, gate_ref, mod_ref, gpost_ref, y_hbm,
                    *rest, split):
    del grp
    outs, (ybuf, sem, acc_s) = rest[:-3], rest[-3:]
    i = pl.program_id(0)
    n = pl.num_programs(0)
    rows = TM * ROW_TILE

    def issue(dref, slot):
        def body(g, c):
            for j in range(DMA_UNROLL):
                t = g * DMA_UNROLL + j
                for kk in range(TOP_K):
                    _tile_copy(y_hbm, dref[t * TOP_K + kk], ybuf.at[slot, kk], t, sem.at[slot]).start(
                        priority=kk % 2)
            return c
        lax.fori_loop(0, TM // DMA_UNROLL, body, 0)

    @pl.when(i == 0)
    def _():
        issue(dest_ref, 0)

    @pl.when(i + 1 < n)
    def _():
        issue(dest_next_ref, (i + 1) % 2)

    slot = i % 2

    for kk in range(TOP_K):
        pltpu.make_async_copy(y_hbm.at[pl.ds(0, rows)], ybuf.at[slot, kk], sem.at[slot]).wait()

    gates = gate_ref[...]
    for s in range(ROW_TILE):
        sl = slice(s * LANES, (s + 1) * LANES)
        acc = gates[:, 0:1] * ybuf[slot, 0, pl.ds(s, TM, stride=ROW_TILE), :]
        for kk in range(1, TOP_K):
            acc = acc + gates[:, kk:kk + 1] * ybuf[slot, kk, pl.ds(s, TM, stride=ROW_TILE), :]
        acc_s[:, sl] = acc
    result = x_ref[...] + mod_ref[5:6] * _rms(acc_s[...], gpost_ref[...])
    if split is None:
        outs[0][...] = result
    else:
        @pl.when(i < split)
        def _():
            outs[0][...] = result

        @pl.when(i >= split)
        def _():
            outs[1][...] = result


def _combine(x, y, dest, gates, mod, grp, g_post, split):
    t, d = x.shape
    n_tiles = t // TM
    tile = pl.BlockSpec((TM, d), lambda i, g: (i, 0))
    if split is None:
        out_specs, out_shape, aliases = tile, jax.ShapeDtypeStruct((t, d), F32), {3: 0}
    else:
        out_specs = [pl.BlockSpec((TM, d), lambda i, g: (jnp.minimum(i, split - 1), 0)),
                     pl.BlockSpec((TM, d), lambda i, g: (jnp.maximum(i - split, 0), 0))]
        out_shape = [jax.ShapeDtypeStruct((split * TM, d), F32),
                     jax.ShapeDtypeStruct((t - split * TM, d), F32)]
        aliases = {}
    return pl.pallas_call(
        functools.partial(_combine_kernel, split=split),
        grid_spec=pltpu.PrefetchScalarGridSpec(
            num_scalar_prefetch=1, grid=(n_tiles,),
            in_specs=[pl.BlockSpec((TM * TOP_K,), lambda i, g: (i,), memory_space=pltpu.SMEM),
                      pl.BlockSpec((TM * TOP_K,), lambda i, g: (jnp.minimum(i + 1, n_tiles - 1),),
                                   memory_space=pltpu.SMEM),
                      tile,
                      pl.BlockSpec((TM, LANES), lambda i, g: (i, 0)),
                      pl.BlockSpec((None, 6, d), lambda i, g: (g[i], 0, 0)),
                      pl.BlockSpec((1, d), lambda i, g: (0, 0)),
                      pl.BlockSpec(memory_space=pl.ANY)],
            out_specs=out_specs,
            scratch_shapes=[pltpu.VMEM((2, TOP_K, TM * ROW_TILE, LANES), F32),
                            pltpu.SemaphoreType.DMA((2,)), pltpu.VMEM((TM, d), F32)]),
        out_shape=out_shape,
        input_output_aliases=aliases,
        compiler_params=_params("arbitrary"),
        name="moe_combine",
    )(grp, dest, dest, x, gates, mod, g_post.reshape(1, d), y)


def _moe_layer(x, mod, grp, grp_mm, layer, g_pre, g_post, w_router, b_router,
               w_gate, b_gate, w_up, b_up, w_down, b_down, split=None):
    t, d = x.shape
    n_exp = w_router.shape[1]
    assert d == ROW_TILE * LANES
    h3, route, gates, cnt = _router(x, mod, grp_mm, g_pre, w_router, b_router)
    idx = route[:, 0:TOP_K]
    rank = route[:, TOP_K:2 * TOP_K]
    counts = cnt[0, :n_exp].astype(I32)
    padded = (counts + MOE_BM - 1) // MOE_BM * MOE_BM
    pad_end = jnp.cumsum(padded)
    pad_start = pad_end - padded
    experts = jnp.arange(n_exp, dtype=I32)
    dest = (rank + jnp.sum(jnp.where(idx[..., None] == experts, pad_start, 0), axis=-1)).reshape(-1)
    n_blocks = t * TOP_K // MOE_BM + n_exp
    xs = _dispatch(h3, dest, pad_start + counts, pad_end, n_blocks * MOE_BM)
    y = _experts(xs, pad_start // MOE_BM, padded // MOE_BM, counts, layer,
                 w_gate, b_gate, w_up, b_up, w_down, b_down)
    return _combine(x, y, dest, gates, mod, grp, g_post, split)


def kernel(x_prompt, x_sample, state_hgrn, cache_k, cache_v, c, c_ctx, w_mod, b_mod, g_pre_mix, g_post_mix, g_pre_ffn, g_post_ffn, hgrn_w_in, hgrn_gamma, hgrn_g_norm, hgrn_w_out, attn_w_qkv, attn_q_gain, attn_k_gain, attn_w_o, moe_w_router, moe_b_router, moe_w_gate, moe_b_gate, moe_w_up, moe_b_up, moe_w_down, moe_b_down):
    bp, tp, d = x_prompt.shape
    bs, ts, _ = x_sample.shape
    depth = w_mod.shape[0]
    n_heads = d // HGRN_DK
    nk = N_KV_HEADS * HEAD_DIM
    assert tp % TM == 0 and ts % MM_TILE == 0 and ts % GRID_W == 0 and d % LANES == 0
    n_ctx = bp * tp
    assert n_ctx % MM_TILE == 0
    ctx_tiles, lat_tiles = tp // TM, ts // TM
    grp = _tile_groups(bp * ctx_tiles, bs, lat_tiles)
    grp_mm = _tile_groups(n_ctx // MM_TILE, bs, ts // MM_TILE)

    rows = -(-(1 + bs) // 8) * 8
    cvec = jnp.zeros((rows, d), F32).at[0].set(c_ctx).at[1:1 + bs].set(c)
    mod_all = _mod_table(cvec, w_mod, b_mod).reshape(depth, rows, 6, d)

    lb_all = jnp.cumsum(jax.nn.softmax(hgrn_gamma.astype(F32), axis=0), axis=0)

    new_states, new_k, new_v = [], [], []
    x = None
    x_parts = (x_prompt, x_sample)
    for l in range(depth):
        mod = mod_all[l]
        if l % 2 == 0:
            a = l // 2
            if x is not None:
                x_parts = (x[:n_ctx].reshape(bp, tp, d), x[n_ctx:].reshape(bs, ts, d))
            hgrn = functools.partial(_hgrn_layer, g_pre=g_pre_mix[l], g_post=g_post_mix[l], w_in=hgrn_w_in[a],
                                     lb=lb_all[l], g_norm=hgrn_g_norm[a], w_out=hgrn_w_out[a])
            assert (bp * ctx_tiles) % (lat_tiles * SEQ_PAR) == 0
            ctx_groups = bp * ctx_tiles // lat_tiles
            stream = jax.ShapeDtypeStruct((ctx_groups + bs, lat_tiles, TM, d), F32)
            stream, ctx_states = hgrn(x_parts[0], jnp.broadcast_to(mod[0], (bp, 6, d)),
                                      s0=jnp.zeros((bp, 2, n_heads, HGRN_DK, HGRN_DK), F32),
                                      stream=stream, group0=0)
            stream, _ = hgrn(x_parts[1], mod[1:1 + bs], s0=state_hgrn[:, a].astype(F32),
                             stream=stream, group0=ctx_groups)
            x = stream.reshape(n_ctx + bs * ts, d)
            new_states.append(ctx_states)
        else:
            m = l // 2
            w_qkv = attn_w_qkv[m].astype(BF16)
            q_c, k_c, v_c = _qkv(x, mod, grp, g_pre_mix[l], w_qkv, attn_q_gain[m], attn_k_gain[m],
                                 0, n_ctx // TM, 0, F32)
            q_l, k_l, v_l = _qkv(x, mod, grp, g_pre_mix[l], w_qkv, attn_q_gain[m], attn_k_gain[m],
                                 n_ctx // TM, bs * ts // TM, ts, BF16)
            o_c = _attention(q_c, k_c, v_c, bp, tp)
            past = cache_k.shape[2]
            o_l = _attention(q_l, k_l, v_l, bs, ts,
                             cache=(cache_k[:, m].reshape(bs, past, nk), cache_v[:, m].reshape(bs, past, nk)))
            new_k.append(k_c.reshape(bp, tp, N_KV_HEADS, HEAD_DIM))
            new_v.append(v_c.reshape(bp, tp, N_KV_HEADS, HEAD_DIM))
            x = _attn_out(x, o_c, o_l, mod, grp_mm, attn_w_o[m].astype(BF16), g_post_mix[l])
        x = _moe_layer(x, mod, grp, grp_mm, l, g_pre_ffn[l], g_post_ffn[l], moe_w_router[l], moe_b_router[l],
                       moe_w_gate, moe_b_gate, moe_w_up, moe_b_up, moe_w_down, moe_b_down,
                       split=bp * ctx_tiles if l == depth - 1 else None)

    y_prompt, y_sample = x
    return (y_prompt.reshape(bp, tp, d), y_sample.reshape(bs, ts, d), jnp.stack(new_states, axis=1),
            jnp.stack(new_k, axis=1), jnp.stack(new_v, axis=1))
```
